```python
import math
import jax, jax.numpy as jnp
from jax import lax
import numpy as np


D_MODEL = 1024
BATCH = 16
SEQ = 2048
DEPTH = 2
DEC_BATCH = 32
DEC_SEQ = 1
PAST_LEN = 16384
PAGE_SIZE = 128

HEAD_DIM = 64
CONV_CH = D_MODEL // 4
CONV_WIDTH = 31
RET_HEADS = D_MODEL // 256
RET_W = RET_HEADS * HEAD_DIM
MOBA_HEADS = D_MODEL // 128
MOBA_W = MOBA_HEADS * HEAD_DIM
D_MIX = CONV_CH + RET_W + MOBA_W
D_IN = 3 * CONV_CH + 4 * RET_W + 4 * MOBA_W
MOBA_BLOCK = 256
TOP_K = 3
N_BUCKETS = 32
MAX_DISTANCE = 128
RET_CHUNK = 128
Q_BLOCK = 128
ROPE_BASE = 10000.0
EPS = 1e-6

kernel_name = 'hymba_conv_retention_moba_step'


def _rmsnorm(x, g):
    xf = x.astype(jnp.float32)
    y = xf * lax.rsqrt(jnp.mean(xf * xf, axis=-1, keepdims=True) + EPS)
    return (y * g.astype(jnp.float32)).astype(x.dtype)


def _headnorm(x):
    xf = x.astype(jnp.float32)
    return (xf * lax.rsqrt(jnp.mean(xf * xf, axis=-1, keepdims=True) + EPS)).astype(x.dtype)


def _layernorm(x, g, b):
    xf = x.astype(jnp.float32)
    mu = jnp.mean(xf, axis=-1, keepdims=True)
    xc = xf - mu
    y = xc * lax.rsqrt(jnp.mean(xc * xc, axis=-1, keepdims=True) + EPS)
    return (y * g.astype(jnp.float32) + b.astype(jnp.float32)).astype(x.dtype)


def _rope(x, pos):
    half = x.shape[-1] // 2
    freqs = ROPE_BASE ** (-jnp.arange(half, dtype=jnp.float32) / half)
    ang = pos.astype(jnp.float32)[:, None] * freqs[None, :]
    cos = jnp.cos(ang)[None, :, None, :]
    sin = jnp.sin(ang)[None, :, None, :]
    xf = x.astype(jnp.float32)
    x1, x2 = xf[..., :half], xf[..., half:]
    return jnp.concatenate([x1 * cos - x2 * sin, x1 * sin + x2 * cos], axis=-1).astype(x.dtype)


def _rel_bucket(d):
    n = jnp.maximum(d, 0)
    max_exact = N_BUCKETS // 2
    large = max_exact + (jnp.log(jnp.maximum(n, 1).astype(jnp.float32) / max_exact)
                         / math.log(MAX_DISTANCE / max_exact) * (N_BUCKETS - max_exact)).astype(jnp.int32)
    large = jnp.minimum(large, N_BUCKETS - 1)
    return jnp.where(n < max_exact, n, large)


def _conv_module(a, glu, prev, w, b, ln_g, ln_b):
    u = a * jax.nn.sigmoid(glu)
    up = jnp.concatenate([prev.astype(u.dtype), u], axis=1)
    y = lax.conv_general_dilated(up, w[:, None, :].astype(u.dtype), window_strides=(1,), padding='VALID',
                                 dimension_numbers=('NWC', 'WIO', 'NWC'), feature_group_count=CONV_CH)
    y = _layernorm(y + b, ln_g, ln_b)
    return jax.nn.silu(y), up[:, -(CONV_WIDTH - 1):]


def _retention(q, k, v, state0):
    B, S, H, dk = q.shape
    C = RET_CHUNK if S % RET_CHUNK == 0 else S
    n = S // C
    lg = jnp.log(1.0 - 2.0 ** (-5.0 - jnp.arange(H, dtype=jnp.float32)))
    t = jnp.arange(C, dtype=jnp.float32)
    diff = t[:, None] - t[None, :]
    dmat = jnp.where(diff >= 0, jnp.exp(lg[:, None, None] * jnp.maximum(diff, 0.0)), 0.0)
    cross = jnp.exp(lg[None, :] * (t[:, None] + 1.0))
    wk = jnp.exp(lg[:, None] * (C - 1.0 - t[None, :]))
    chunk_decay = jnp.exp(lg * C)

    def chunks(z):
        return jnp.swapaxes(z.reshape(B, n, C, H, z.shape[-1]), 0, 1).astype(jnp.float32)

    def step(R, xs):
        qc, kc, vc = xs
        s = jnp.einsum('bqhd,bkhd->bhqk', qc, kc) * dmat[None]
        o = jnp.einsum('bhqk,bkhe->bqhe', s, vc) + jnp.einsum('bqhd,bhde->bqhe', qc, R) * cross[None, :, :, None]
        R = R * chunk_decay[None, :, None, None] + jnp.einsum('bkhd,bkhe,hk->bhde', kc, vc, wk)
        return R, o

    R, o = lax.scan(step, state0.astype(jnp.float32), (chunks(q), chunks(k), chunks(v)))
    o = jnp.swapaxes(o, 0, 1).reshape(B, S, H, v.shape[-1])
    return o.astype(q.dtype), R.astype(state0.dtype)


def _blockify(k, v):
    B, L, H, d = k.shape
    nb = max(-(-L // MOBA_BLOCK), TOP_K)
    pad = nb * MOBA_BLOCK - L
    kb = jnp.pad(k, ((0, 0), (0, pad), (0, 0), (0, 0))).reshape(B, nb, MOBA_BLOCK, H, d)
    vb = jnp.pad(v, ((0, 0), (0, pad), (0, 0), (0, 0))).reshape(B, nb, MOBA_BLOCK, H, d)
    kmean = jnp.mean(kb, axis=2, dtype=jnp.float32)
    return kb, vb, kmean


def _moba_attend(q, q_pos, kb, vb, kmean, rel_bias):
    B, Q, H, d = q.shape
    nb = kb.shape[1]
    qbs = Q_BLOCK if Q % Q_BLOCK == 0 else Q
    nqb = Q // qbs
    q_r = q.reshape(B, nqb, qbs, H, d)
    pos_r = q_pos.reshape(nqb, qbs)
    scale = HEAD_DIM ** -0.5
    hidx = jnp.arange(H)[:, None, None]
    bias_t = rel_bias.astype(jnp.float32).T
    n_slots = TOP_K + 1

    def one(args):
        b, i = args
        qc = q_r[b, i].astype(jnp.float32)
        pc = pos_r[i]
        own = pc // MOBA_BLOCK
        s = jnp.einsum('qhd,nhd->hqn', qc, kmean[b])
        past_ok = jnp.arange(nb)[None, :] < own[:, None]
        s = jnp.where(past_ok[None], s, -jnp.inf)
        _, sel = lax.top_k(s, TOP_K)
        idx = jnp.concatenate([sel, jnp.broadcast_to(own[None, :, None], (H, qbs, 1))], axis=-1)
        kg = kb[b, idx, :, hidx].astype(jnp.float32)
        vg = vb[b, idx, :, hidx].astype(jnp.float32)
        kpos = idx[..., None] * MOBA_BLOCK + jnp.arange(MOBA_BLOCK)
        logits = jnp.einsum('qhd,hqjkd->hqjk', qc, kg) * scale
        logits = logits + bias_t[hidx[..., None], _rel_bucket(pc[None, :, None, None] - kpos)]
        sel_ok = jnp.arange(TOP_K)[None, :] < own[:, None]
        valid = jnp.concatenate([
            jnp.broadcast_to(sel_ok[None, :, :, None], (H, qbs, TOP_K, MOBA_BLOCK)),
            kpos[:, :, TOP_K:] <= pc[None, :, None, None]], axis=2)
        logits = jnp.where(valid, logits, -jnp.inf)
        p = jax.nn.softmax(logits.reshape(H, qbs, n_slots * MOBA_BLOCK), axis=-1)
        p = p.reshape(H, qbs, n_slots, MOBA_BLOCK)
        return jnp.einsum('hqjk,hqjkd->qhd', p, vg).astype(q.dtype)

    bs = jnp.repeat(jnp.arange(B), nqb)
    iis = jnp.tile(jnp.arange(nqb), B)
    out = lax.map(one, (bs, iis))
    return out.reshape(B, Q, H, d)


def _mixer_layer(x, pos, conv_prev, ret_prev, k_past, v_past,
                 norm_g, w_in, conv_w, conv_b, conv_ln_g, conv_ln_b, w_out, rel_bias):
    B, S, _ = x.shape
    h = _rmsnorm(x, norm_g)
    proj = h @ w_in
    splits = [int(c) for c in np.cumsum([CONV_CH] * 3 + [RET_W] * 4 + [MOBA_W] * 3)]
    c_a, c_glu, c_gate, r_q, r_k, r_v, r_gate, m_q, m_k, m_v, m_gate = jnp.split(proj, splits, axis=-1)

    conv_out, conv_state = _conv_module(c_a, c_glu, conv_prev, conv_w, conv_b, conv_ln_g, conv_ln_b)
    conv_out = conv_out * jax.nn.silu(c_gate)

    rq = _rope(r_q.reshape(B, S, RET_HEADS, HEAD_DIM), pos)
    rk = _rope(r_k.reshape(B, S, RET_HEADS, HEAD_DIM), pos) * (HEAD_DIM ** -0.5)
    rv = r_v.reshape(B, S, RET_HEADS, HEAD_DIM)
    ro, ret_state = _retention(rq, rk, rv, ret_prev)
    ro = _headnorm(ro).reshape(B, S, RET_W) * jax.nn.silu(r_gate)

    mq = m_q.reshape(B, S, MOBA_HEADS, HEAD_DIM)
    mk = m_k.reshape(B, S, MOBA_HEADS, HEAD_DIM)
    mv = m_v.reshape(B, S, MOBA_HEADS, HEAD_DIM)
    if k_past is None:
        k_all, v_all = mk, mv
    else:
        k_all = jnp.concatenate([k_past.astype(mk.dtype), mk], axis=1)
        v_all = jnp.concatenate([v_past.astype(mv.dtype), mv], axis=1)
    kb, vb, kmean = _blockify(k_all, v_all)
    mo = _moba_attend(mq, pos, kb, vb, kmean, rel_bias).reshape(B, S, MOBA_W) * jax.nn.silu(m_gate)

    mix = jnp.concatenate([conv_out, ro, mo], axis=-1)
    y = x + mix @ w_out
    return y, mk, mv, ret_state, conv_state


def setup_inputs(seed: int = 0) -> dict:
    key = jax.random.key(seed)
    ks = jax.random.split(key, 18)
    n_pages = PAST_LEN // PAGE_SIZE
    n_used = DEC_BATCH * n_pages
    n_phys = n_used + n_used // 4
    nrm = jax.random.normal
    f32 = jnp.float32
    return {
        'x_prompt': nrm(ks[0], (BATCH, SEQ, D_MODEL), f32),
        'x_sample': nrm(ks[1], (DEC_BATCH, DEC_SEQ, D_MODEL), f32),
        'cache_k': nrm(ks[2], (DEPTH, n_phys, PAGE_SIZE, MOBA_HEADS, HEAD_DIM), f32),
        'cache_v': nrm(ks[3], (DEPTH, n_phys, PAGE_SIZE, MOBA_HEADS, HEAD_DIM), f32),
        'page_table': jax.random.permutation(ks[4], n_phys)[:n_used].reshape(DEC_BATCH, n_pages).astype(jnp.int32),
        'state_ret': 0.5 * nrm(ks[5], (DEPTH, DEC_BATCH, RET_HEADS, HEAD_DIM, HEAD_DIM), f32),
        'state_conv': 0.5 * nrm(ks[6], (DEPTH, DEC_BATCH, CONV_WIDTH - 1, CONV_CH), f32),
        'norm_g': 1.0 + 0.01 * nrm(ks[7], (DEPTH, D_MODEL), f32),
        'w_in': nrm(ks[8], (DEPTH, D_MODEL, D_IN), f32) * D_MODEL ** -0.5,
        'conv_w': nrm(ks[9], (DEPTH, CONV_WIDTH, CONV_CH), f32) * CONV_WIDTH ** -0.5,
        'conv_b': 0.01 * nrm(ks[10], (DEPTH, CONV_CH), f32),
        'conv_ln_g': 1.0 + 0.01 * nrm(ks[11], (DEPTH, CONV_CH), f32),
        'conv_ln_b': 0.01 * nrm(ks[12], (DEPTH, CONV_CH), f32),
        'w_out': nrm(ks[13], (DEPTH, D_MIX, D_MODEL), f32) * D_MIX ** -0.5,
        'rel_bias': 0.5 * nrm(ks[14], (N_BUCKETS, MOBA_HEADS), f32),
        'final_g': 1.0 + 0.01 * nrm(ks[15], (D_MODEL,), f32),
    }


def reference(x_prompt, x_sample, cache_k, cache_v, page_table, state_ret, state_conv,
              norm_g, w_in, conv_w, conv_b, conv_ln_g, conv_ln_b, w_out, rel_bias, final_g):
    B, S, _ = x_prompt.shape
    DB, T, _ = x_sample.shape
    n_pages = page_table.shape[1]
    past_len = n_pages * cache_k.shape[2]
    pos_p = jnp.arange(S, dtype=jnp.int32)
    pos_s = past_len + jnp.arange(T, dtype=jnp.int32)
    hp, hs = x_prompt, x_sample
    kp_l, vp_l, rp_l, cp_l = [], [], [], []
    ks_l, vs_l, rs_l, cs_l = [], [], [], []
    for l in range(DEPTH):
        params = (norm_g[l], w_in[l], conv_w[l], conv_b[l], conv_ln_g[l], conv_ln_b[l], w_out[l], rel_bias)
        conv0 = jnp.zeros((B, CONV_WIDTH - 1, CONV_CH), x_prompt.dtype)
        ret0 = jnp.zeros((B, RET_HEADS, HEAD_DIM, HEAD_DIM), x_prompt.dtype)
        hp, kp, vp, rp, cp = _mixer_layer(hp, pos_p, conv0, ret0, None, None, *params)
        k_past = cache_k[l, page_table].reshape(DB, past_len, MOBA_HEADS, HEAD_DIM)
        v_past = cache_v[l, page_table].reshape(DB, past_len, MOBA_HEADS, HEAD_DIM)
        hs, ksm, vsm, rsm, csm = _mixer_layer(hs, pos_s, state_conv[l], state_ret[l], k_past, v_past, *params)
        kp_l.append(kp); vp_l.append(vp); rp_l.append(rp); cp_l.append(cp)
        ks_l.append(ksm); vs_l.append(vsm); rs_l.append(rsm); cs_l.append(csm)
    y_prompt = _rmsnorm(hp, final_g)
    y_sample = _rmsnorm(hs, final_g)
    k_prompt = jnp.stack(kp_l)
    v_prompt = jnp.stack(vp_l)
    k_sample = jnp.stack(ks_l)
    v_sample = jnp.stack(vs_l)
    ret_prompt = jnp.stack(rp_l)
    ret_sample = jnp.stack(rs_l)
    conv_prompt = jnp.stack(cp_l)
    conv_sample = jnp.stack(cs_l)
    return (y_prompt, y_sample, k_prompt, v_prompt, k_sample, v_sample, ret_prompt, ret_sample, conv_prompt, conv_sample)
```

```python
import functools
import math

import jax
import jax.numpy as jnp
from jax import lax
from jax.experimental import pallas as pl
from jax.experimental.pallas import tpu as pltpu

F32 = jnp.float32
BF16 = jnp.bfloat16

HEAD_DIM = 64
CONV_WIDTH = 31
MOBA_BLOCK = 256
TOP_K = 3
N_BUCKETS = 32
MAX_DISTANCE = 128
RET_CHUNK = 128
ROPE_BASE = 10000.0
EPS = 1e-6

LANES = 128
PAIR = LANES // HEAD_DIM
MASKED = -1e30
VMEM_LIMIT = 56 * 1024 * 1024
_CONTRACT_LAST = (((1,), (1,)), ((), ()))
_CONTRACT_FIRST = (((0,), (0,)), ((), ()))


def _silu(x):
    return x * jax.nn.sigmoid(x)


def _params(*sem):
    return pltpu.CompilerParams(dimension_semantics=sem, vmem_limit_bytes=VMEM_LIMIT)


def _inproj_kernel(x_ref, g_ref, w_ref, conv_ref, ret_ref, q_ref, k_ref, v_ref, gate_ref, *km_refs,
                   splits, n_mean):
    x = x_ref[...]
    h = x * lax.rsqrt(jnp.mean(x * x, axis=-1, keepdims=True) + EPS) * g_ref[...]
    hb = h.astype(BF16)
    outs = (conv_ref, ret_ref, q_ref, k_ref, v_ref, gate_ref)
    for (c0, c1), o_ref in zip(splits, outs):
        o_ref[...] = jnp.dot(hb, w_ref[:, c0:c1], preferred_element_type=F32)
    if n_mean:
        km_ref, = km_refs
        for i in range(n_mean):
            blk = k_ref[i * MOBA_BLOCK:(i + 1) * MOBA_BLOCK, :]
            km_ref[0, i:i + 1, :] = jnp.sum(blk, axis=0, keepdims=True) * (1.0 / MOBA_BLOCK)


def _inproj(x2d, g, w_bf16, *, tm, with_kmean):
    m, d = x2d.shape
    conv_ch, ret_w, moba_w = d // 4, (d // 256) * HEAD_DIM, (d // 128) * HEAD_DIM
    widths = (3 * conv_ch, 4 * ret_w, moba_w, moba_w, moba_w, moba_w)
    edges = [0]
    for wd in widths:
        edges.append(edges[-1] + wd)
    splits = tuple(zip(edges[:-1], edges[1:]))
    assert edges[-1] == w_bf16.shape[1] and m % tm == 0
    n_mean = tm // MOBA_BLOCK if with_kmean else 0
    out_shape = [jax.ShapeDtypeStruct((m, wd), F32) for wd in widths]
    out_specs = [pl.BlockSpec((tm, wd), lambda i: (i, 0)) for wd in widths]
    if with_kmean:
        assert tm % MOBA_BLOCK == 0
        out_shape.append(jax.ShapeDtypeStruct((m // tm, n_mean, moba_w), F32))
        out_specs.append(pl.BlockSpec((1, n_mean, moba_w), lambda i: (i, 0, 0)))
    return pl.pallas_call(
        functools.partial(_inproj_kernel, splits=splits, n_mean=n_mean),
        grid=(m // tm,),
        in_specs=[pl.BlockSpec((tm, d), lambda i: (i, 0)),
                  pl.BlockSpec((1, d), lambda i: (0, 0)),
                  pl.BlockSpec(w_bf16.shape, lambda i: (0, 0))],
        out_specs=out_specs,
        out_shape=out_shape,
        compiler_params=_params("parallel"),
        name="inproj",
    )(x2d, g.reshape(1, d), w_bf16)


def _outproj_kernel(x_ref, c_ref, r_ref, m_ref, w_ref, fg_ref, y_ref, *, conv_ch, ret_w, final_norm):
    y = x_ref[...]
    y += jnp.dot(c_ref[...].astype(BF16), w_ref[0:conv_ch, :], preferred_element_type=F32)
    y += jnp.dot(r_ref[...].astype(BF16), w_ref[conv_ch:conv_ch + ret_w, :], preferred_element_type=F32)
    y += jnp.dot(m_ref[...].astype(BF16), w_ref[conv_ch + ret_w:, :], preferred_element_type=F32)
    if final_norm:
        y = y * lax.rsqrt(jnp.mean(y * y, axis=-1, keepdims=True) + EPS) * fg_ref[...]
    y_ref[...] = y


def _outproj(x2d, conv_out, ret_out, moba_out, w_bf16, final_g, *, tm, final_norm):
    m, d = x2d.shape
    conv_ch, ret_w, moba_w = conv_out.shape[1], ret_out.shape[1], moba_out.shape[1]
    row = lambda wd: pl.BlockSpec((tm, wd), lambda i: (i, 0))
    return pl.pallas_call(
        functools.partial(_outproj_kernel, conv_ch=conv_ch, ret_w=ret_w, final_norm=final_norm),
        grid=(m // tm,),
        in_specs=[row(d), row(conv_ch), row(ret_w), row(moba_w),
                  pl.BlockSpec(w_bf16.shape, lambda i: (0, 0)),
                  pl.BlockSpec((1, d), lambda i: (0, 0))],
        out_specs=row(d),
        out_shape=jax.ShapeDtypeStruct((m, d), F32),
        compiler_params=_params("parallel"),
        name="outproj",
    )(x2d, conv_out, ret_out, moba_out, w_bf16, final_g.reshape(1, d))


def _layernorm_silu(y, g, b):
    mu = jnp.mean(y, axis=-1, keepdims=True)
    yc = y - mu
    yn = yc * lax.rsqrt(jnp.mean(yc * yc, axis=-1, keepdims=True) + EPS)
    return _silu(yn * g + b)


CONV_PAD = 32
CONV_ROWS = 64


def _conv_kernel(in_ref, prev_ref, w_ref, b_ref, g_ref, beta_ref, o_ref, st_ref, buf, *, ts, ch):
    t = pl.program_id(1)
    keep = CONV_WIDTH - 1
    lo = CONV_PAD - keep

    @pl.when(t == 0)
    def _():
        buf[lo:CONV_PAD, :] = prev_ref[0]

    @pl.when(t > 0)
    def _():
        buf[lo:CONV_PAD, :] = buf[ts + lo:ts + CONV_PAD, :]

    buf[CONV_PAD:CONV_PAD + ts, :] = in_ref[:, 0:ch] * jax.nn.sigmoid(in_ref[:, ch:2 * ch])
    for r0 in range(0, ts, CONV_ROWS):
        acc = jnp.zeros((CONV_ROWS, ch), F32)
        for j in range(CONV_WIDTH):
            acc += buf[lo + r0 + j:lo + r0 + j + CONV_ROWS, :] * w_ref[j:j + 1, :]
        y = _layernorm_silu(acc + b_ref[...], g_ref[...], beta_ref[...])
        o_ref[r0:r0 + CONV_ROWS, :] = y * _silu(in_ref[r0:r0 + CONV_ROWS, 2 * ch:3 * ch])

    @pl.when(t == pl.num_programs(1) - 1)
    def _():
        st_ref[0] = buf[ts + lo:ts + CONV_PAD, :]


def _conv_prompt(conv_in, prev, w, b, g, beta, *, batch, seq, ts):
    ch = conv_in.shape[1] // 3
    nts = seq // ts
    assert seq % ts == 0 and ts % CONV_ROWS == 0 and ts >= CONV_PAD
    vec = lambda: pl.BlockSpec((1, ch), lambda bi, t: (0, 0))
    return pl.pallas_call(
        functools.partial(_conv_kernel, ts=ts, ch=ch),
        grid=(batch, nts),
        in_specs=[pl.BlockSpec((ts, 3 * ch), lambda bi, t: (bi * nts + t, 0)),
                  pl.BlockSpec((1, CONV_WIDTH - 1, ch), lambda bi, t: (bi, 0, 0)),
                  pl.BlockSpec((CONV_WIDTH, ch), lambda bi, t: (0, 0)),
                  vec(), vec(), vec()],
        out_specs=[pl.BlockSpec((ts, ch), lambda bi, t: (bi * nts + t, 0)),
                   pl.BlockSpec((1, CONV_WIDTH - 1, ch), lambda bi, t: (bi, 0, 0))],
        out_shape=[jax.ShapeDtypeStruct((batch * seq, ch), F32),
                   jax.ShapeDtypeStruct((batch, CONV_WIDTH - 1, ch), F32)],
        scratch_shapes=[pltpu.VMEM((ts + CONV_PAD, ch), F32)],
        compiler_params=_params("parallel", "arbitrary"),
        name="conv_prompt",
    )(conv_in, prev, w, b.reshape(1, ch), g.reshape(1, ch), beta.reshape(1, ch))


def _conv_step_kernel(in_ref, prev_ref, w_ref, b_ref, g_ref, beta_ref, o_ref, st_ref, *, ch):
    keep = CONV_WIDTH - 1
    row = in_ref[0]
    u = row[:, 0:ch] * jax.nn.sigmoid(row[:, ch:2 * ch])
    prev = prev_ref[0]
    acc = jnp.sum(prev * w_ref[0:keep, :], axis=0, keepdims=True) + u * w_ref[keep:keep + 1, :]
    y = _layernorm_silu(acc + b_ref[...], g_ref[...], beta_ref[...])
    o_ref[0] = y * _silu(row[:, 2 * ch:3 * ch])
    st_ref[0, 0:keep - 1, :] = prev_ref[0, 1:keep, :]
    st_ref[0, keep - 1:keep, :] = u


def _conv_sample(conv_in, prev, w, b, g, beta):
    db, ch3 = conv_in.shape
    ch = ch3 // 3
    keep = CONV_WIDTH - 1
    vec = lambda: pl.BlockSpec((1, ch), lambda bi: (0, 0))
    out, st = pl.pallas_call(
        functools.partial(_conv_step_kernel, ch=ch),
        grid=(db,),
        in_specs=[pl.BlockSpec((1, 1, ch3), lambda bi: (bi, 0, 0)),
                  pl.BlockSpec((1, keep, ch), lambda bi: (bi, 0, 0)),
                  pl.BlockSpec((CONV_WIDTH, ch), lambda bi: (0, 0)),
                  vec(), vec(), vec()],
        out_specs=[pl.BlockSpec((1, 1, ch), lambda bi: (bi, 0, 0)),
                   pl.BlockSpec((1, keep, ch), lambda bi: (bi, 0, 0))],
        out_shape=[jax.ShapeDtypeStruct((db, 1, ch), F32),
                   jax.ShapeDtypeStruct((db, keep, ch), F32)],
        compiler_params=_params("parallel"),
        name="conv_sample",
    )(conv_in.reshape(db, 1, ch3), prev, w, b.reshape(1, ch), g.reshape(1, ch), beta.reshape(1, ch))
    return out.reshape(db, ch), st


def _rope_rows(x, cos, sin_signed):
    w = x.shape[-1]
    half = HEAD_DIM // 2
    lane = lax.broadcasted_iota(jnp.int32, (1, w), 1)
    partner = jnp.where((lane % HEAD_DIM) < half, pltpu.roll(x, w - half, 1), pltpu.roll(x, half, 1))
    return x * cos + partner * sin_signed


def _half_sums(x, first_half):
    a = jnp.sum(jnp.where(first_half, x, 0.0), axis=-1, keepdims=True)
    b = jnp.sum(jnp.where(first_half, 0.0, x), axis=-1, keepdims=True)
    return jnp.where(first_half, a, b)


def _ret_kernel(in_ref, r0_ref, cos_ref, sin_ref, dmat_ref, cross_ref, wk_ref, decay_ref,
                o_ref, rout_ref, r_scr, *, rw):
    c = pl.program_id(1)
    npair = rw // LANES

    @pl.when(c == 0)
    def _():
        r_scr[...] = r0_ref[0]

    q = _rope_rows(in_ref[:, 0:rw], cos_ref[...], sin_ref[...])
    k = _rope_rows(in_ref[:, rw:2 * rw], cos_ref[...], sin_ref[...]) * (HEAD_DIM ** -0.5)
    lane = lax.broadcasted_iota(jnp.int32, (1, LANES), 1)
    first = lane < HEAD_DIM
    row = lax.broadcasted_iota(jnp.int32, (LANES, 1), 0)
    blockdiag = (row < HEAD_DIM) == first
    for p in range(npair):
        sl = slice(p * LANES, (p + 1) * LANES)
        qp, kp = q[:, sl], k[:, sl]
        vp = in_ref[:, 2 * rw + p * LANES:2 * rw + (p + 1) * LANES]
        gp = in_ref[:, 3 * rw + p * LANES:3 * rw + (p + 1) * LANES]
        kb, vb = kp.astype(BF16), vp.astype(BF16)
        o = jnp.dot(qp.astype(BF16), r_scr[p].astype(BF16), preferred_element_type=F32) * cross_ref[p]
        halves = []
        for hh in range(PAIR):
            own = first if hh == 0 else jnp.logical_not(first)
            qh = jnp.where(own, qp, 0.0).astype(BF16)
            s = lax.dot_general(qh, kb, _CONTRACT_LAST, preferred_element_type=F32) * dmat_ref[PAIR * p + hh]
            halves.append(jnp.dot(s.astype(BF16), vb, preferred_element_type=F32))
        o = o + jnp.where(first, halves[0], halves[1])
        kv = lax.dot_general((kp * wk_ref[p]).astype(BF16), vb, _CONTRACT_FIRST, preferred_element_type=F32)
        r_scr[p] = r_scr[p] * decay_ref[p] + jnp.where(blockdiag, kv, 0.0)
        ms = _half_sums(o * o, first) * (1.0 / HEAD_DIM)
        o_ref[:, sl] = o * lax.rsqrt(ms + EPS) * _silu(gp)

    @pl.when(c == pl.num_programs(1) - 1)
    def _():
        rout_ref[0] = r_scr[...]


def _ret_tables(heads):
    c = RET_CHUNK
    lg = jnp.log(1.0 - 2.0 ** (-5.0 - jnp.arange(heads, dtype=F32)))
    t = jnp.arange(c, dtype=F32)
    diff = t[:, None] - t[None, :]
    dmat = jnp.where(diff >= 0, jnp.exp(lg[:, None, None] * jnp.maximum(diff, 0.0)), 0.0)
    cross = jnp.exp(lg[None, :] * (t[:, None] + 1.0))
    wk = jnp.exp(lg[:, None] * (c - 1.0 - t[None, :]))
    chunk_decay = jnp.exp(lg * c)
    per_lane = lambda a: jnp.repeat(a, HEAD_DIM, axis=-1)
    to_pairs = lambda a: jnp.swapaxes(a.reshape(c, heads // PAIR, LANES), 0, 1)
    cross_p = to_pairs(per_lane(cross))
    wk_p = to_pairs(per_lane(wk.T))
    decay_p = to_pairs(per_lane(jnp.broadcast_to(chunk_decay[None, :], (c, heads))))
    return dmat, cross_p, wk_p, decay_p


def _blockdiag_pairs(state):
    b, h = state.shape[:2]
    s = state.reshape(b, h // PAIR, PAIR, HEAD_DIM, HEAD_DIM)
    z = jnp.zeros_like(s[:, :, 0])
    top = jnp.concatenate([s[:, :, 0], z], axis=-1)
    bot = jnp.concatenate([z, s[:, :, 1]], axis=-1)
    return jnp.concatenate([top, bot], axis=-2)


def _unpair_states(pairs):
    a = pairs[:, :, :HEAD_DIM, :HEAD_DIM]
    b = pairs[:, :, HEAD_DIM:, HEAD_DIM:]
    return jnp.stack([a, b], axis=2).reshape(pairs.shape[0], -1, HEAD_DIM, HEAD_DIM)


def _ret_prompt(ret_in, state0_pairs, cos_t, sin_t, tables, *, batch, seq):
    rw = ret_in.shape[1] // 4
    npair = rw // LANES
    c = RET_CHUNK
    nc = seq // c
    assert seq % c == 0
    dmat, cross_p, wk_p, decay_p = tables
    const = lambda a: pl.BlockSpec(a.shape, lambda bi, ci: (0,) * a.ndim)
    out, rout = pl.pallas_call(
        functools.partial(_ret_kernel, rw=rw),
        grid=(batch, nc),
        in_specs=[pl.BlockSpec((c, 4 * rw), lambda bi, ci: (bi * nc + ci, 0)),
                  pl.BlockSpec((1, npair, LANES, LANES), lambda bi, ci: (bi, 0, 0, 0)),
                  pl.BlockSpec((c, rw), lambda bi, ci: (ci, 0)),
                  pl.BlockSpec((c, rw), lambda bi, ci: (ci, 0)),
                  const(dmat), const(cross_p), const(wk_p), const(decay_p)],
        out_specs=[pl.BlockSpec((c, rw), lambda bi, ci: (bi * nc + ci, 0)),
                   pl.BlockSpec((1, npair, LANES, LANES), lambda bi, ci: (bi, 0, 0, 0))],
        out_shape=[jax.ShapeDtypeStruct((batch * seq, rw), F32),
                   jax.ShapeDtypeStruct((batch, npair, LANES, LANES), F32)],
        scratch_shapes=[pltpu.VMEM((npair, LANES, LANES), F32)],
        compiler_params=_params("parallel", "arbitrary"),
        name="ret_prompt",
    )(ret_in, state0_pairs, cos_t, sin_t, dmat, cross_p, wk_p, decay_p)
    return out, rout


def _ret_step_kernel(in_ref, r_ref, cos_ref, sin_ref, decay_ref, o_ref, rout_ref, *, rw):
    heads = rw // HEAD_DIM
    row = in_ref[0]
    q = _rope_rows(row[:, 0:rw], cos_ref[...], sin_ref[...])
    k = _rope_rows(row[:, rw:2 * rw], cos_ref[...], sin_ref[...]) * (HEAD_DIM ** -0.5)
    v = row[:, 2 * rw:3 * rw]
    gate = row[:, 3 * rw:4 * rw]
    eye = (lax.broadcasted_iota(jnp.int32, (HEAD_DIM, HEAD_DIM), 0)
           == lax.broadcasted_iota(jnp.int32, (HEAD_DIM, HEAD_DIM), 1))
    col = lambda r: jnp.sum(jnp.where(eye, r, 0.0), axis=-1, keepdims=True)
    outs = []
    for h in range(heads):
        sl = slice(h * HEAD_DIM, (h + 1) * HEAD_DIM)
        qh, kh, vh = q[:, sl], k[:, sl], v[:, sl]
        decay = decay_ref[h:h + 1, :]
        state = r_ref[0, h]
        o = jnp.sum(qh * kh, axis=-1, keepdims=True) * vh
        o = o + jnp.sum(col(qh) * state, axis=0, keepdims=True) * decay
        rout_ref[0, h] = state * decay + col(kh) * vh
        o = o * lax.rsqrt(jnp.mean(o * o, axis=-1, keepdims=True) + EPS)
        outs.append(o * _silu(gate[:, sl]))
    o_ref[0] = jnp.concatenate(outs, axis=-1)


def _ret_sample(ret_in, state, cos_row, sin_row, decay_rows):
    db, rw4 = ret_in.shape
    rw = rw4 // 4
    heads = rw // HEAD_DIM
    out, rout = pl.pallas_call(
        functools.partial(_ret_step_kernel, rw=rw),
        grid=(db,),
        in_specs=[pl.BlockSpec((1, 1, rw4), lambda bi: (bi, 0, 0)),
                  pl.BlockSpec((1, heads, HEAD_DIM, HEAD_DIM), lambda bi: (bi, 0, 0, 0)),
                  pl.BlockSpec((1, rw), lambda bi: (0, 0)),
                  pl.BlockSpec((1, rw), lambda bi: (0, 0)),
                  pl.BlockSpec((heads, HEAD_DIM), lambda bi: (0, 0))],
        out_specs=[pl.BlockSpec((1, 1, rw), lambda bi: (bi, 0, 0)),
                   pl.BlockSpec((1, heads, HEAD_DIM, HEAD_DIM), lambda bi: (bi, 0, 0, 0))],
        out_shape=[jax.ShapeDtypeStruct((db, 1, rw), F32),
                   jax.ShapeDtypeStruct(state.shape, F32)],
        compiler_params=_params("parallel"),
        name="ret_sample",
    )(ret_in.reshape(db, 1, rw4), state, cos_row, sin_row, decay_rows)
    return out.reshape(db, rw), rout


def _rope_tables(pos, heads):
    half = HEAD_DIM // 2
    freqs = ROPE_BASE ** (-jnp.arange(half, dtype=F32) / half)
    ang = pos.astype(F32)[:, None] * freqs[None, :]
    cos, sin = jnp.cos(ang), jnp.sin(ang)
    cos_t = jnp.tile(jnp.concatenate([cos, cos], axis=-1), (1, heads))
    sin_t = jnp.tile(jnp.concatenate([-sin, sin], axis=-1), (1, heads))
    return cos_t, sin_t


def _rel_bucket(d):
    n = jnp.maximum(d, 0)
    max_exact = N_BUCKETS // 2
    large = max_exact + (jnp.log(jnp.maximum(n, 1).astype(F32) / max_exact)
                         / math.log(MAX_DISTANCE / max_exact) * (N_BUCKETS - max_exact)).astype(jnp.int32)
    large = jnp.minimum(large, N_BUCKETS - 1)
    return jnp.where(n < max_exact, n, large)


def _block_rank_penalty(scores, past):
    nb = scores.shape[0]
    blk = lax.broadcasted_iota(jnp.int32, scores.shape, 0)
    sm = jnp.where(past, scores, -jnp.inf)
    rank = jnp.zeros(scores.shape, F32)
    for m in range(nb):
        other = sm[m:m + 1, :]
        beats = (other > sm) | ((other == sm) & (m < blk))
        rank += jnp.where(beats, 1.0, 0.0)
    return jnp.where(past & (rank < TOP_K), 0.0, MASKED)


def _moba_kernel(q_ref, k_ref, v_ref, km_ref, gate_ref, bias_ref, o_ref, m_scr, l_scr, acc_scr, *, nb):
    t = pl.program_id(2)
    tq = MOBA_BLOCK
    lane = lax.broadcasted_iota(jnp.int32, (1, LANES), 1)
    q2 = q_ref[...]
    km = km_ref[0]
    past = lax.broadcasted_iota(jnp.int32, (nb, tq), 0) < t

    def attend(hh, own, q_ext, blk, bias, first):
        start = pl.multiple_of(blk * MOBA_BLOCK, MOBA_BLOCK)
        k2 = k_ref[pl.ds(start, MOBA_BLOCK), :]
        v2 = v_ref[pl.ds(start, MOBA_BLOCK), :]
        if first:
            other_half = 0.0
        else:
            flag_lane = blk if hh == 1 else blk + HEAD_DIM
            other_half = jnp.where(lane == flag_lane, 1.0, 0.0)
        k_ext = jnp.where(own, k2, other_half).astype(BF16)
        s = lax.dot_general(q_ext, k_ext, _CONTRACT_LAST, preferred_element_type=F32) + bias
        s0, s1 = s[:, :LANES], s[:, LANES:]
        row_max = jnp.max(jnp.maximum(s0, s1), axis=-1, keepdims=True)
        if first:
            m_new = jnp.broadcast_to(row_max, (tq, LANES))
        else:
            m_prev = m_scr[hh]
            m_new = jnp.maximum(m_prev, row_max)
            alpha = jnp.exp(m_prev - m_new)
        p0, p1 = jnp.exp(s0 - m_new), jnp.exp(s1 - m_new)
        pv = jnp.dot(jnp.concatenate([p0, p1], axis=1).astype(BF16), v2.astype(BF16),
                     preferred_element_type=F32)
        if first:
            l_scr[hh] = p0 + p1
            acc_scr[hh] = pv
        else:
            l_scr[hh] = alpha * l_scr[hh] + (p0 + p1)
            acc_scr[hh] = alpha * acc_scr[hh] + pv
        m_scr[hh] = m_new

    for hh in range(PAIR):
        own = (lane < HEAD_DIM) if hh == 0 else (lane >= HEAD_DIM)
        scores = lax.dot_general(jnp.where(own, km, 0.0), q2, _CONTRACT_LAST,
                                 precision=lax.Precision.HIGHEST, preferred_element_type=F32)
        pen = _block_rank_penalty(scores, past)
        pad = jnp.zeros((HEAD_DIM, tq), F32)
        tail = jnp.zeros((HEAD_DIM - nb, tq), F32)
        pen_rows = [pad, pen, tail] if hh == 0 else [pen, tail, pad]
        pen_t = jnp.transpose(jnp.concatenate(pen_rows, axis=0))
        q_ext = jnp.where(own, q2 * (HEAD_DIM ** -0.5), pen_t).astype(BF16)
        attend(hh, own, q_ext, t, bias_ref[hh, 0], True)

        def body(i, carry, hh=hh, own=own, q_ext=q_ext):
            attend(hh, own, q_ext, t - i, bias_ref[hh, jnp.minimum(i, 2)], False)
            return carry

        lax.fori_loop(1, t + 1, body, 0)

    outs = [acc_scr[hh] / jnp.sum(l_scr[hh], axis=-1, keepdims=True) for hh in range(PAIR)]
    o_ref[...] = jnp.where(lane < HEAD_DIM, outs[0], outs[1]) * _silu(gate_ref[...])


def _moba_bias_tiles(rel_bias):
    i = jnp.arange(MOBA_BLOCK)[:, None]
    j = jnp.arange(MOBA_BLOCK)[None, :]
    bias_t = rel_bias.astype(F32).T
    tiles = []
    for dist in range(3):
        d = dist * MOBA_BLOCK + i - j
        tile = bias_t[:, _rel_bucket(d)]
        if dist == 0:
            tile = jnp.where((d >= 0)[None], tile, -jnp.inf)
        tiles.append(tile)
    return jnp.stack(tiles, axis=1)


def _moba_prompt(mq, mk, mv, kmean, mgate, bias_tiles, *, batch, seq):
    mw = mq.shape[1]
    npair = mw // LANES
    nq = seq // MOBA_BLOCK
    nb = kmean.shape[1]
    assert seq % MOBA_BLOCK == 0 and nb % 8 == 0 and nq <= nb <= HEAD_DIM
    assert MOBA_BLOCK + 1 >= MAX_DISTANCE
    tile = lambda: pl.BlockSpec((MOBA_BLOCK, LANES), lambda b, p, t: (b * nq + t, p))
    full = lambda: pl.BlockSpec((seq, LANES), lambda b, p, t: (b, p))
    return pl.pallas_call(
        functools.partial(_moba_kernel, nb=nb),
        grid=(batch, npair, nq),
        in_specs=[tile(), full(), full(),
                  pl.BlockSpec((1, nb, LANES), lambda b, p, t: (b, 0, p)),
                  tile(),
                  pl.BlockSpec((PAIR, 3, MOBA_BLOCK, MOBA_BLOCK), lambda b, p, t: (p, 0, 0, 0))],
        out_specs=tile(),
        out_shape=jax.ShapeDtypeStruct((batch * seq, mw), F32),
        scratch_shapes=[pltpu.VMEM((PAIR, MOBA_BLOCK, LANES), F32)] * 3,
        compiler_params=_params("parallel", "parallel", "arbitrary"),
        name="moba_prompt",
    )(mq, mk, mv, kmean, mgate, bias_tiles)


PAGES_PER_STEP = 8


def _page_score_kernel(pt_ref, q_ref, *refs, pages_per_block, heads):
    page_refs, s_ref = refs[:PAGES_PER_STEP], refs[PAGES_PER_STEP]
    g = pl.program_id(1)

    @pl.when(g == 0)
    def _():
        s_ref[...] = jnp.zeros(s_ref.shape, F32)

    q8 = q_ref[0]
    lane = lax.broadcasted_iota(jnp.int32, (1, LANES), 1)
    blocks_per_step = PAGES_PER_STEP // pages_per_block
    for bi in range(blocks_per_step):
        total = jnp.zeros((heads, HEAD_DIM), F32)
        for pi in range(pages_per_block):
            page = page_refs[bi * pages_per_block + pi]

            def add_token(r, acc, page=page):
                return acc + page[0, pl.ds(pl.multiple_of(r * heads, heads), heads), :]

            total = lax.fori_loop(0, page.shape[1] // heads, add_token, total, unroll=8)
        score = jnp.sum(total * (1.0 / MOBA_BLOCK) * q8, axis=-1, keepdims=True)
        blk = g * blocks_per_step + bi
        s_ref[0] += jnp.where(lane == blk, score, 0.0)


def _page_scores(pt_flat, q8, cache_pages, *, n_pages, page_size, page0):
    db, heads, _ = q8.shape
    pages_per_block = MOBA_BLOCK // page_size
    assert MOBA_BLOCK % page_size == 0 and PAGES_PER_STEP % pages_per_block == 0
    assert n_pages % PAGES_PER_STEP == 0 and n_pages // pages_per_block <= LANES
    rows = cache_pages.shape[1]
    steps = n_pages // PAGES_PER_STEP

    def page_spec(i):
        return pl.BlockSpec((1, rows, HEAD_DIM),
                            lambda b, g, pt: (page0 + pt[b * n_pages + g * PAGES_PER_STEP + i], 0, 0))

    grid_spec = pltpu.PrefetchScalarGridSpec(
        num_scalar_prefetch=1,
        grid=(db, steps),
        in_specs=[pl.BlockSpec((1, heads, HEAD_DIM), lambda b, g, pt: (b, 0, 0))]
                 + [page_spec(i) for i in range(PAGES_PER_STEP)],
        out_specs=pl.BlockSpec((1, heads, LANES), lambda b, g, pt: (b, 0, 0)),
    )
    return pl.pallas_call(
        functools.partial(_page_score_kernel, pages_per_block=pages_per_block, heads=heads),
        grid_spec=grid_spec,
        out_shape=jax.ShapeDtypeStruct((db, heads, LANES), F32),
        compiler_params=_params("parallel", "arbitrary"),
        name="page_scores",
    )(pt_flat, q8, *([cache_pages] * PAGES_PER_STEP))


def _topk_kernel(s_ref, sel_ref, *, n_blocks):
    s = s_ref[0]
    lane = lax.broadcasted_iota(jnp.int32, s.shape, 1).astype(F32)
    s = jnp.where(lane < n_blocks, s, -jnp.inf)
    sel = jnp.zeros(s.shape, F32)
    for j in range(TOP_K):
        best = jnp.max(s, axis=-1, keepdims=True)
        idx = jnp.min(jnp.where(s == best, lane, float(LANES)), axis=-1, keepdims=True)
        sel = jnp.where(lane == j, idx, sel)
        s = jnp.where(lane == idx, -jnp.inf, s)
    sel_ref[0] = sel.astype(jnp.int32)


def _topk_blocks(scores, *, n_blocks):
    db, heads, _ = scores.shape
    assert n_blocks >= TOP_K
    spec = pl.BlockSpec((1, heads, LANES), lambda b: (b, 0, 0))
    return pl.pallas_call(
        functools.partial(_topk_kernel, n_blocks=n_blocks),
        grid=(db,),
        in_specs=[spec],
        out_specs=spec,
        out_shape=jax.ShapeDtypeStruct((db, heads, LANES), jnp.int32),
        compiler_params=_params("parallel"),
        name="topk_blocks",
    )(scores)


def _page_attend_kernel(pt_ref, seqp_ref, q_ref, kn_ref, vn_ref, gate_ref, bown_ref, kp_ref, vp_ref, bias_ref,
                        o_ref, m_scr, l_scr, acc_scr, *, heads, n_sel_pages):
    h = pl.program_id(1)
    j = pl.program_id(2)
    sub = lax.broadcasted_iota(jnp.int32, (heads, 1), 0)
    mine = sub == h

    q8 = q_ref[0] * (HEAD_DIM ** -0.5)
    s_all = lax.dot_general(q8.astype(BF16), kp_ref[0].astype(BF16), _CONTRACT_LAST,
                            preferred_element_type=F32)
    s = jnp.sum(jnp.where(mine, s_all, 0.0), axis=0, keepdims=True) + bias_ref[0]
    page_max = jnp.max(s, axis=-1, keepdims=True)

    @pl.when(j == 0)
    def _():
        m_scr[...] = jnp.full(m_scr.shape, -jnp.inf, F32)
        l_scr[...] = jnp.zeros(l_scr.shape, F32)
        acc_scr[...] = jnp.zeros(acc_scr.shape, F32)

    m_prev = m_scr[...]
    m_new = jnp.maximum(m_prev, page_max)
    alpha = jnp.exp(m_prev - m_new)
    p = jnp.exp(s - m_new)
    l_new = alpha * l_scr[...] + jnp.sum(p, axis=-1, keepdims=True)
    p_rows = jnp.broadcast_to(p, (heads, p.shape[1])).astype(BF16)
    pv = jnp.dot(p_rows, vp_ref[0].astype(BF16), preferred_element_type=F32)
    acc_new = alpha * acc_scr[...] + pv[0:1, :]
    m_scr[...] = m_new
    l_scr[...] = l_new
    acc_scr[...] = acc_new

    @pl.when(j == n_sel_pages - 1)
    def _():
        qh = jnp.sum(jnp.where(mine, q8, 0.0), axis=0, keepdims=True)
        kh = jnp.sum(jnp.where(mine, kn_ref[0], 0.0), axis=0, keepdims=True)
        vh = jnp.sum(jnp.where(mine, vn_ref[0], 0.0), axis=0, keepdims=True)
        gh = jnp.sum(jnp.where(mine, gate_ref[0], 0.0), axis=0, keepdims=True)
        bown = jnp.sum(jnp.where(mine, bown_ref[...], 0.0), axis=0, keepdims=True)
        s_own = jnp.sum(qh * kh, axis=-1, keepdims=True) + bown[:, 0:1]
        m_fin = jnp.maximum(m_new, s_own)
        a = jnp.exp(m_new - m_fin)
        p_own = jnp.exp(s_own - m_fin)
        l_fin = a * l_new + p_own
        acc = a * acc_new + p_own * vh
        o_ref[0, pl.ds(h, 1), :] = acc / l_fin * _silu(gh)


def _page_attend(pt_flat, seq_pages, q8, kn8, vn8, gate8, bown, cache_k_pages, cache_v_pages, bias_pages,
                 *, n_sel_pages, n_pages, page0):
    db, heads, _ = q8.shape
    rows = cache_k_pages.shape[1]
    per_b = lambda: pl.BlockSpec((1, heads, HEAD_DIM), lambda b, h, j, pt, sp: (b, 0, 0))

    def seq_page(b, h, j, sp):
        return sp[(b * heads + h) * n_sel_pages + j]

    page = lambda: pl.BlockSpec(
        (1, rows, HEAD_DIM), lambda b, h, j, pt, sp: (page0 + pt[b * n_pages + seq_page(b, h, j, sp)], 0, 0))
    grid_spec = pltpu.PrefetchScalarGridSpec(
        num_scalar_prefetch=2,
        grid=(db, heads, n_sel_pages),
        in_specs=[per_b(), per_b(), per_b(), per_b(),
                  pl.BlockSpec((heads, HEAD_DIM), lambda b, h, j, pt, sp: (0, 0)),
                  page(), page(),
                  pl.BlockSpec((1, 1, rows), lambda b, h, j, pt, sp: (seq_page(b, h, j, sp) * heads + h, 0, 0))],
        out_specs=per_b(),
        scratch_shapes=[pltpu.VMEM((1, 1), F32), pltpu.VMEM((1, 1), F32), pltpu.VMEM((1, HEAD_DIM), F32)],
    )
    return pl.pallas_call(
        functools.partial(_page_attend_kernel, heads=heads, n_sel_pages=n_sel_pages),
        grid_spec=grid_spec,
        out_shape=jax.ShapeDtypeStruct((db, heads, HEAD_DIM), F32),
        compiler_params=_params("parallel", "arbitrary", "arbitrary"),
        name="page_attend",
    )(pt_flat, seq_pages, q8, kn8, vn8, gate8, bown, cache_k_pages, cache_v_pages, bias_pages)


def _sample_bias_pages(rel_bias, *, past_len, page_size):
    heads = rel_bias.shape[1]
    n_pages = past_len // page_size
    kpos = jnp.arange(past_len).reshape(n_pages, page_size)
    bias = rel_bias.astype(F32).T[:, _rel_bucket(past_len - kpos)]
    bias = jnp.swapaxes(bias, 0, 1)
    same = jnp.arange(heads)[:, None] == jnp.arange(heads)[None, :]
    rows = jnp.where(same[None, :, None, :], bias[:, :, :, None], -jnp.inf)
    return rows.reshape(n_pages * heads, 1, page_size * heads)


def _moba_sample(mq, mk, mv, mgate, k_pages, v_pages, page0, page_table, rel_bias, bias_pages, *, page_size):
    db, mw = mq.shape
    heads = mw // HEAD_DIM
    n_pages = page_table.shape[1]
    past_len = n_pages * page_size
    pages_per_block = MOBA_BLOCK // page_size
    n_blocks = past_len // MOBA_BLOCK
    assert past_len % MOBA_BLOCK == 0
    pt_flat = page_table.reshape(-1)
    q8 = mq.reshape(db, heads, HEAD_DIM)
    scores = _page_scores(pt_flat, q8, k_pages, n_pages=n_pages, page_size=page_size, page0=page0)
    sel = _topk_blocks(scores, n_blocks=n_blocks)[:, :, :TOP_K]
    seq_pages = (sel[..., None] * pages_per_block + jnp.arange(pages_per_block, dtype=jnp.int32)).reshape(-1)
    bown = jnp.broadcast_to(rel_bias.astype(F32)[0][:, None], (heads, HEAD_DIM))
    out = _page_attend(pt_flat, seq_pages, q8, mk.reshape(db, heads, HEAD_DIM), mv.reshape(db, heads, HEAD_DIM),
                       mgate.reshape(db, heads, HEAD_DIM), bown, k_pages, v_pages, bias_pages,
                       n_sel_pages=TOP_K * pages_per_block, n_pages=n_pages, page0=page0)
    return out.reshape(db, mw)


PROMPT_TM = 512
CONV_TS = 256


def kernel(x_prompt, x_sample, cache_k, cache_v, page_table, state_ret, state_conv, norm_g, w_in, conv_w,
           conv_b, conv_ln_g, conv_ln_b, w_out, rel_bias, final_g):
    batch, seq, d = x_prompt.shape
    db, dec_seq, _ = x_sample.shape
    assert dec_seq == 1, "the sample path handles one new token per sequence"
    depth = w_in.shape[0]
    ret_heads = d // 256
    moba_heads = d // 128
    conv_ch = d // 4
    n_pages = page_table.shape[1]
    page_size = cache_k.shape[2]
    past_len = n_pages * page_size
    n_phys = cache_k.shape[1]
    k_pages = cache_k.reshape(depth * n_phys, page_size * moba_heads, HEAD_DIM)
    v_pages = cache_v.reshape(depth * n_phys, page_size * moba_heads, HEAD_DIM)

    w_in_b = w_in.astype(BF16)
    w_out_b = w_out.astype(BF16)
    cos_p, sin_p = _rope_tables(jnp.arange(seq, dtype=jnp.int32), ret_heads)
    cos_s, sin_s = _rope_tables(past_len + jnp.arange(1, dtype=jnp.int32), ret_heads)
    ret_tables = _ret_tables(ret_heads)
    lg = jnp.log(1.0 - 2.0 ** (-5.0 - jnp.arange(ret_heads, dtype=F32)))
    decay_rows = jnp.broadcast_to(jnp.exp(lg)[:, None], (ret_heads, HEAD_DIM))
    bias_tiles = _moba_bias_tiles(rel_bias)
    bias_pages = _sample_bias_pages(rel_bias, past_len=past_len, page_size=page_size)
    conv0 = jnp.zeros((batch, CONV_WIDTH - 1, conv_ch), F32)
    ret0 = jnp.zeros((batch, ret_heads // PAIR, LANES, LANES), F32)
    nb = seq // MOBA_BLOCK
    nb_pad = -(-nb // 8) * 8

    hp = x_prompt.reshape(batch * seq, d)
    hs = x_sample.reshape(db, d)
    outs = {name: [] for name in ("kp", "vp", "ks", "vs", "rp", "rs", "cp", "cs")}
    for l in range(depth):
        last = l == depth - 1
        conv_in, ret_in, mq, mk, mv, mgate, kmean = _inproj(hp, norm_g[l], w_in_b[l], tm=PROMPT_TM,
                                                            with_kmean=True)
        conv_out, conv_state = _conv_prompt(conv_in, conv0, conv_w[l], conv_b[l], conv_ln_g[l], conv_ln_b[l],
                                            batch=batch, seq=seq, ts=CONV_TS)
        ret_out, ret_state = _ret_prompt(ret_in, ret0, cos_p, sin_p, ret_tables, batch=batch, seq=seq)
        kmean = jnp.pad(kmean.reshape(batch, nb, -1), ((0, 0), (0, nb_pad - nb), (0, 0)))
        moba_out = _moba_prompt(mq, mk, mv, kmean, mgate, bias_tiles, batch=batch, seq=seq)
        hp = _outproj(hp, conv_out, ret_out, moba_out, w_out_b[l], final_g, tm=PROMPT_TM, final_norm=last)
        outs["kp"].append(mk.reshape(batch, seq, moba_heads, HEAD_DIM))
        outs["vp"].append(mv.reshape(batch, seq, moba_heads, HEAD_DIM))
        outs["rp"].append(_unpair_states(ret_state))
        outs["cp"].append(conv_state)
        conv_in, ret_in, mq, mk, mv, mgate = _inproj(hs, norm_g[l], w_in_b[l], tm=db, with_kmean=False)
        conv_out, conv_state = _conv_sample(conv_in, state_conv[l], conv_w[l], conv_b[l], conv_ln_g[l],
                                            conv_ln_b[l])
        ret_out, ret_state = _ret_sample(ret_in, state_ret[l], cos_s, sin_s, decay_rows)
        moba_out = _moba_sample(mq, mk, mv, mgate, k_pages, v_pages, l * n_phys, page_table, rel_bias,
                                bias_pages, page_size=page_size)
        hs = _outproj(hs, conv_out, ret_out, moba_out, w_out_b[l], final_g, tm=db, final_norm=last)
        outs["ks"].append(mk.reshape(db, 1, moba_heads, HEAD_DIM))
        outs["vs"].append(mv.reshape(db, 1, moba_heads, HEAD_DIM))
        outs["rs"].append(ret_state)
        outs["cs"].append(conv_state)

    st = {name: jnp.stack(vals) for name, vals in outs.items()}
    return (hp.reshape(batch, seq, d), hs.reshape(db, 1, d), st["kp"], st["vp"], st["ks"], st["vs"],
            st["rp"], st["rs"], st["cp"], st["cs"])
```

```python
import functools
import math

import jax
import jax.numpy as jnp
from jax import lax
from jax.experimental import pallas as pl
from jax.experimental.pallas import tpu as pltpu

F32 = jnp.float32
BF16 = jnp.bfloat16

HEAD_DIM = 64
CONV_WIDTH = 31
MOBA_BLOCK = 256
TOP_K = 3
N_BUCKETS = 32
MAX_DISTANCE = 128
RET_CHUNK = 128
ROPE_BASE = 10000.0
EPS = 1e-6

LANES = 128
PAIR = LANES // HEAD_DIM
MASKED = -1e30
VMEM_LIMIT = 56 * 1024 * 1024
_CONTRACT_LAST = (((1,), (1,)), ((), ()))
_CONTRACT_FIRST = (((0,), (0,)), ((), ()))


def _silu(x):
    return x * jax.nn.sigmoid(x)


def _params(*sem):
    return pltpu.CompilerParams(dimension_semantics=sem, vmem_limit_bytes=VMEM_LIMIT)


def _inproj_kernel(x_ref, g_ref, w_ref, *out_refs, splits):
    x = x_ref[...]
    h = x * lax.rsqrt(jnp.mean(x * x, axis=-1, keepdims=True) + EPS) * g_ref[...]
    hb = h.astype(BF16)
    for (c0, c1), o_ref in zip(splits, out_refs):
        o_ref[...] = jnp.dot(hb, w_ref[:, c0:c1], preferred_element_type=F32)


def _inproj(x2d, g, w_bf16):
    m, d = x2d.shape
    conv_ch, ret_w, moba_w = d // 4, (d // 256) * HEAD_DIM, (d // 128) * HEAD_DIM
    widths = (3 * conv_ch, 4 * ret_w, moba_w, moba_w, moba_w, moba_w)
    edges = [0]
    for wd in widths:
        edges.append(edges[-1] + wd)
    splits = tuple(zip(edges[:-1], edges[1:]))
    assert edges[-1] == w_bf16.shape[1]
    return pl.pallas_call(
        functools.partial(_inproj_kernel, splits=splits),
        grid=(1,),
        in_specs=[pl.BlockSpec((m, d), lambda i: (0, 0)),
                  pl.BlockSpec((1, d), lambda i: (0, 0)),
                  pl.BlockSpec(w_bf16.shape, lambda i: (0, 0))],
        out_specs=[pl.BlockSpec((m, wd), lambda i: (0, 0)) for wd in widths],
        out_shape=[jax.ShapeDtypeStruct((m, wd), F32) for wd in widths],
        compiler_params=_params("arbitrary"),
        name="inproj_sample",
    )(x2d, g.reshape(1, d), w_bf16)


def _inproj_prompt_kernel(x_ref, g_ref, w_ref, wkv_ref, wk32_ref, *refs, splits, n_mean, aliased):
    if aliased:
        refs = refs[2:]
    conv_ref, ret_ref, q_ref, gate_ref, kt_ref, vt_ref, km_ref = refs
    x = x_ref[...]
    h = x * lax.rsqrt(jnp.mean(x * x, axis=-1, keepdims=True) + EPS) * g_ref[...]
    hb = h.astype(BF16)
    for (c0, c1), o_ref in zip(splits, (conv_ref, ret_ref, q_ref, gate_ref)):
        o_ref[...] = jnp.dot(hb, w_ref[:, c0:c1], preferred_element_type=F32)
    mw = kt_ref.shape[2]
    kt_ref[0, 0] = lax.dot_general(wkv_ref[0:mw, :], hb, _CONTRACT_LAST, preferred_element_type=F32)
    vt_ref[0, 0] = lax.dot_general(wkv_ref[mw:2 * mw, :], hb, _CONTRACT_LAST, preferred_element_type=F32)
    means = [jnp.mean(h[i * MOBA_BLOCK:(i + 1) * MOBA_BLOCK, :], axis=0, keepdims=True) for i in range(n_mean)]
    hm = jnp.concatenate(means + [jnp.zeros((8 - n_mean, h.shape[1]), F32)], axis=0)
    km = jnp.dot(hm, wk32_ref[...], precision=lax.Precision.HIGHEST, preferred_element_type=F32)
    km_ref[0] = km[0:n_mean, :]


def _inproj_prompt(x2d, g, w_rows, w_kv_t, w_k32, kt_buf, vt_buf, *, layer, depth, batch, seq, tm):
    m, d = x2d.shape
    conv_ch, ret_w, mw = d // 4, (d // 256) * HEAD_DIM, (d // 128) * HEAD_DIM
    widths = (3 * conv_ch, 4 * ret_w, mw, mw)
    edges = [0]
    for wd in widths:
        edges.append(edges[-1] + wd)
    splits = tuple(zip(edges[:-1], edges[1:]))
    assert edges[-1] == w_rows.shape[1] and seq % tm == 0 and tm % MOBA_BLOCK == 0
    n_mean = tm // MOBA_BLOCK
    assert n_mean <= 8
    per_seq = seq // tm
    aliased = kt_buf is not None
    row = lambda wd: pl.BlockSpec((tm, wd), lambda i: (i, 0))
    const = lambda a: pl.BlockSpec(a.shape, lambda i: (0,) * a.ndim)
    kv_spec = pl.BlockSpec((1, 1, mw, tm), lambda i: (layer, i // per_seq, 0, i % per_seq))
    kv_shape = jax.ShapeDtypeStruct((depth, batch, mw, seq), F32)
    in_specs = [row(d), pl.BlockSpec((1, d), lambda i: (0, 0)), const(w_rows), const(w_kv_t), const(w_k32)]
    args = [x2d, g.reshape(1, d), w_rows, w_kv_t, w_k32]
    aliases = {}
    if aliased:
        in_specs += [pl.BlockSpec(memory_space=pl.ANY)] * 2
        args += [kt_buf, vt_buf]
        aliases = {5: 4, 6: 5}
    return pl.pallas_call(
        functools.partial(_inproj_prompt_kernel, splits=splits, n_mean=n_mean, aliased=aliased),
        grid=(m // tm,),
        in_specs=in_specs,
        out_specs=[row(wd) for wd in widths] + [kv_spec, kv_spec,
                                                pl.BlockSpec((1, n_mean, mw), lambda i: (i, 0, 0))],
        out_shape=[jax.ShapeDtypeStruct((m, wd), F32) for wd in widths]
                  + [kv_shape, kv_shape, jax.ShapeDtypeStruct((m // tm, n_mean, mw), F32)],
        input_output_aliases=aliases,
        compiler_params=_params("parallel"),
        name="inproj_prompt",
    )(*args)


def _outproj_kernel(x_ref, c_ref, r_ref, m_ref, w_ref, fg_ref, y_ref, *, conv_ch, ret_w, final_norm):
    y = x_ref[...]
    y += jnp.dot(c_ref[...].astype(BF16), w_ref[0:conv_ch, :], preferred_element_type=F32)
    y += jnp.dot(r_ref[...].astype(BF16), w_ref[conv_ch:conv_ch + ret_w, :], preferred_element_type=F32)
    y += jnp.dot(m_ref[...].astype(BF16), w_ref[conv_ch + ret_w:, :], preferred_element_type=F32)
    if final_norm:
        y = y * lax.rsqrt(jnp.mean(y * y, axis=-1, keepdims=True) + EPS) * fg_ref[...]
    y_ref[...] = y


def _outproj(x2d, conv_out, ret_out, moba_out, w_bf16, final_g, *, tm, final_norm):
    m, d = x2d.shape
    conv_ch, ret_w, moba_w = conv_out.shape[1], ret_out.shape[1], moba_out.shape[1]
    row = lambda wd: pl.BlockSpec((tm, wd), lambda i: (i, 0))
    return pl.pallas_call(
        functools.partial(_outproj_kernel, conv_ch=conv_ch, ret_w=ret_w, final_norm=final_norm),
        grid=(m // tm,),
        in_specs=[row(d), row(conv_ch), row(ret_w), row(moba_w),
                  pl.BlockSpec(w_bf16.shape, lambda i: (0, 0)),
                  pl.BlockSpec((1, d), lambda i: (0, 0))],
        out_specs=row(d),
        out_shape=jax.ShapeDtypeStruct((m, d), F32),
        compiler_params=_params("parallel"),
        name="outproj",
    )(x2d, conv_out, ret_out, moba_out, w_bf16, final_g.reshape(1, d))


def _layernorm_silu(y, g, b):
    mu = jnp.mean(y, axis=-1, keepdims=True)
    yc = y - mu
    yn = yc * lax.rsqrt(jnp.mean(yc * yc, axis=-1, keepdims=True) + EPS)
    return _silu(yn * g + b)


CONV_PAD = 32
CONV_ROWS = 64


def _conv_kernel(in_ref, prev_ref, w_ref, b_ref, g_ref, beta_ref, o_ref, st_ref, buf, *, ts, ch):
    t = pl.program_id(1)
    keep = CONV_WIDTH - 1
    lo = CONV_PAD - keep

    @pl.when(t == 0)
    def _():
        buf[lo:CONV_PAD, :] = prev_ref[0]

    @pl.when(t > 0)
    def _():
        buf[lo:CONV_PAD, :] = buf[ts + lo:ts + CONV_PAD, :]

    buf[CONV_PAD:CONV_PAD + ts, :] = in_ref[:, 0:ch] * jax.nn.sigmoid(in_ref[:, ch:2 * ch])
    for r0 in range(0, ts, CONV_ROWS):
        acc = jnp.zeros((CONV_ROWS, ch), F32)
        for j in range(CONV_WIDTH):
            acc += buf[lo + r0 + j:lo + r0 + j + CONV_ROWS, :] * w_ref[j:j + 1, :]
        y = _layernorm_silu(acc + b_ref[...], g_ref[...], beta_ref[...])
        o_ref[r0:r0 + CONV_ROWS, :] = y * _silu(in_ref[r0:r0 + CONV_ROWS, 2 * ch:3 * ch])

    @pl.when(t == pl.num_programs(1) - 1)
    def _():
        st_ref[0] = buf[ts + lo:ts + CONV_PAD, :]


def _conv_prompt(conv_in, prev, w, b, g, beta, *, batch, seq, ts):
    ch = conv_in.shape[1] // 3
    nts = seq // ts
    assert seq % ts == 0 and ts % CONV_ROWS == 0 and ts >= CONV_PAD
    vec = lambda: pl.BlockSpec((1, ch), lambda bi, t: (0, 0))
    return pl.pallas_call(
        functools.partial(_conv_kernel, ts=ts, ch=ch),
        grid=(batch, nts),
        in_specs=[pl.BlockSpec((ts, 3 * ch), lambda bi, t: (bi * nts + t, 0)),
                  pl.BlockSpec((1, CONV_WIDTH - 1, ch), lambda bi, t: (bi, 0, 0)),
                  pl.BlockSpec((CONV_WIDTH, ch), lambda bi, t: (0, 0)),
                  vec(), vec(), vec()],
        out_specs=[pl.BlockSpec((ts, ch), lambda bi, t: (bi * nts + t, 0)),
                   pl.BlockSpec((1, CONV_WIDTH - 1, ch), lambda bi, t: (bi, 0, 0))],
        out_shape=[jax.ShapeDtypeStruct((batch * seq, ch), F32),
                   jax.ShapeDtypeStruct((batch, CONV_WIDTH - 1, ch), F32)],
        scratch_shapes=[pltpu.VMEM((ts + CONV_PAD, ch), F32)],
        compiler_params=_params("parallel", "arbitrary"),
        name="conv_prompt",
    )(conv_in, prev, w, b.reshape(1, ch), g.reshape(1, ch), beta.reshape(1, ch))


def _conv_step_kernel(in_ref, prev_ref, w_ref, b_ref, g_ref, beta_ref, o_ref, st_ref, *, ch):
    keep = CONV_WIDTH - 1
    row = in_ref[0]
    u = row[:, 0:ch] * jax.nn.sigmoid(row[:, ch:2 * ch])
    prev = prev_ref[0]
    acc = jnp.sum(prev * w_ref[0:keep, :], axis=0, keepdims=True) + u * w_ref[keep:keep + 1, :]
    y = _layernorm_silu(acc + b_ref[...], g_ref[...], beta_ref[...])
    o_ref[0] = y * _silu(row[:, 2 * ch:3 * ch])
    st_ref[0, 0:keep - 1, :] = prev_ref[0, 1:keep, :]
    st_ref[0, keep - 1:keep, :] = u


def _conv_sample(conv_in, prev, w, b, g, beta):
    db, ch3 = conv_in.shape
    ch = ch3 // 3
    keep = CONV_WIDTH - 1
    vec = lambda: pl.BlockSpec((1, ch), lambda bi: (0, 0))
    out, st = pl.pallas_call(
        functools.partial(_conv_step_kernel, ch=ch),
        grid=(db,),
        in_specs=[pl.BlockSpec((1, 1, ch3), lambda bi: (bi, 0, 0)),
                  pl.BlockSpec((1, keep, ch), lambda bi: (bi, 0, 0)),
                  pl.BlockSpec((CONV_WIDTH, ch), lambda bi: (0, 0)),
                  vec(), vec(), vec()],
        out_specs=[pl.BlockSpec((1, 1, ch), lambda bi: (bi, 0, 0)),
                   pl.BlockSpec((1, keep, ch), lambda bi: (bi, 0, 0))],
        out_shape=[jax.ShapeDtypeStruct((db, 1, ch), F32),
                   jax.ShapeDtypeStruct((db, keep, ch), F32)],
        compiler_params=_params("parallel"),
        name="conv_sample",
    )(conv_in.reshape(db, 1, ch3), prev, w, b.reshape(1, ch), g.reshape(1, ch), beta.reshape(1, ch))
    return out.reshape(db, ch), st


def _rope_rows(x, cos, sin_signed):
    w = x.shape[-1]
    half = HEAD_DIM // 2
    lane = lax.broadcasted_iota(jnp.int32, (1, w), 1)
    partner = jnp.where((lane % HEAD_DIM) < half, pltpu.roll(x, w - half, 1), pltpu.roll(x, half, 1))
    return x * cos + partner * sin_signed


def _half_sums(x, first_half):
    a = jnp.sum(jnp.where(first_half, x, 0.0), axis=-1, keepdims=True)
    b = jnp.sum(jnp.where(first_half, 0.0, x), axis=-1, keepdims=True)
    return jnp.where(first_half, a, b)


def _ret_kernel(in_ref, r0_ref, cos_ref, sin_ref, dmat_ref, cross_ref, wk_ref, decay_ref,
                o_ref, rout_ref, r_scr, *, rw):
    c = pl.program_id(1)
    npair = rw // LANES

    @pl.when(c == 0)
    def _():
        r_scr[...] = r0_ref[0]

    q = _rope_rows(in_ref[:, 0:rw], cos_ref[...], sin_ref[...])
    k = _rope_rows(in_ref[:, rw:2 * rw], cos_ref[...], sin_ref[...]) * (HEAD_DIM ** -0.5)
    lane = lax.broadcasted_iota(jnp.int32, (1, LANES), 1)
    first = lane < HEAD_DIM
    row = lax.broadcasted_iota(jnp.int32, (LANES, 1), 0)
    blockdiag = (row < HEAD_DIM) == first
    for p in range(npair):
        sl = slice(p * LANES, (p + 1) * LANES)
        qp, kp = q[:, sl], k[:, sl]
        vp = in_ref[:, 2 * rw + p * LANES:2 * rw + (p + 1) * LANES]
        gp = in_ref[:, 3 * rw + p * LANES:3 * rw + (p + 1) * LANES]
        kb, vb = kp.astype(BF16), vp.astype(BF16)
        o = jnp.dot(qp.astype(BF16), r_scr[p].astype(BF16), preferred_element_type=F32) * cross_ref[p]
        halves = []
        for hh in range(PAIR):
            own = first if hh == 0 else jnp.logical_not(first)
            qh = jnp.where(own, qp, 0.0).astype(BF16)
            s = lax.dot_general(qh, kb, _CONTRACT_LAST, preferred_element_type=F32) * dmat_ref[PAIR * p + hh]
            halves.append(jnp.dot(s.astype(BF16), vb, preferred_element_type=F32))
        o = o + jnp.where(first, halves[0], halves[1])
        kv = lax.dot_general((kp * wk_ref[p]).astype(BF16), vb, _CONTRACT_FIRST, preferred_element_type=F32)
        r_scr[p] = r_scr[p] * decay_ref[p] + jnp.where(blockdiag, kv, 0.0)
        ms = _half_sums(o * o, first) * (1.0 / HEAD_DIM)
        o_ref[:, sl] = o * lax.rsqrt(ms + EPS) * _silu(gp)

    @pl.when(c == pl.num_programs(1) - 1)
    def _():
        rout_ref[0] = r_scr[...]


def _ret_tables(heads):
    c = RET_CHUNK
    lg = jnp.log(1.0 - 2.0 ** (-5.0 - jnp.arange(heads, dtype=F32)))
    t = jnp.arange(c, dtype=F32)
    diff = t[:, None] - t[None, :]
    dmat = jnp.where(diff >= 0, jnp.exp(lg[:, None, None] * jnp.maximum(diff, 0.0)), 0.0)
    cross = jnp.exp(lg[None, :] * (t[:, None] + 1.0))
    wk = jnp.exp(lg[:, None] * (c - 1.0 - t[None, :]))
    chunk_decay = jnp.exp(lg * c)
    per_lane = lambda a: jnp.repeat(a, HEAD_DIM, axis=-1)
    to_pairs = lambda a: jnp.swapaxes(a.reshape(c, heads // PAIR, LANES), 0, 1)
    cross_p = to_pairs(per_lane(cross))
    wk_p = to_pairs(per_lane(wk.T))
    decay_p = to_pairs(per_lane(jnp.broadcast_to(chunk_decay[None, :], (c, heads))))
    return dmat, cross_p, wk_p, decay_p


def _blockdiag_pairs(state):
    b, h = state.shape[:2]
    s = state.reshape(b, h // PAIR, PAIR, HEAD_DIM, HEAD_DIM)
    z = jnp.zeros_like(s[:, :, 0])
    top = jnp.concatenate([s[:, :, 0], z], axis=-1)
    bot = jnp.concatenate([z, s[:, :, 1]], axis=-1)
    return jnp.concatenate([top, bot], axis=-2)


def _unpair_states(pairs):
    a = pairs[:, :, :HEAD_DIM, :HEAD_DIM]
    b = pairs[:, :, HEAD_DIM:, HEAD_DIM:]
    return jnp.stack([a, b], axis=2).reshape(pairs.shape[0], -1, HEAD_DIM, HEAD_DIM)


def _ret_prompt(ret_in, state0_pairs, cos_t, sin_t, tables, *, batch, seq):
    rw = ret_in.shape[1] // 4
    npair = rw // LANES
    c = RET_CHUNK
    nc = seq // c
    assert seq % c == 0
    dmat, cross_p, wk_p, decay_p = tables
    const = lambda a: pl.BlockSpec(a.shape, lambda bi, ci: (0,) * a.ndim)
    out, rout = pl.pallas_call(
        functools.partial(_ret_kernel, rw=rw),
        grid=(batch, nc),
        in_specs=[pl.BlockSpec((c, 4 * rw), lambda bi, ci: (bi * nc + ci, 0)),
                  pl.BlockSpec((1, npair, LANES, LANES), lambda bi, ci: (bi, 0, 0, 0)),
                  pl.BlockSpec((c, rw), lambda bi, ci: (ci, 0)),
                  pl.BlockSpec((c, rw), lambda bi, ci: (ci, 0)),
                  const(dmat), const(cross_p), const(wk_p), const(decay_p)],
        out_specs=[pl.BlockSpec((c, rw), lambda bi, ci: (bi * nc + ci, 0)),
                   pl.BlockSpec((1, npair, LANES, LANES), lambda bi, ci: (bi, 0, 0, 0))],
        out_shape=[jax.ShapeDtypeStruct((batch * seq, rw), F32),
                   jax.ShapeDtypeStruct((batch, npair, LANES, LANES), F32)],
        scratch_shapes=[pltpu.VMEM((npair, LANES, LANES), F32)],
        compiler_params=_params("parallel", "arbitrary"),
        name="ret_prompt",
    )(ret_in, state0_pairs, cos_t, sin_t, dmat, cross_p, wk_p, decay_p)
    return out, rout


def _ret_step_kernel(in_ref, r_ref, cos_ref, sin_ref, decay_ref, o_ref, rout_ref, *, rw):
    heads = rw // HEAD_DIM
    row = in_ref[0]
    q = _rope_rows(row[:, 0:rw], cos_ref[...], sin_ref[...])
    k = _rope_rows(row[:, rw:2 * rw], cos_ref[...], sin_ref[...]) * (HEAD_DIM ** -0.5)
    v = row[:, 2 * rw:3 * rw]
    gate = row[:, 3 * rw:4 * rw]
    eye = (lax.broadcasted_iota(jnp.int32, (HEAD_DIM, HEAD_DIM), 0)
           == lax.broadcasted_iota(jnp.int32, (HEAD_DIM, HEAD_DIM), 1))
    col = lambda r: jnp.sum(jnp.where(eye, r, 0.0), axis=-1, keepdims=True)
    outs = []
    for h in range(heads):
        sl = slice(h * HEAD_DIM, (h + 1) * HEAD_DIM)
        qh, kh, vh = q[:, sl], k[:, sl], v[:, sl]
        decay = decay_ref[h:h + 1, :]
        state = r_ref[0, h]
        o = jnp.sum(qh * kh, axis=-1, keepdims=True) * vh
        o = o + jnp.sum(col(qh) * state, axis=0, keepdims=True) * decay
        rout_ref[0, h] = state * decay + col(kh) * vh
        o = o * lax.rsqrt(jnp.mean(o * o, axis=-1, keepdims=True) + EPS)
        outs.append(o * _silu(gate[:, sl]))
    o_ref[0] = jnp.concatenate(outs, axis=-1)


def _ret_sample(ret_in, state, cos_row, sin_row, decay_rows):
    db, rw4 = ret_in.shape
    rw = rw4 // 4
    heads = rw // HEAD_DIM
    out, rout = pl.pallas_call(
        functools.partial(_ret_step_kernel, rw=rw),
        grid=(db,),
        in_specs=[pl.BlockSpec((1, 1, rw4), lambda bi: (bi, 0, 0)),
                  pl.BlockSpec((1, heads, HEAD_DIM, HEAD_DIM), lambda bi: (bi, 0, 0, 0)),
                  pl.BlockSpec((1, rw), lambda bi: (0, 0)),
                  pl.BlockSpec((1, rw), lambda bi: (0, 0)),
                  pl.BlockSpec((heads, HEAD_DIM), lambda bi: (0, 0))],
        out_specs=[pl.BlockSpec((1, 1, rw), lambda bi: (bi, 0, 0)),
                   pl.BlockSpec((1, heads, HEAD_DIM, HEAD_DIM), lambda bi: (bi, 0, 0, 0))],
        out_shape=[jax.ShapeDtypeStruct((db, 1, rw), F32),
                   jax.ShapeDtypeStruct(state.shape, F32)],
        compiler_params=_params("parallel"),
        name="ret_sample",
    )(ret_in.reshape(db, 1, rw4), state, cos_row, sin_row, decay_rows)
    return out.reshape(db, rw), rout


def _rope_tables(pos, heads):
    half = HEAD_DIM // 2
    freqs = ROPE_BASE ** (-jnp.arange(half, dtype=F32) / half)
    ang = pos.astype(F32)[:, None] * freqs[None, :]
    cos, sin = jnp.cos(ang), jnp.sin(ang)
    cos_t = jnp.tile(jnp.concatenate([cos, cos], axis=-1), (1, heads))
    sin_t = jnp.tile(jnp.concatenate([-sin, sin], axis=-1), (1, heads))
    return cos_t, sin_t


def _rel_bucket(d):
    n = jnp.maximum(d, 0)
    max_exact = N_BUCKETS // 2
    large = max_exact + (jnp.log(jnp.maximum(n, 1).astype(F32) / max_exact)
                         / math.log(MAX_DISTANCE / max_exact) * (N_BUCKETS - max_exact)).astype(jnp.int32)
    large = jnp.minimum(large, N_BUCKETS - 1)
    return jnp.where(n < max_exact, n, large)


def _block_rank_penalty(scores, past):
    nb = scores.shape[0]
    blk = lax.broadcasted_iota(jnp.int32, scores.shape, 0)
    sm = jnp.where(past, scores, -jnp.inf)
    rank = jnp.zeros(scores.shape, F32)
    for m in range(nb):
        other = sm[m:m + 1, :]
        beats = (other > sm) | ((other == sm) & (m < blk))
        rank += jnp.where(beats, 1.0, 0.0)
    return jnp.where(past & (rank < TOP_K), 0.0, MASKED)


def _moba_kernel(q_ref, kt_ref, vt_ref, km_ref, gate_ref, bias_ref, o_ref,
                 kext_scr, v_scr, m_scr, l_scr, acc_scr, *, nb):
    t = pl.program_id(2)
    tq = MOBA_BLOCK
    seq = kt_ref.shape[-1]
    lane = lax.broadcasted_iota(jnp.int32, (1, LANES), 1)

    @pl.when(t == 0)
    def _():
        row = lax.broadcasted_iota(jnp.int32, (LANES, 1), 0)
        for c0 in range(0, seq, MOBA_BLOCK):
            kt = kt_ref[0, 0, :, c0:c0 + MOBA_BLOCK]
            blk = c0 // MOBA_BLOCK
            for hh in range(PAIR):
                own_rows = (row < HEAD_DIM) if hh == 0 else (row >= HEAD_DIM)
                flag_row = blk + HEAD_DIM if hh == 0 else blk
                kext_scr[hh, :, c0:c0 + MOBA_BLOCK] = jnp.where(
                    own_rows, kt, jnp.where(row == flag_row, 1.0, 0.0)).astype(BF16)
            v_scr[:, c0:c0 + MOBA_BLOCK] = vt_ref[0, 0, :, c0:c0 + MOBA_BLOCK].astype(BF16)

    q2 = q_ref[...]
    km = km_ref[0]
    blk_id = lax.broadcasted_iota(jnp.int32, (nb, tq), 0)
    q_exts = []
    for hh in range(PAIR):
        own = (lane < HEAD_DIM) if hh == 0 else (lane >= HEAD_DIM)
        scores = lax.dot_general(jnp.where(own, km, 0.0), q2, _CONTRACT_LAST,
                                 precision=lax.Precision.HIGHEST, preferred_element_type=F32)
        pen = jnp.where(blk_id == t, 0.0, _block_rank_penalty(scores, blk_id < t))
        pad = jnp.zeros((HEAD_DIM, tq), F32)
        tail = jnp.zeros((HEAD_DIM - nb, tq), F32)
        pen_rows = [pad, pen, tail] if hh == 0 else [pen, tail, pad]
        pen_t = jnp.transpose(jnp.concatenate(pen_rows, axis=0))
        q_exts.append(jnp.where(own, q2 * (HEAD_DIM ** -0.5), pen_t).astype(BF16))

    m_scr[...] = jnp.full(m_scr.shape, -jnp.inf, F32)
    l_scr[...] = jnp.zeros(l_scr.shape, F32)
    acc_scr[...] = jnp.zeros(acc_scr.shape, F32)

    def block_start(blk):
        return pl.multiple_of(blk * MOBA_BLOCK, MOBA_BLOCK)

    def logits(blk, dist):
        out = []
        for hh in range(PAIR):
            kx = kext_scr[hh, :, pl.ds(block_start(blk), MOBA_BLOCK)]
            out.append(jnp.dot(q_exts[hh], kx, preferred_element_type=F32) + bias_ref[hh, dist])
        return tuple(out)

    def softmax_pv(blk, s_pair):
        vb = v_scr[:, pl.ds(block_start(blk), MOBA_BLOCK)]
        for hh in range(PAIR):
            s0, s1 = s_pair[hh][:, :LANES], s_pair[hh][:, LANES:]
            m_prev = m_scr[hh]
            m_new = jnp.maximum(m_prev, jnp.max(jnp.maximum(s0, s1), axis=-1, keepdims=True))
            alpha = jnp.exp(m_prev - m_new)
            p0, p1 = jnp.exp(s0 - m_new), jnp.exp(s1 - m_new)
            pv = lax.dot_general(jnp.concatenate([p0, p1], axis=1).astype(BF16), vb, _CONTRACT_LAST,
                                 preferred_element_type=F32)
            l_scr[hh] = alpha * l_scr[hh] + (p0 + p1)
            acc_scr[hh] = alpha * acc_scr[hh] + pv
            m_scr[hh] = m_new

    def body(i, s_cur):
        s_next = logits(t - i - 1, jnp.minimum(i + 1, 2))
        softmax_pv(t - i, s_cur)
        return s_next

    s_last = lax.fori_loop(0, t, body, logits(t, 0))
    softmax_pv(0, s_last)

    outs = [acc_scr[hh] / jnp.sum(l_scr[hh], axis=-1, keepdims=True) for hh in range(PAIR)]
    o_ref[...] = jnp.where(lane < HEAD_DIM, outs[0], outs[1]) * _silu(gate_ref[...])


def _toeplitz(vec, rows, cols):
    h, n = vec.shape
    assert n == rows + cols - 1
    flat = jnp.tile(vec, (1, rows + 1))[:, :rows * (n + 1)]
    return flat.reshape(h, rows, n + 1)[:, :, :cols]


def _moba_bias_tiles(rel_bias):
    blk = MOBA_BLOCK
    bias_t = rel_bias.astype(F32).T
    tiles = []
    for dist in range(3):
        d = dist * blk + jnp.arange(2 * blk - 1) - (blk - 1)
        g = bias_t[:, _rel_bucket(d)]
        if dist == 0:
            g = jnp.where((d >= 0)[None], g, -jnp.inf)
        tiles.append(_toeplitz(g, blk, blk)[:, :, ::-1])
    return jnp.stack(tiles, axis=1)


def _moba_prompt(mq, kt_buf, vt_buf, kmean, mgate, bias_tiles, *, layer, batch, seq):
    mw = mq.shape[1]
    npair = mw // LANES
    nq = seq // MOBA_BLOCK
    nb = kmean.shape[1]
    assert seq % MOBA_BLOCK == 0 and nb % 8 == 0 and nq <= nb <= HEAD_DIM
    assert MOBA_BLOCK + 1 >= MAX_DISTANCE
    tile = lambda: pl.BlockSpec((MOBA_BLOCK, LANES), lambda b, p, t: (b * nq + t, p))
    full = lambda: pl.BlockSpec((1, 1, LANES, seq), lambda b, p, t: (layer, b, p, 0))
    return pl.pallas_call(
        functools.partial(_moba_kernel, nb=nb),
        grid=(batch, npair, nq),
        in_specs=[tile(), full(), full(),
                  pl.BlockSpec((1, nb, LANES), lambda b, p, t: (b, 0, p)),
                  tile(),
                  pl.BlockSpec((PAIR, 3, MOBA_BLOCK, MOBA_BLOCK), lambda b, p, t: (p, 0, 0, 0))],
        out_specs=tile(),
        out_shape=jax.ShapeDtypeStruct((batch * seq, mw), F32),
        scratch_shapes=[pltpu.VMEM((PAIR, LANES, seq), BF16), pltpu.VMEM((LANES, seq), BF16)]
                       + [pltpu.VMEM((PAIR, MOBA_BLOCK, LANES), F32)] * 3,
        compiler_params=_params("parallel", "parallel", "arbitrary"),
        name="moba_prompt",
    )(mq, kt_buf, vt_buf, kmean, mgate, bias_tiles)


PAGES_PER_STEP = 16


def _page_score_kernel(pt_ref, q_ref, *refs, pages_per_block, heads):
    page_refs, s_ref = refs[:PAGES_PER_STEP], refs[PAGES_PER_STEP]
    g = pl.program_id(1)

    @pl.when(g == 0)
    def _():
        s_ref[...] = jnp.zeros(s_ref.shape, F32)

    lane = lax.broadcasted_iota(jnp.int32, (1, LANES), 1)
    head_row = lax.broadcasted_iota(jnp.int32, (heads, 1), 0)
    blocks_per_step = PAGES_PER_STEP // pages_per_block
    for bi in range(blocks_per_step):
        blk = g * blocks_per_step + bi
        pages = page_refs[bi * pages_per_block:(bi + 1) * pages_per_block]
        upd = jnp.zeros((heads, LANES), F32)
        for h in range(heads):
            rows = slice(h * HEAD_DIM, (h + 1) * HEAD_DIM)
            ksum = pages[0][0, rows, :]
            for page in pages[1:]:
                ksum = ksum + page[0, rows, :]
            part = jnp.sum(ksum * q_ref[0, rows, :], axis=0, keepdims=True)
            score = jnp.sum(part, axis=-1, keepdims=True) * (1.0 / MOBA_BLOCK)
            upd = jnp.where((head_row == h) & (lane == blk), score, upd)
        s_ref[0] += upd


def _page_scores(pt_flat, q_rep, cache_pages, *, n_pages, page_size, page0):
    db, width, _ = q_rep.shape
    heads = width // HEAD_DIM
    pages_per_block = MOBA_BLOCK // page_size
    assert MOBA_BLOCK % page_size == 0 and PAGES_PER_STEP % pages_per_block == 0
    assert n_pages % PAGES_PER_STEP == 0 and n_pages // pages_per_block <= LANES and page_size == LANES
    steps = n_pages // PAGES_PER_STEP

    def page_spec(i):
        return pl.BlockSpec((1, width, page_size),
                            lambda b, g, pt: (page0 + pt[b * n_pages + g * PAGES_PER_STEP + i], 0, 0))

    grid_spec = pltpu.PrefetchScalarGridSpec(
        num_scalar_prefetch=1,
        grid=(db, steps),
        in_specs=[pl.BlockSpec((1, width, LANES), lambda b, g, pt: (b, 0, 0))]
                 + [page_spec(i) for i in range(PAGES_PER_STEP)],
        out_specs=pl.BlockSpec((1, heads, LANES), lambda b, g, pt: (b, 0, 0)),
    )
    return pl.pallas_call(
        functools.partial(_page_score_kernel, pages_per_block=pages_per_block, heads=heads),
        grid_spec=grid_spec,
        out_shape=jax.ShapeDtypeStruct((db, heads, LANES), F32),
        compiler_params=_params("parallel", "arbitrary"),
        name="page_scores",
    )(pt_flat, q_rep, *([cache_pages] * PAGES_PER_STEP))


def _topk_kernel(s_ref, sel_ref, *, n_blocks):
    s = s_ref[0]
    lane = lax.broadcasted_iota(jnp.int32, s.shape, 1).astype(F32)
    s = jnp.where(lane < n_blocks, s, -jnp.inf)
    sel = jnp.zeros(s.shape, F32)
    for j in range(TOP_K):
        best = jnp.max(s, axis=-1, keepdims=True)
        idx = jnp.min(jnp.where(s == best, lane, float(LANES)), axis=-1, keepdims=True)
        sel = jnp.where(lane == j, idx, sel)
        s = jnp.where(lane == idx, -jnp.inf, s)
    sel_ref[0] = sel.astype(jnp.int32)


def _topk_blocks(scores, *, n_blocks):
    db, heads, _ = scores.shape
    assert n_blocks >= TOP_K
    spec = pl.BlockSpec((1, heads, LANES), lambda b: (b, 0, 0))
    return pl.pallas_call(
        functools.partial(_topk_kernel, n_blocks=n_blocks),
        grid=(db,),
        in_specs=[spec],
        out_specs=spec,
        out_shape=jax.ShapeDtypeStruct((db, heads, LANES), jnp.int32),
        compiler_params=_params("parallel"),
        name="topk_blocks",
    )(scores)


def _page_attend_kernel(pt_ref, seqp_ref, q_ref, kn_ref, vn_ref, gate_ref, bown_ref, bias_ref, *refs,
                        heads, n_sel_pages):
    k_refs, v_refs, o_ref = refs[:n_sel_pages], refs[n_sel_pages:2 * n_sel_pages], refs[2 * n_sel_pages]
    b = pl.program_id(0)
    h = pl.program_id(1)
    q = q_ref[0] * (HEAD_DIM ** -0.5)
    logits = []
    for j in range(n_sel_pages):
        seq_page = seqp_ref[(b * heads + h) * n_sel_pages + j]
        bias = bias_ref[h, pl.ds(seq_page, 1), :]
        logits.append(jnp.sum(k_refs[j][0] * q, axis=0, keepdims=True) + bias)
    s_own = jnp.sum(kn_ref[0] * q, axis=0, keepdims=True) + bown_ref[pl.ds(h, 1), :]
    top = logits[0]
    for s in logits[1:]:
        top = jnp.maximum(top, s)
    m = jnp.maximum(jnp.max(top, axis=-1, keepdims=True), s_own)
    p_own = jnp.exp(s_own - m)
    psum = jnp.zeros_like(m)
    acc = jnp.zeros((HEAD_DIM, LANES), F32)
    for j in range(n_sel_pages):
        p = jnp.exp(logits[j] - m)
        psum += p
        acc += v_refs[j][0] * p
    denom = jnp.sum(psum, axis=-1, keepdims=True) + p_own
    out_rep = (jnp.sum(acc, axis=-1, keepdims=True) + p_own * vn_ref[0]) / denom
    diag = (lax.broadcasted_iota(jnp.int32, (HEAD_DIM, LANES), 0)
            == lax.broadcasted_iota(jnp.int32, (HEAD_DIM, LANES), 1))
    out_row = jnp.sum(jnp.where(diag, out_rep, 0.0), axis=0, keepdims=True)[:, :HEAD_DIM]
    o_ref[0, pl.ds(h, 1), :] = out_row * _silu(gate_ref[0, pl.ds(h, 1), :])


def _page_attend(pt_flat, seq_pages, q_rep, kn_rep, vn_rep, gate8, bown, bias_rows, k_tiles, v_tiles,
                 *, n_sel_pages, n_pages, page0):
    db, heads, _ = gate8.shape
    page_size = k_tiles.shape[-1]
    head_rep = lambda: pl.BlockSpec((1, HEAD_DIM, LANES), lambda b, h, pt, sp: (b, h, 0))
    whole = lambda a: pl.BlockSpec(a.shape, lambda b, h, pt, sp: (0,) * a.ndim)

    def tile_spec(j):
        def index(b, h, pt, sp):
            page = page0 + pt[b * n_pages + sp[(b * heads + h) * n_sel_pages + j]]
            return (page * heads + h, 0, 0)
        return pl.BlockSpec((1, HEAD_DIM, page_size), index)

    grid_spec = pltpu.PrefetchScalarGridSpec(
        num_scalar_prefetch=2,
        grid=(db, heads),
        in_specs=[head_rep(), head_rep(), head_rep(),
                  pl.BlockSpec((1, heads, HEAD_DIM), lambda b, h, pt, sp: (b, 0, 0)),
                  whole(bown), whole(bias_rows)]
                 + [tile_spec(j) for j in range(n_sel_pages)] * 2,
        out_specs=pl.BlockSpec((1, heads, HEAD_DIM), lambda b, h, pt, sp: (b, 0, 0)),
    )
    return pl.pallas_call(
        functools.partial(_page_attend_kernel, heads=heads, n_sel_pages=n_sel_pages),
        grid_spec=grid_spec,
        out_shape=jax.ShapeDtypeStruct((db, heads, HEAD_DIM), F32),
        compiler_params=_params("parallel", "arbitrary"),
        name="page_attend",
    )(pt_flat, seq_pages, q_rep, kn_rep, vn_rep, gate8, bown, bias_rows,
      *([k_tiles] * n_sel_pages), *([v_tiles] * n_sel_pages))


def _sample_bias_rows(rel_bias, *, past_len, page_size):
    kpos = jnp.arange(past_len)
    onehot = (_rel_bucket(past_len - kpos)[:, None] == jnp.arange(N_BUCKETS)[None, :]).astype(F32)
    bias = jnp.dot(onehot, rel_bias.astype(F32), precision=lax.Precision.HIGHEST)
    return bias.T.reshape(rel_bias.shape[1], past_len // page_size, page_size)


def _lane_replicated(x):
    return jnp.broadcast_to(x[:, :, None], x.shape + (LANES,))


def _moba_sample(mq, mk, mv, mgate, k_cache_t, v_cache_t, page0, page_table, rel_bias, bias_rows, *, page_size):
    db, mw = mq.shape
    heads = mw // HEAD_DIM
    n_pages = page_table.shape[1]
    past_len = n_pages * page_size
    pages_per_block = MOBA_BLOCK // page_size
    n_blocks = past_len // MOBA_BLOCK
    assert past_len % MOBA_BLOCK == 0
    n_rows = k_cache_t.shape[0]
    pt_flat = page_table.reshape(-1)
    q_rep = _lane_replicated(mq)
    scores = _page_scores(pt_flat, q_rep, k_cache_t.reshape(n_rows, mw, page_size), n_pages=n_pages,
                          page_size=page_size, page0=page0)
    sel = _topk_blocks(scores, n_blocks=n_blocks)[:, :, :TOP_K]
    seq_pages = (sel[..., None] * pages_per_block + jnp.arange(pages_per_block, dtype=jnp.int32)).reshape(-1)
    bown = jnp.broadcast_to(rel_bias.astype(F32)[0][:, None], (heads, LANES))
    out = _page_attend(pt_flat, seq_pages, q_rep, _lane_replicated(mk), _lane_replicated(mv),
                       mgate.reshape(db, heads, HEAD_DIM), bown, bias_rows,
                       k_cache_t.reshape(n_rows * heads, HEAD_DIM, page_size),
                       v_cache_t.reshape(n_rows * heads, HEAD_DIM, page_size),
                       n_sel_pages=TOP_K * pages_per_block, n_pages=n_pages, page0=page0)
    return out.reshape(db, mw)


PROMPT_TM = 512
CONV_TS = 256


def kernel(x_prompt, x_sample, cache_k, cache_v, page_table, state_ret, state_conv, norm_g, w_in, conv_w,
           conv_b, conv_ln_g, conv_ln_b, w_out, rel_bias, final_g):
    batch, seq, d = x_prompt.shape
    db, dec_seq, _ = x_sample.shape
    assert dec_seq == 1, "the sample path handles one new token per sequence"
    depth = w_in.shape[0]
    ret_heads = d // 256
    moba_heads = d // 128
    conv_ch = d // 4
    n_pages = page_table.shape[1]
    page_size = cache_k.shape[2]
    past_len = n_pages * page_size
    n_phys = cache_k.shape[1]
    tokens_minor = lambda c: jnp.transpose(c, (0, 1, 3, 4, 2)).reshape(depth * n_phys, moba_heads, HEAD_DIM,
                                                                         page_size)
    k_cache_t, v_cache_t = tokens_minor(cache_k), tokens_minor(cache_v)

    mw = moba_heads * HEAD_DIM
    k0 = 3 * conv_ch + 4 * ret_heads * HEAD_DIM + mw
    w_rows = jnp.concatenate([w_in[:, :, :k0], w_in[:, :, k0 + 2 * mw:]], axis=-1).astype(BF16)
    w_kv_t = jnp.swapaxes(w_in[:, :, k0:k0 + 2 * mw], 1, 2).astype(BF16)
    w_k32 = w_in[:, :, k0:k0 + mw]
    w_in_b = w_in.astype(BF16)
    w_out_b = w_out.astype(BF16)
    cos_p, sin_p = _rope_tables(jnp.arange(seq, dtype=jnp.int32), ret_heads)
    cos_s, sin_s = _rope_tables(past_len + jnp.arange(1, dtype=jnp.int32), ret_heads)
    ret_tables = _ret_tables(ret_heads)
    lg = jnp.log(1.0 - 2.0 ** (-5.0 - jnp.arange(ret_heads, dtype=F32)))
    decay_rows = jnp.broadcast_to(jnp.exp(lg)[:, None], (ret_heads, HEAD_DIM))
    bias_tiles = _moba_bias_tiles(rel_bias)
    bias_rows = _sample_bias_rows(rel_bias, past_len=past_len, page_size=page_size)
    conv0 = jnp.zeros((batch, CONV_WIDTH - 1, conv_ch), F32)
    ret0 = jnp.zeros((batch, ret_heads // PAIR, LANES, LANES), F32)
    nb = seq // MOBA_BLOCK
    nb_pad = -(-nb // 8) * 8

    hp = x_prompt.reshape(batch * seq, d)
    hs = x_sample.reshape(db, d)
    outs = {name: [] for name in ("ks", "vs", "rp", "rs", "cp", "cs")}
    kt_buf = vt_buf = None
    for l in range(depth):
        last = l == depth - 1
        conv_in, ret_in, mq, mgate, kt_buf, vt_buf, kmean = _inproj_prompt(
            hp, norm_g[l], w_rows[l], w_kv_t[l], w_k32[l], kt_buf, vt_buf, layer=l, depth=depth, batch=batch,
            seq=seq, tm=PROMPT_TM)
        conv_out, conv_state = _conv_prompt(conv_in, conv0, conv_w[l], conv_b[l], conv_ln_g[l], conv_ln_b[l],
                                            batch=batch, seq=seq, ts=CONV_TS)
        ret_out, ret_state = _ret_prompt(ret_in, ret0, cos_p, sin_p, ret_tables, batch=batch, seq=seq)
        kmean = jnp.pad(kmean.reshape(batch, nb, -1), ((0, 0), (0, nb_pad - nb), (0, 0)))
        moba_out = _moba_prompt(mq, kt_buf, vt_buf, kmean, mgate, bias_tiles, layer=l, batch=batch, seq=seq)
        hp = _outproj(hp, conv_out, ret_out, moba_out, w_out_b[l], final_g, tm=PROMPT_TM, final_norm=last)
        outs["rp"].append(_unpair_states(ret_state))
        outs["cp"].append(conv_state)
        conv_in, ret_in, mq, mk, mv, mgate = _inproj(hs, norm_g[l], w_in_b[l])
        conv_out, conv_state = _conv_sample(conv_in, state_conv[l], conv_w[l], conv_b[l], conv_ln_g[l],
                                            conv_ln_b[l])
        ret_out, ret_state = _ret_sample(ret_in, state_ret[l], cos_s, sin_s, decay_rows)
        moba_out = _moba_sample(mq, mk, mv, mgate, k_cache_t, v_cache_t, l * n_phys, page_table, rel_bias,
                                bias_rows, page_size=page_size)
        hs = _outproj(hs, conv_out, ret_out, moba_out, w_out_b[l], final_g, tm=db, final_norm=last)
        outs["ks"].append(mk.reshape(db, 1, moba_heads, HEAD_DIM))
        outs["vs"].append(mv.reshape(db, 1, moba_heads, HEAD_DIM))
        outs["rs"].append(ret_state)
        outs["cs"].append(conv_state)

    st = {name: jnp.stack(vals) for name, vals in outs.items()}
    rows_major = lambda t: jnp.transpose(t.reshape(depth, batch, moba_heads, HEAD_DIM, seq), (0, 1, 4, 2, 3))
    return (hp.reshape(batch, seq, d), hs.reshape(db, 1, d), rows_major(kt_buf), rows_major(vt_buf),
            st["ks"], st["vs"], st["rp"], st["rs"], st["cp"], st["cs"])
```

```python
import functools
import math

import jax
import jax.numpy as jnp
from jax import lax
from jax.experimental import pallas as pl
from jax.experimental.pallas import tpu as pltpu

F32 = jnp.float32
BF16 = jnp.bfloat16

HEAD_DIM = 64
CONV_WIDTH = 31
MOBA_BLOCK = 256
TOP_K = 3
N_BUCKETS = 32
MAX_DISTANCE = 128
RET_CHUNK = 128
ROPE_BASE = 10000.0
EPS = 1e-6

LANES = 128
SUBLANES = 8
PAIR = LANES // HEAD_DIM
MASKED = -1e30
LOG2E = math.log2(math.e)
VMEM_LIMIT = 56 * 1024 * 1024
_CONTRACT_LAST = (((1,), (1,)), ((), ()))
_CONTRACT_FIRST = (((0,), (0,)), ((), ()))


def _silu(x):
    return x * jax.nn.sigmoid(x)


def _params(*sem):
    return pltpu.CompilerParams(dimension_semantics=sem, vmem_limit_bytes=VMEM_LIMIT)


def _inproj_kernel(x_ref, g_ref, w_ref, *out_refs, splits):
    x = x_ref[...]
    h = x * lax.rsqrt(jnp.mean(x * x, axis=-1, keepdims=True) + EPS) * g_ref[...]
    hb = h.astype(BF16)
    for (c0, c1), o_ref in zip(splits, out_refs):
        o_ref[...] = jnp.dot(hb, w_ref[:, c0:c1], preferred_element_type=F32)


def _inproj(x2d, g, w_bf16):
    m, d = x2d.shape
    conv_ch, ret_w, moba_w = d // 4, (d // 256) * HEAD_DIM, (d // 128) * HEAD_DIM
    widths = (3 * conv_ch, 4 * ret_w, moba_w, moba_w, moba_w, moba_w)
    edges = [0]
    for wd in widths:
        edges.append(edges[-1] + wd)
    splits = tuple(zip(edges[:-1], edges[1:]))
    assert edges[-1] == w_bf16.shape[1]
    return pl.pallas_call(
        functools.partial(_inproj_kernel, splits=splits),
        grid=(1,),
        in_specs=[pl.BlockSpec((m, d), lambda i: (0, 0)),
                  pl.BlockSpec((1, d), lambda i: (0, 0)),
                  pl.BlockSpec(w_bf16.shape, lambda i: (0, 0))],
        out_specs=[pl.BlockSpec((m, wd), lambda i: (0, 0)) for wd in widths],
        out_shape=[jax.ShapeDtypeStruct((m, wd), F32) for wd in widths],
        compiler_params=_params("arbitrary"),
        name="inproj_sample",
    )(x2d, g.reshape(1, d), w_bf16)


def _inproj_prompt_kernel(x_ref, g_ref, w_ref, wkv_ref, wk32_ref, *refs, splits, n_mean, aliased):
    if aliased:
        refs = refs[2:]
    conv_ref, ret_ref, q_ref, gate_ref, kt_ref, vt_ref, km_ref = refs
    x = x_ref[...]
    h = x * lax.rsqrt(jnp.mean(x * x, axis=-1, keepdims=True) + EPS) * g_ref[...]
    hb = h.astype(BF16)
    for (c0, c1), o_ref in zip(splits, (conv_ref, ret_ref, q_ref, gate_ref)):
        o_ref[...] = jnp.dot(hb, w_ref[:, c0:c1], preferred_element_type=F32)
    mw = kt_ref.shape[2]
    kt_ref[0, 0] = lax.dot_general(wkv_ref[0:mw, :], hb, _CONTRACT_LAST, preferred_element_type=F32)
    vt_ref[0, 0] = lax.dot_general(wkv_ref[mw:2 * mw, :], hb, _CONTRACT_LAST, preferred_element_type=F32)
    means = [jnp.mean(h[i * MOBA_BLOCK:(i + 1) * MOBA_BLOCK, :], axis=0, keepdims=True) for i in range(n_mean)]
    hm = jnp.concatenate(means + [jnp.zeros((8 - n_mean, h.shape[1]), F32)], axis=0)
    km = jnp.dot(hm, wk32_ref[...], precision=lax.Precision.HIGHEST, preferred_element_type=F32)
    km_ref[0] = km[0:n_mean, :]


def _inproj_prompt(x2d, g, w_rows, w_kv_t, w_k32, kt_buf, vt_buf, *, layer, depth, batch, seq, tm):
    m, d = x2d.shape
    conv_ch, ret_w, mw = d // 4, (d // 256) * HEAD_DIM, (d // 128) * HEAD_DIM
    widths = (3 * conv_ch, 4 * ret_w, mw, mw)
    edges = [0]
    for wd in widths:
        edges.append(edges[-1] + wd)
    splits = tuple(zip(edges[:-1], edges[1:]))
    assert edges[-1] == w_rows.shape[1] and seq % tm == 0 and tm % MOBA_BLOCK == 0
    n_mean = tm // MOBA_BLOCK
    assert n_mean <= 8
    per_seq = seq // tm
    aliased = kt_buf is not None
    row = lambda wd: pl.BlockSpec((tm, wd), lambda i: (i, 0))
    const = lambda a: pl.BlockSpec(a.shape, lambda i: (0,) * a.ndim)
    kv_spec = pl.BlockSpec((1, 1, mw, tm), lambda i: (layer, i // per_seq, 0, i % per_seq))
    kv_shape = jax.ShapeDtypeStruct((depth, batch, mw, seq), F32)
    in_specs = [row(d), pl.BlockSpec((1, d), lambda i: (0, 0)), const(w_rows), const(w_kv_t), const(w_k32)]
    args = [x2d, g.reshape(1, d), w_rows, w_kv_t, w_k32]
    aliases = {}
    if aliased:
        in_specs += [pl.BlockSpec(memory_space=pl.ANY)] * 2
        args += [kt_buf, vt_buf]
        aliases = {5: 4, 6: 5}
    return pl.pallas_call(
        functools.partial(_inproj_prompt_kernel, splits=splits, n_mean=n_mean, aliased=aliased),
        grid=(m // tm,),
        in_specs=in_specs,
        out_specs=[row(wd) for wd in widths] + [kv_spec, kv_spec,
                                                pl.BlockSpec((1, n_mean, mw), lambda i: (i, 0, 0))],
        out_shape=[jax.ShapeDtypeStruct((m, wd), F32) for wd in widths]
                  + [kv_shape, kv_shape, jax.ShapeDtypeStruct((m // tm, n_mean, mw), F32)],
        input_output_aliases=aliases,
        compiler_params=_params("parallel"),
        name="inproj_prompt",
    )(*args)


def _outproj_kernel(x_ref, c_ref, r_ref, m_ref, w_ref, fg_ref, y_ref, *, conv_ch, ret_w, final_norm):
    y = x_ref[...]
    y += jnp.dot(c_ref[...].astype(BF16), w_ref[0:conv_ch, :], preferred_element_type=F32)
    y += jnp.dot(r_ref[...].astype(BF16), w_ref[conv_ch:conv_ch + ret_w, :], preferred_element_type=F32)
    y += jnp.dot(m_ref[...].astype(BF16), w_ref[conv_ch + ret_w:, :], preferred_element_type=F32)
    if final_norm:
        y = y * lax.rsqrt(jnp.mean(y * y, axis=-1, keepdims=True) + EPS) * fg_ref[...]
    y_ref[...] = y


def _outproj(x2d, conv_out, ret_out, moba_out, w_bf16, final_g, *, tm, final_norm):
    m, d = x2d.shape
    conv_ch, ret_w, moba_w = conv_out.shape[1], ret_out.shape[1], moba_out.shape[1]
    row = lambda wd: pl.BlockSpec((tm, wd), lambda i: (i, 0))
    return pl.pallas_call(
        functools.partial(_outproj_kernel, conv_ch=conv_ch, ret_w=ret_w, final_norm=final_norm),
        grid=(m // tm,),
        in_specs=[row(d), row(conv_ch), row(ret_w), row(moba_w),
                  pl.BlockSpec(w_bf16.shape, lambda i: (0, 0)),
                  pl.BlockSpec((1, d), lambda i: (0, 0))],
        out_specs=row(d),
        out_shape=jax.ShapeDtypeStruct((m, d), F32),
        compiler_params=_params("parallel"),
        name="outproj",
    )(x2d, conv_out, ret_out, moba_out, w_bf16, final_g.reshape(1, d))


def _layernorm_silu(y, g, b):
    mu = jnp.mean(y, axis=-1, keepdims=True)
    yc = y - mu
    yn = yc * lax.rsqrt(jnp.mean(yc * yc, axis=-1, keepdims=True) + EPS)
    return _silu(yn * g + b)


CONV_PAD = 32
CONV_ROWS = 64


def _conv_kernel(in_ref, prev_ref, w_ref, b_ref, g_ref, beta_ref, o_ref, st_ref, buf, shifted, *, ts, ch):
    t = pl.program_id(1)
    keep = CONV_WIDTH - 1
    lo = CONV_PAD - keep

    @pl.when(t == 0)
    def _():
        buf[lo:CONV_PAD, :] = prev_ref[0]

    @pl.when(t > 0)
    def _():
        buf[lo:CONV_PAD, :] = buf[ts + lo:ts + CONV_PAD, :]

    buf[CONV_PAD:CONV_PAD + ts, :] = in_ref[:, 0:ch] * jax.nn.sigmoid(in_ref[:, ch:2 * ch])
    used = ts + CONV_PAD - SUBLANES
    for k in range(1, SUBLANES):
        shifted[k - 1, 0:used, :] = buf[k:k + used, :]
    for r0 in range(0, ts, CONV_ROWS):
        acc = jnp.zeros((CONV_ROWS, ch), F32)
        for j in range(CONV_WIDTH):
            k = (lo + j) % SUBLANES
            base = r0 + lo + j - k
            rows = buf[base:base + CONV_ROWS, :] if k == 0 else shifted[k - 1, base:base + CONV_ROWS, :]
            acc += rows * w_ref[j:j + 1, :]
        y = _layernorm_silu(acc + b_ref[...], g_ref[...], beta_ref[...])
        o_ref[r0:r0 + CONV_ROWS, :] = y * _silu(in_ref[r0:r0 + CONV_ROWS, 2 * ch:3 * ch])

    @pl.when(t == pl.num_programs(1) - 1)
    def _():
        st_ref[0] = buf[ts + lo:ts + CONV_PAD, :]


def _conv_prompt(conv_in, prev, w, b, g, beta, *, batch, seq, ts):
    ch = conv_in.shape[1] // 3
    nts = seq // ts
    assert seq % ts == 0 and ts % CONV_ROWS == 0 and ts >= CONV_PAD
    vec = lambda: pl.BlockSpec((1, ch), lambda bi, t: (0, 0))
    return pl.pallas_call(
        functools.partial(_conv_kernel, ts=ts, ch=ch),
        grid=(batch, nts),
        in_specs=[pl.BlockSpec((ts, 3 * ch), lambda bi, t: (bi * nts + t, 0)),
                  pl.BlockSpec((1, CONV_WIDTH - 1, ch), lambda bi, t: (bi, 0, 0)),
                  pl.BlockSpec((CONV_WIDTH, ch), lambda bi, t: (0, 0)),
                  vec(), vec(), vec()],
        out_specs=[pl.BlockSpec((ts, ch), lambda bi, t: (bi * nts + t, 0)),
                   pl.BlockSpec((1, CONV_WIDTH - 1, ch), lambda bi, t: (bi, 0, 0))],
        out_shape=[jax.ShapeDtypeStruct((batch * seq, ch), F32),
                   jax.ShapeDtypeStruct((batch, CONV_WIDTH - 1, ch), F32)],
        scratch_shapes=[pltpu.VMEM((ts + CONV_PAD, ch), F32),
                        pltpu.VMEM((SUBLANES - 1, ts + CONV_PAD, ch), F32)],
        compiler_params=_params("parallel", "arbitrary"),
        name="conv_prompt",
    )(conv_in, prev, w, b.reshape(1, ch), g.reshape(1, ch), beta.reshape(1, ch))


def _conv_step_kernel(in_ref, prev_ref, w_ref, b_ref, g_ref, beta_ref, o_ref, st_ref, *, ch):
    keep = CONV_WIDTH - 1
    row = in_ref[0]
    u = row[:, 0:ch] * jax.nn.sigmoid(row[:, ch:2 * ch])
    prev = prev_ref[0]
    acc = jnp.sum(prev * w_ref[0:keep, :], axis=0, keepdims=True) + u * w_ref[keep:keep + 1, :]
    y = _layernorm_silu(acc + b_ref[...], g_ref[...], beta_ref[...])
    o_ref[0] = y * _silu(row[:, 2 * ch:3 * ch])
    st_ref[0, 0:keep - 1, :] = prev_ref[0, 1:keep, :]
    st_ref[0, keep - 1:keep, :] = u


def _conv_sample(conv_in, prev, w, b, g, beta):
    db, ch3 = conv_in.shape
    ch = ch3 // 3
    keep = CONV_WIDTH - 1
    vec = lambda: pl.BlockSpec((1, ch), lambda bi: (0, 0))
    out, st = pl.pallas_call(
        functools.partial(_conv_step_kernel, ch=ch),
        grid=(db,),
        in_specs=[pl.BlockSpec((1, 1, ch3), lambda bi: (bi, 0, 0)),
                  pl.BlockSpec((1, keep, ch), lambda bi: (bi, 0, 0)),
                  pl.BlockSpec((CONV_WIDTH, ch), lambda bi: (0, 0)),
                  vec(), vec(), vec()],
        out_specs=[pl.BlockSpec((1, 1, ch), lambda bi: (bi, 0, 0)),
                   pl.BlockSpec((1, keep, ch), lambda bi: (bi, 0, 0))],
        out_shape=[jax.ShapeDtypeStruct((db, 1, ch), F32),
                   jax.ShapeDtypeStruct((db, keep, ch), F32)],
        compiler_params=_params("parallel"),
        name="conv_sample",
    )(conv_in.reshape(db, 1, ch3), prev, w, b.reshape(1, ch), g.reshape(1, ch), beta.reshape(1, ch))
    return out.reshape(db, ch), st


def _rope_rows(x, cos, sin_signed):
    w = x.shape[-1]
    half = HEAD_DIM // 2
    lane = lax.broadcasted_iota(jnp.int32, (1, w), 1)
    partner = jnp.where((lane % HEAD_DIM) < half, pltpu.roll(x, w - half, 1), pltpu.roll(x, half, 1))
    return x * cos + partner * sin_signed


def _half_sums(x, first_half):
    a = jnp.sum(jnp.where(first_half, x, 0.0), axis=-1, keepdims=True)
    b = jnp.sum(jnp.where(first_half, 0.0, x), axis=-1, keepdims=True)
    return jnp.where(first_half, a, b)


RET_SEQS = 2


def _ret_kernel(in_ref, r0_ref, cos_ref, sin_ref, dmat_ref, cross_ref, wk_ref, decay_ref,
                o_ref, rout_ref, r_scr, *, rw):
    in_refs = [in_ref.at[b] for b in range(RET_SEQS)]
    o_refs = [o_ref.at[b] for b in range(RET_SEQS)]
    c = pl.program_id(1)
    npair = rw // LANES

    @pl.when(c == 0)
    def _():
        r_scr[...] = r0_ref[...]

    lane = lax.broadcasted_iota(jnp.int32, (1, LANES), 1)
    first = lane < HEAD_DIM
    row = lax.broadcasted_iota(jnp.int32, (LANES, 1), 0)
    blockdiag = (row < HEAD_DIM) == first
    q = [_rope_rows(r[:, 0:rw], cos_ref[...], sin_ref[...]) for r in in_refs]
    k = [_rope_rows(r[:, rw:2 * rw], cos_ref[...], sin_ref[...]) * (HEAD_DIM ** -0.5) for r in in_refs]
    problems = [(b, p) for b in range(RET_SEQS) for p in range(npair)]
    lanes_of = lambda p: slice(p * LANES, (p + 1) * LANES)
    qb = {(b, p): q[b][:, lanes_of(p)] for b, p in problems}
    kp = {(b, p): k[b][:, lanes_of(p)] for b, p in problems}
    vb = {(b, p): in_refs[b][:, 2 * rw + p * LANES:2 * rw + (p + 1) * LANES].astype(BF16) for b, p in problems}
    kb = {pr: kp[pr].astype(BF16) for pr in problems}
    carry = {(b, p): jnp.dot(qb[b, p].astype(BF16), r_scr[b, p].astype(BF16), preferred_element_type=F32)
             for b, p in problems}
    scores = {}
    for b, p in problems:
        for hh in range(PAIR):
            own = first if hh == 0 else jnp.logical_not(first)
            qh = jnp.where(own, qb[b, p], 0.0).astype(BF16)
            scores[b, p, hh] = lax.dot_general(qh, kb[b, p], _CONTRACT_LAST, preferred_element_type=F32)
    kv = {(b, p): lax.dot_general((kp[b, p] * wk_ref[p]).astype(BF16), vb[b, p], _CONTRACT_FIRST,
                                  preferred_element_type=F32) for b, p in problems}
    inner = {(b, p, hh): jnp.dot((scores[b, p, hh] * dmat_ref[PAIR * p + hh]).astype(BF16), vb[b, p],
                                 preferred_element_type=F32)
             for b, p in problems for hh in range(PAIR)}
    for b, p in problems:
        o = carry[b, p] * cross_ref[p] + jnp.where(first, inner[b, p, 0], inner[b, p, 1])
        r_scr[b, p] = r_scr[b, p] * decay_ref[p] + jnp.where(blockdiag, kv[b, p], 0.0)
        ms = _half_sums(o * o, first) * (1.0 / HEAD_DIM)
        gate = in_refs[b][:, 3 * rw + p * LANES:3 * rw + (p + 1) * LANES]
        o_refs[b][:, lanes_of(p)] = o * lax.rsqrt(ms + EPS) * _silu(gate)

    @pl.when(c == pl.num_programs(1) - 1)
    def _():
        rout_ref[...] = r_scr[...]


def _ret_tables(heads):
    c = RET_CHUNK
    lg = jnp.log(1.0 - 2.0 ** (-5.0 - jnp.arange(heads, dtype=F32)))
    t = jnp.arange(c, dtype=F32)
    diff = t[:, None] - t[None, :]
    dmat = jnp.where(diff >= 0, jnp.exp(lg[:, None, None] * jnp.maximum(diff, 0.0)), 0.0)
    cross = jnp.exp(lg[None, :] * (t[:, None] + 1.0))
    wk = jnp.exp(lg[:, None] * (c - 1.0 - t[None, :]))
    chunk_decay = jnp.exp(lg * c)
    per_lane = lambda a: jnp.repeat(a, HEAD_DIM, axis=-1)
    to_pairs = lambda a: jnp.swapaxes(a.reshape(c, heads // PAIR, LANES), 0, 1)
    cross_p = to_pairs(per_lane(cross))
    wk_p = to_pairs(per_lane(wk.T))
    decay_p = to_pairs(per_lane(jnp.broadcast_to(chunk_decay[None, :], (c, heads))))
    return dmat, cross_p, wk_p, decay_p


def _unpair_states(pairs):
    a = pairs[:, :, :HEAD_DIM, :HEAD_DIM]
    b = pairs[:, :, HEAD_DIM:, HEAD_DIM:]
    return jnp.stack([a, b], axis=2).reshape(pairs.shape[0], -1, HEAD_DIM, HEAD_DIM)


def _ret_prompt(ret_in, state0_pairs, cos_t, sin_t, tables, *, batch, seq):
    rw = ret_in.shape[1] // 4
    npair = rw // LANES
    c = RET_CHUNK
    nc = seq // c
    assert seq % c == 0 and batch % RET_SEQS == 0
    dmat, cross_p, wk_p, decay_p = tables
    const = lambda a: pl.BlockSpec(a.shape, lambda bi, ci: (0,) * a.ndim)
    state_spec = pl.BlockSpec((RET_SEQS, npair, LANES, LANES), lambda bi, ci: (bi, 0, 0, 0))
    out, rout = pl.pallas_call(
        functools.partial(_ret_kernel, rw=rw),
        grid=(batch // RET_SEQS, nc),
        in_specs=[pl.BlockSpec((RET_SEQS, c, 4 * rw), lambda bi, ci: (bi, ci, 0)),
                  state_spec,
                  pl.BlockSpec((c, rw), lambda bi, ci: (ci, 0)),
                  pl.BlockSpec((c, rw), lambda bi, ci: (ci, 0)),
                  const(dmat), const(cross_p), const(wk_p), const(decay_p)],
        out_specs=[pl.BlockSpec((RET_SEQS, c, rw), lambda bi, ci: (bi, ci, 0)), state_spec],
        out_shape=[jax.ShapeDtypeStruct((batch, seq, rw), F32),
                   jax.ShapeDtypeStruct((batch, npair, LANES, LANES), F32)],
        scratch_shapes=[pltpu.VMEM((RET_SEQS, npair, LANES, LANES), F32)],
        compiler_params=_params("parallel", "arbitrary"),
        name="ret_prompt",
    )(ret_in.reshape(batch, seq, 4 * rw), state0_pairs, cos_t, sin_t, dmat, cross_p, wk_p, decay_p)
    return out.reshape(batch * seq, rw), rout


def _ret_step_kernel(in_ref, r_ref, cos_ref, sin_ref, decay_ref, o_ref, rout_ref, *, rw):
    heads = rw // HEAD_DIM
    row = in_ref[0]
    q = _rope_rows(row[:, 0:rw], cos_ref[...], sin_ref[...])
    k = _rope_rows(row[:, rw:2 * rw], cos_ref[...], sin_ref[...]) * (HEAD_DIM ** -0.5)
    v = row[:, 2 * rw:3 * rw]
    gate = row[:, 3 * rw:4 * rw]
    eye = (lax.broadcasted_iota(jnp.int32, (HEAD_DIM, HEAD_DIM), 0)
           == lax.broadcasted_iota(jnp.int32, (HEAD_DIM, HEAD_DIM), 1))
    col = lambda r: jnp.sum(jnp.where(eye, r, 0.0), axis=-1, keepdims=True)
    outs = []
    for h in range(heads):
        sl = slice(h * HEAD_DIM, (h + 1) * HEAD_DIM)
        qh, kh, vh = q[:, sl], k[:, sl], v[:, sl]
        decay = decay_ref[h:h + 1, :]
        state = r_ref[0, h]
        o = jnp.sum(qh * kh, axis=-1, keepdims=True) * vh
        o = o + jnp.sum(col(qh) * state, axis=0, keepdims=True) * decay
        rout_ref[0, h] = state * decay + col(kh) * vh
        o = o * lax.rsqrt(jnp.mean(o * o, axis=-1, keepdims=True) + EPS)
        outs.append(o * _silu(gate[:, sl]))
    o_ref[0] = jnp.concatenate(outs, axis=-1)


def _ret_sample(ret_in, state, cos_row, sin_row, decay_rows):
    db, rw4 = ret_in.shape
    rw = rw4 // 4
    heads = rw // HEAD_DIM
    out, rout = pl.pallas_call(
        functools.partial(_ret_step_kernel, rw=rw),
        grid=(db,),
        in_specs=[pl.BlockSpec((1, 1, rw4), lambda bi: (bi, 0, 0)),
                  pl.BlockSpec((1, heads, HEAD_DIM, HEAD_DIM), lambda bi: (bi, 0, 0, 0)),
                  pl.BlockSpec((1, rw), lambda bi: (0, 0)),
                  pl.BlockSpec((1, rw), lambda bi: (0, 0)),
                  pl.BlockSpec((heads, HEAD_DIM), lambda bi: (0, 0))],
        out_specs=[pl.BlockSpec((1, 1, rw), lambda bi: (bi, 0, 0)),
                   pl.BlockSpec((1, heads, HEAD_DIM, HEAD_DIM), lambda bi: (bi, 0, 0, 0))],
        out_shape=[jax.ShapeDtypeStruct((db, 1, rw), F32),
                   jax.ShapeDtypeStruct(state.shape, F32)],
        compiler_params=_params("parallel"),
        name="ret_sample",
    )(ret_in.reshape(db, 1, rw4), state, cos_row, sin_row, decay_rows)
    return out.reshape(db, rw), rout


def _rope_tables(pos, heads):
    half = HEAD_DIM // 2
    freqs = ROPE_BASE ** (-jnp.arange(half, dtype=F32) / half)
    ang = pos.astype(F32)[:, None] * freqs[None, :]
    cos, sin = jnp.cos(ang), jnp.sin(ang)
    cos_t = jnp.tile(jnp.concatenate([cos, cos], axis=-1), (1, heads))
    sin_t = jnp.tile(jnp.concatenate([-sin, sin], axis=-1), (1, heads))
    return cos_t, sin_t


def _rel_bucket(d):
    n = jnp.maximum(d, 0)
    max_exact = N_BUCKETS // 2
    large = max_exact + (jnp.log(jnp.maximum(n, 1).astype(F32) / max_exact)
                         / math.log(MAX_DISTANCE / max_exact) * (N_BUCKETS - max_exact)).astype(jnp.int32)
    large = jnp.minimum(large, N_BUCKETS - 1)
    return jnp.where(n < max_exact, n, large)


def _block_rank_penalty(scores, past):
    nb = scores.shape[0]
    blk = lax.broadcasted_iota(jnp.int32, scores.shape, 0)
    sm = jnp.where(past, scores, -jnp.inf)
    rank = jnp.zeros(scores.shape, F32)
    for m in range(nb):
        other = sm[m:m + 1, :]
        beats = (other > sm) | ((other == sm) & (m < blk))
        rank += jnp.where(beats, 1.0, 0.0)
    return jnp.where(past & (rank < TOP_K), 0.0, MASKED)


def _moba_kernel(q_ref, kt_ref, vt_ref, km_ref, gate_ref, bias_ref, o_ref,
                 kext_scr, v_scr, qext_scr, m_scr, l_scr, acc_scr, *, nb):
    tq = MOBA_BLOCK
    seq = kt_ref.shape[-1]
    nq = seq // tq
    lane = lax.broadcasted_iota(jnp.int32, (1, LANES), 1)
    row = lax.broadcasted_iota(jnp.int32, (LANES, 1), 0)

    for c0 in range(0, seq, tq):
        kt = kt_ref[0, 0, :, c0:c0 + tq]
        for hh in range(PAIR):
            own_rows = (row < HEAD_DIM) if hh == 0 else (row >= HEAD_DIM)
            flag_row = c0 // tq + (HEAD_DIM if hh == 0 else 0)
            kext_scr[hh, :, c0:c0 + tq] = jnp.where(own_rows, kt, jnp.where(row == flag_row, 1.0, 0.0)).astype(BF16)
        v_scr[:, c0:c0 + tq] = vt_ref[0, 0, :, c0:c0 + tq].astype(BF16)

    km = km_ref[0]
    blk_id = lax.broadcasted_iota(jnp.int32, (nb, seq), 0)
    q_blk = lax.broadcasted_iota(jnp.int32, (nb, seq), 1) // tq
    for hh in range(PAIR):
        own = (lane < HEAD_DIM) if hh == 0 else (lane >= HEAD_DIM)
        scores = lax.dot_general(jnp.where(own, km, 0.0), q_ref[...], _CONTRACT_LAST,
                                 precision=lax.Precision.HIGHEST, preferred_element_type=F32)
        pen = jnp.where(blk_id == q_blk, 0.0, _block_rank_penalty(scores, blk_id < q_blk))
        pad = jnp.zeros((HEAD_DIM, tq), F32)
        tail = jnp.zeros((HEAD_DIM - nb, tq), F32)
        for c0 in range(0, seq, tq):
            pen_c = pen[:, c0:c0 + tq]
            pen_rows = [pad, pen_c, tail] if hh == 0 else [pen_c, tail, pad]
            pen_t = jnp.transpose(jnp.concatenate(pen_rows, axis=0))
            q_c = q_ref[c0:c0 + tq, :] * (HEAD_DIM ** -0.5 * LOG2E)
            qext_scr[hh, c0:c0 + tq, :] = jnp.where(own, q_c, pen_t).astype(BF16)

    def start_of(blk):
        return pl.multiple_of(blk * tq, tq)

    def first_block(t, j):
        return jnp.maximum(t - 2 * j - 1, 0)

    def logits(t, j):
        c = first_block(t, j)
        tile_lo = jnp.minimum(t - c, 2)
        tile_hi = jnp.where(t - 2 * j - 1 >= 0, jnp.minimum(t - c - 1, 2), 3)
        out = []
        for hh in range(PAIR):
            qx = qext_scr[hh, pl.ds(start_of(t), tq), :]
            kx = kext_scr[hh, :, pl.ds(start_of(c), 2 * tq)]
            s = jnp.dot(qx, kx, preferred_element_type=F32)
            out.append((s[:, :tq] + bias_ref[hh, tile_lo], s[:, tq:] + bias_ref[hh, tile_hi]))
        return tuple(out)

    def softmax_pv(t, j, s_pair):
        vb = v_scr[:, pl.ds(start_of(first_block(t, j)), 2 * tq)]
        for hh in range(PAIR):
            parts = [half[:, k * LANES:(k + 1) * LANES] for half in s_pair[hh] for k in range(tq // LANES)]
            top = functools.reduce(jnp.maximum, parts)
            m_prev = m_scr[hh]
            m_new = jnp.maximum(m_prev, jnp.max(top, axis=-1, keepdims=True))
            alpha = jnp.exp2(m_prev - m_new)
            probs = [jnp.exp2(part - m_new) for part in parts]
            pv = lax.dot_general(jnp.concatenate(probs, axis=1).astype(BF16), vb, _CONTRACT_LAST,
                                 preferred_element_type=F32)
            l_scr[hh] = alpha * l_scr[hh] + functools.reduce(jnp.add, probs)
            acc_scr[hh] = alpha * acc_scr[hh] + pv
            m_scr[hh] = m_new

    def tile_body(t, s_first):
        m_scr[...] = jnp.full(m_scr.shape, -jnp.inf, F32)
        l_scr[...] = jnp.zeros(l_scr.shape, F32)
        acc_scr[...] = jnp.zeros(acc_scr.shape, F32)
        last = t // 2

        def step_body(j, s_cur):
            s_next = logits(t, j + 1)
            softmax_pv(t, j, s_cur)
            return s_next

        s_last = lax.fori_loop(0, last, step_body, s_first)
        s_next_first = logits(jnp.minimum(t + 1, nq - 1), 0)
        softmax_pv(t, last, s_last)
        outs = [acc_scr[hh] / jnp.sum(l_scr[hh], axis=-1, keepdims=True) for hh in range(PAIR)]
        rows = pl.ds(start_of(t), tq)
        o_ref[rows, :] = jnp.where(lane < HEAD_DIM, outs[0], outs[1]) * _silu(gate_ref[rows, :])
        return s_next_first

    lax.fori_loop(0, nq, tile_body, logits(0, 0))


def _toeplitz(vec, rows, cols):
    h, n = vec.shape
    assert n == rows + cols - 1
    flat = jnp.tile(vec, (1, rows + 1))[:, :rows * (n + 1)]
    return flat.reshape(h, rows, n + 1)[:, :, :cols]


def _moba_bias_tiles(rel_bias):
    blk = MOBA_BLOCK
    bias_t = rel_bias.astype(F32).T
    tiles = []
    for dist in range(3):
        d = dist * blk + jnp.arange(2 * blk - 1) - (blk - 1)
        g = bias_t[:, _rel_bucket(d)]
        if dist == 0:
            g = jnp.where((d >= 0)[None], g, -jnp.inf)
        tiles.append(_toeplitz(g, blk, blk)[:, :, ::-1])
    tiles.append(jnp.full_like(tiles[0], -jnp.inf))
    return jnp.stack(tiles, axis=1) * LOG2E


def _moba_prompt(mq, kt_buf, vt_buf, kmean, mgate, bias_tiles, *, layer, batch, seq):
    mw = mq.shape[1]
    npair = mw // LANES
    nq = seq // MOBA_BLOCK
    nb = kmean.shape[1]
    assert seq % MOBA_BLOCK == 0 and nb % 8 == 0 and 2 <= nq <= nb <= HEAD_DIM
    assert MOBA_BLOCK + 1 >= MAX_DISTANCE
    rows = lambda: pl.BlockSpec((seq, LANES), lambda b, p: (b, p))
    cols = lambda: pl.BlockSpec((1, 1, LANES, seq), lambda b, p: (layer, b, p, 0))
    return pl.pallas_call(
        functools.partial(_moba_kernel, nb=nb),
        grid=(batch, npair),
        in_specs=[rows(), cols(), cols(),
                  pl.BlockSpec((1, nb, LANES), lambda b, p: (b, 0, p)),
                  rows(),
                  pl.BlockSpec((PAIR, 4, MOBA_BLOCK, MOBA_BLOCK), lambda b, p: (p, 0, 0, 0))],
        out_specs=rows(),
        out_shape=jax.ShapeDtypeStruct((batch * seq, mw), F32),
        scratch_shapes=[pltpu.VMEM((PAIR, LANES, seq), BF16), pltpu.VMEM((LANES, seq), BF16),
                        pltpu.VMEM((PAIR, seq, LANES), BF16)]
                       + [pltpu.VMEM((PAIR, MOBA_BLOCK, LANES), F32)] * 3,
        compiler_params=_params("parallel", "parallel"),
        name="moba_prompt",
    )(mq, kt_buf, vt_buf, kmean, mgate, bias_tiles)


PAGES_PER_STEP = 16


def _page_score_kernel(pt_ref, q_ref, *refs, pages_per_block, heads):
    page_refs, s_ref = refs[:PAGES_PER_STEP], refs[PAGES_PER_STEP]
    g = pl.program_id(1)

    @pl.when(g == 0)
    def _():
        s_ref[...] = jnp.zeros(s_ref.shape, F32)

    lane = lax.broadcasted_iota(jnp.int32, (1, LANES), 1)
    head_row = lax.broadcasted_iota(jnp.int32, (heads, 1), 0)
    blocks_per_step = PAGES_PER_STEP // pages_per_block
    for bi in range(blocks_per_step):
        blk = g * blocks_per_step + bi
        pages = page_refs[bi * pages_per_block:(bi + 1) * pages_per_block]
        upd = jnp.zeros((heads, LANES), F32)
        for h in range(heads):
            rows = slice(h * HEAD_DIM, (h + 1) * HEAD_DIM)
            ksum = pages[0][0, rows, :]
            for page in pages[1:]:
                ksum = ksum + page[0, rows, :]
            part = jnp.sum(ksum * q_ref[0, rows, :], axis=0, keepdims=True)
            score = jnp.sum(part, axis=-1, keepdims=True) * (1.0 / MOBA_BLOCK)
            upd = jnp.where((head_row == h) & (lane == blk), score, upd)
        s_ref[0] += upd


def _page_scores(pt_flat, q_rep, cache_pages, *, n_pages, page_size, page0):
    db, width, _ = q_rep.shape
    heads = width // HEAD_DIM
    pages_per_block = MOBA_BLOCK // page_size
    assert MOBA_BLOCK % page_size == 0 and PAGES_PER_STEP % pages_per_block == 0
    assert n_pages % PAGES_PER_STEP == 0 and n_pages // pages_per_block <= LANES and page_size == LANES
    steps = n_pages // PAGES_PER_STEP

    def page_spec(i):
        return pl.BlockSpec((1, width, page_size),
                            lambda b, g, pt: (page0 + pt[b * n_pages + g * PAGES_PER_STEP + i], 0, 0))

    grid_spec = pltpu.PrefetchScalarGridSpec(
        num_scalar_prefetch=1,
        grid=(db, steps),
        in_specs=[pl.BlockSpec((1, width, LANES), lambda b, g, pt: (b, 0, 0))]
                 + [page_spec(i) for i in range(PAGES_PER_STEP)],
        out_specs=pl.BlockSpec((1, heads, LANES), lambda b, g, pt: (b, 0, 0)),
    )
    return pl.pallas_call(
        functools.partial(_page_score_kernel, pages_per_block=pages_per_block, heads=heads),
        grid_spec=grid_spec,
        out_shape=jax.ShapeDtypeStruct((db, heads, LANES), F32),
        compiler_params=_params("parallel", "arbitrary"),
        name="page_scores",
    )(pt_flat, q_rep, *([cache_pages] * PAGES_PER_STEP))


def _topk_kernel(s_ref, sel_ref, *, n_blocks):
    s = s_ref[0]
    lane = lax.broadcasted_iota(jnp.int32, s.shape, 1).astype(F32)
    s = jnp.where(lane < n_blocks, s, -jnp.inf)
    sel = jnp.zeros(s.shape, F32)
    for j in range(TOP_K):
        best = jnp.max(s, axis=-1, keepdims=True)
        idx = jnp.min(jnp.where(s == best, lane, float(LANES)), axis=-1, keepdims=True)
        sel = jnp.where(lane == j, idx, sel)
        s = jnp.where(lane == idx, -jnp.inf, s)
    sel_ref[0] = sel.astype(jnp.int32)


def _topk_blocks(scores, *, n_blocks):
    db, heads, _ = scores.shape
    assert n_blocks >= TOP_K
    spec = pl.BlockSpec((1, heads, LANES), lambda b: (b, 0, 0))
    return pl.pallas_call(
        functools.partial(_topk_kernel, n_blocks=n_blocks),
        grid=(db,),
        in_specs=[spec],
        out_specs=spec,
        out_shape=jax.ShapeDtypeStruct((db, heads, LANES), jnp.int32),
        compiler_params=_params("parallel"),
        name="topk_blocks",
    )(scores)


def _page_attend_kernel(pt_ref, seqp_ref, q_ref, kn_ref, vn_ref, gate_ref, bown_ref, bias_ref, *refs,
                        heads, n_sel_pages):
    k_refs, v_refs, o_ref = refs[:n_sel_pages], refs[n_sel_pages:2 * n_sel_pages], refs[2 * n_sel_pages]
    b = pl.program_id(0)
    h = pl.program_id(1)
    q = q_ref[0] * (HEAD_DIM ** -0.5)
    logits = []
    for j in range(n_sel_pages):
        seq_page = seqp_ref[(b * heads + h) * n_sel_pages + j]
        bias = bias_ref[h, pl.ds(seq_page, 1), :]
        logits.append(jnp.sum(k_refs[j][0] * q, axis=0, keepdims=True) + bias)
    s_own = jnp.sum(kn_ref[0] * q, axis=0, keepdims=True) + bown_ref[pl.ds(h, 1), :]
    top = logits[0]
    for s in logits[1:]:
        top = jnp.maximum(top, s)
    m = jnp.maximum(jnp.max(top, axis=-1, keepdims=True), s_own)
    p_own = jnp.exp(s_own - m)
    psum = jnp.zeros_like(m)
    acc = jnp.zeros((HEAD_DIM, LANES), F32)
    for j in range(n_sel_pages):
        p = jnp.exp(logits[j] - m)
        psum += p
        acc += v_refs[j][0] * p
    denom = jnp.sum(psum, axis=-1, keepdims=True) + p_own
    out_rep = (jnp.sum(acc, axis=-1, keepdims=True) + p_own * vn_ref[0]) / denom
    diag = (lax.broadcasted_iota(jnp.int32, (HEAD_DIM, LANES), 0)
            == lax.broadcasted_iota(jnp.int32, (HEAD_DIM, LANES), 1))
    out_row = jnp.sum(jnp.where(diag, out_rep, 0.0), axis=0, keepdims=True)[:, :HEAD_DIM]
    o_ref[0, pl.ds(h, 1), :] = out_row * _silu(gate_ref[0, pl.ds(h, 1), :])


def _page_attend(pt_flat, seq_pages, q_rep, kn_rep, vn_rep, gate8, bown, bias_rows, k_tiles, v_tiles,
                 *, n_sel_pages, n_pages, page0):
    db, heads, _ = gate8.shape
    page_size = k_tiles.shape[-1]
    head_rep = lambda: pl.BlockSpec((1, HEAD_DIM, LANES), lambda b, h, pt, sp: (b, h, 0))
    whole = lambda a: pl.BlockSpec(a.shape, lambda b, h, pt, sp: (0,) * a.ndim)

    def tile_spec(j):
        def index(b, h, pt, sp):
            page = page0 + pt[b * n_pages + sp[(b * heads + h) * n_sel_pages + j]]
            return (page * heads + h, 0, 0)
        return pl.BlockSpec((1, HEAD_DIM, page_size), index)

    grid_spec = pltpu.PrefetchScalarGridSpec(
        num_scalar_prefetch=2,
        grid=(db, heads),
        in_specs=[head_rep(), head_rep(), head_rep(),
                  pl.BlockSpec((1, heads, HEAD_DIM), lambda b, h, pt, sp: (b, 0, 0)),
                  whole(bown), whole(bias_rows)]
                 + [tile_spec(j) for j in range(n_sel_pages)] * 2,
        out_specs=pl.BlockSpec((1, heads, HEAD_DIM), lambda b, h, pt, sp: (b, 0, 0)),
    )
    return pl.pallas_call(
        functools.partial(_page_attend_kernel, heads=heads, n_sel_pages=n_sel_pages),
        grid_spec=grid_spec,
        out_shape=jax.ShapeDtypeStruct((db, heads, HEAD_DIM), F32),
        compiler_params=_params("parallel", "arbitrary"),
        name="page_attend",
    )(pt_flat, seq_pages, q_rep, kn_rep, vn_rep, gate8, bown, bias_rows,
      *([k_tiles] * n_sel_pages), *([v_tiles] * n_sel_pages))


def _sample_bias_rows(rel_bias, *, past_len, page_size):
    kpos = jnp.arange(past_len)
    onehot = (_rel_bucket(past_len - kpos)[:, None] == jnp.arange(N_BUCKETS)[None, :]).astype(F32)
    bias = jnp.dot(onehot, rel_bias.astype(F32), precision=lax.Precision.HIGHEST)
    return bias.T.reshape(rel_bias.shape[1], past_len // page_size, page_size)


def _lane_replicated(x):
    return jnp.broadcast_to(x[:, :, None], x.shape + (LANES,))


def _moba_sample(mq, mk, mv, mgate, k_cache_t, v_cache_t, page0, page_table, rel_bias, bias_rows, *, page_size):
    db, mw = mq.shape
    heads = mw // HEAD_DIM
    n_pages = page_table.shape[1]
    past_len = n_pages * page_size
    pages_per_block = MOBA_BLOCK // page_size
    n_blocks = past_len // MOBA_BLOCK
    assert past_len % MOBA_BLOCK == 0
    n_rows = k_cache_t.shape[0]
    pt_flat = page_table.reshape(-1)
    q_rep = _lane_replicated(mq)
    scores = _page_scores(pt_flat, q_rep, k_cache_t.reshape(n_rows, mw, page_size), n_pages=n_pages,
                          page_size=page_size, page0=page0)
    sel = _topk_blocks(scores, n_blocks=n_blocks)[:, :, :TOP_K]
    seq_pages = (sel[..., None] * pages_per_block + jnp.arange(pages_per_block, dtype=jnp.int32)).reshape(-1)
    bown = jnp.broadcast_to(rel_bias.astype(F32)[0][:, None], (heads, LANES))
    out = _page_attend(pt_flat, seq_pages, q_rep, _lane_replicated(mk), _lane_replicated(mv),
                       mgate.reshape(db, heads, HEAD_DIM), bown, bias_rows,
                       k_cache_t.reshape(n_rows * heads, HEAD_DIM, page_size),
                       v_cache_t.reshape(n_rows * heads, HEAD_DIM, page_size),
                       n_sel_pages=TOP_K * pages_per_block, n_pages=n_pages, page0=page0)
    return out.reshape(db, mw)


PROMPT_TM = 512
CONV_TS = 512


def kernel(x_prompt, x_sample, cache_k, cache_v, page_table, state_ret, state_conv, norm_g, w_in, conv_w,
           conv_b, conv_ln_g, conv_ln_b, w_out, rel_bias, final_g):
    batch, seq, d = x_prompt.shape
    db, dec_seq, _ = x_sample.shape
    assert dec_seq == 1, "the sample path handles one new token per sequence"
    depth = w_in.shape[0]
    ret_heads = d // 256
    moba_heads = d // 128
    conv_ch = d // 4
    n_pages = page_table.shape[1]
    page_size = cache_k.shape[2]
    past_len = n_pages * page_size
    n_phys = cache_k.shape[1]
    tokens_minor = lambda c: jnp.transpose(c, (0, 1, 3, 4, 2)).reshape(depth * n_phys, moba_heads, HEAD_DIM,
                                                                         page_size)
    k_cache_t, v_cache_t = tokens_minor(cache_k), tokens_minor(cache_v)

    mw = moba_heads * HEAD_DIM
    k0 = 3 * conv_ch + 4 * ret_heads * HEAD_DIM + mw
    w_rows = jnp.concatenate([w_in[:, :, :k0], w_in[:, :, k0 + 2 * mw:]], axis=-1).astype(BF16)
    w_kv_t = jnp.swapaxes(w_in[:, :, k0:k0 + 2 * mw], 1, 2).astype(BF16)
    w_k32 = w_in[:, :, k0:k0 + mw]
    w_in_b = w_in.astype(BF16)
    w_out_b = w_out.astype(BF16)
    cos_p, sin_p = _rope_tables(jnp.arange(seq, dtype=jnp.int32), ret_heads)
    cos_s, sin_s = _rope_tables(past_len + jnp.arange(1, dtype=jnp.int32), ret_heads)
    ret_tables = _ret_tables(ret_heads)
    lg = jnp.log(1.0 - 2.0 ** (-5.0 - jnp.arange(ret_heads, dtype=F32)))
    decay_rows = jnp.broadcast_to(jnp.exp(lg)[:, None], (ret_heads, HEAD_DIM))
    bias_tiles = _moba_bias_tiles(rel_bias)
    bias_rows = _sample_bias_rows(rel_bias, past_len=past_len, page_size=page_size)
    conv0 = jnp.zeros((batch, CONV_WIDTH - 1, conv_ch), F32)
    ret0 = jnp.zeros((batch, ret_heads // PAIR, LANES, LANES), F32)
    nb = seq // MOBA_BLOCK
    nb_pad = -(-nb // 8) * 8

    hp = x_prompt.reshape(batch * seq, d)
    hs = x_sample.reshape(db, d)
    outs = {name: [] for name in ("ks", "vs", "rp", "rs", "cp", "cs")}
    kt_buf = vt_buf = None
    for l in range(depth):
        last = l == depth - 1
        conv_in, ret_in, mq, mgate, kt_buf, vt_buf, kmean = _inproj_prompt(
            hp, norm_g[l], w_rows[l], w_kv_t[l], w_k32[l], kt_buf, vt_buf, layer=l, depth=depth, batch=batch,
            seq=seq, tm=PROMPT_TM)
        conv_out, conv_state = _conv_prompt(conv_in, conv0, conv_w[l], conv_b[l], conv_ln_g[l], conv_ln_b[l],
                                            batch=batch, seq=seq, ts=CONV_TS)
        ret_out, ret_state = _ret_prompt(ret_in, ret0, cos_p, sin_p, ret_tables, batch=batch, seq=seq)
        kmean = jnp.pad(kmean.reshape(batch, nb, -1), ((0, 0), (0, nb_pad - nb), (0, 0)))
        moba_out = _moba_prompt(mq, kt_buf, vt_buf, kmean, mgate, bias_tiles, layer=l, batch=batch, seq=seq)
        hp = _outproj(hp, conv_out, ret_out, moba_out, w_out_b[l], final_g, tm=PROMPT_TM, final_norm=last)
        outs["rp"].append(_unpair_states(ret_state))
        outs["cp"].append(conv_state)
        conv_in, ret_in, mq, mk, mv, mgate = _inproj(hs, norm_g[l], w_in_b[l])
        conv_out, conv_state = _conv_sample(conv_in, state_conv[l], conv_w[l], conv_b[l], conv_ln_g[l],
                                            conv_ln_b[l])
        ret_out, ret_state = _ret_sample(ret_in, state_ret[l], cos_s, sin_s, decay_rows)
        moba_out = _moba_sample(mq, mk, mv, mgate, k_cache_t, v_cache_t, l * n_phys, page_table, rel_bias,
                                bias_rows, page_size=page_size)
        hs = _outproj(hs, conv_out, ret_out, moba_out, w_out_b[l], final_g, tm=db, final_norm=last)
        outs["ks"].append(mk.reshape(db, 1, moba_heads, HEAD_DIM))
        outs["vs"].append(mv.reshape(db, 1, moba_heads, HEAD_DIM))
        outs["rs"].append(ret_state)
        outs["cs"].append(conv_state)

    st = {name: jnp.stack(vals) for name, vals in outs.items()}
    rows_major = lambda t: jnp.transpose(t.reshape(depth, batch, moba_heads, HEAD_DIM, seq), (0, 1, 4, 2, 3))
    return (hp.reshape(batch, seq, d), hs.reshape(db, 1, d), rows_major(kt_buf), rows_major(vt_buf),
            st["ks"], st["vs"], st["rp"], st["rs"], st["cp"], st["cs"])
```

```python
import functools
import math

import jax
import jax.numpy as jnp
from jax import lax
from jax.experimental import pallas as pl
from jax.experimental.pallas import tpu as pltpu

F32 = jnp.float32
BF16 = jnp.bfloat16

HEAD_DIM = 64
CONV_WIDTH = 31
MOBA_BLOCK = 256
TOP_K = 3
N_BUCKETS = 32
MAX_DISTANCE = 128
RET_CHUNK = 128
ROPE_BASE = 10000.0
EPS = 1e-6

LANES = 128
SUBLANES = 8
PAIR = LANES // HEAD_DIM
MASKED = -1e30
LOG2E = math.log2(math.e)
VMEM_LIMIT = 56 * 1024 * 1024
_CONTRACT_LAST = (((1,), (1,)), ((), ()))
_CONTRACT_FIRST = (((0,), (0,)), ((), ()))


def _silu(x):
    return x * jax.nn.sigmoid(x)


def _params(*sem):
    return pltpu.CompilerParams(dimension_semantics=sem, vmem_limit_bytes=VMEM_LIMIT)


def _inproj_kernel(x_ref, g_ref, w_ref, *out_refs, splits):
    x = x_ref[...]
    h = x * lax.rsqrt(jnp.mean(x * x, axis=-1, keepdims=True) + EPS) * g_ref[...]
    hb = h.astype(BF16)
    for (c0, c1), o_ref in zip(splits, out_refs):
        o_ref[...] = jnp.dot(hb, w_ref[:, c0:c1], preferred_element_type=F32)


def _inproj(x2d, g, w_bf16):
    m, d = x2d.shape
    conv_ch, ret_w, moba_w = d // 4, (d // 256) * HEAD_DIM, (d // 128) * HEAD_DIM
    widths = (3 * conv_ch, 4 * ret_w, moba_w, moba_w, moba_w, moba_w)
    edges = [0]
    for wd in widths:
        edges.append(edges[-1] + wd)
    splits = tuple(zip(edges[:-1], edges[1:]))
    assert edges[-1] == w_bf16.shape[1]
    return pl.pallas_call(
        functools.partial(_inproj_kernel, splits=splits),
        grid=(1,),
        in_specs=[pl.BlockSpec((m, d), lambda i: (0, 0)),
                  pl.BlockSpec((1, d), lambda i: (0, 0)),
                  pl.BlockSpec(w_bf16.shape, lambda i: (0, 0))],
        out_specs=[pl.BlockSpec((m, wd), lambda i: (0, 0)) for wd in widths],
        out_shape=[jax.ShapeDtypeStruct((m, wd), F32) for wd in widths],
        compiler_params=_params("arbitrary"),
        name="inproj_sample",
    )(x2d, g.reshape(1, d), w_bf16)


def _inproj_prompt_kernel(x_ref, g_ref, w_ref, wkv_ref, *refs, splits, n_mean, aliased):
    if aliased:
        refs = refs[2:]
    conv_ref, ret_ref, q_ref, gate_ref, kt_ref, vt_ref, km_ref = refs
    x = x_ref[...]
    h = x * lax.rsqrt(jnp.mean(x * x, axis=-1, keepdims=True) + EPS) * g_ref[...]
    hb = h.astype(BF16)
    for (c0, c1), o_ref in zip(splits, (conv_ref, ret_ref, q_ref, gate_ref)):
        o_ref[...] = jnp.dot(hb, w_ref[:, c0:c1], preferred_element_type=F32)
    mw = kt_ref.shape[2]
    kt = lax.dot_general(wkv_ref[0:mw, :], hb, _CONTRACT_LAST, preferred_element_type=F32)
    kt_ref[0, 0] = kt
    vt_ref[0, 0] = lax.dot_general(wkv_ref[mw:2 * mw, :], hb, _CONTRACT_LAST, preferred_element_type=F32)
    for i in range(n_mean):
        blk = kt[:, i * MOBA_BLOCK:(i + 1) * MOBA_BLOCK]
        km_ref[0, :, i:i + 1] = jnp.sum(blk, axis=-1, keepdims=True) * (1.0 / MOBA_BLOCK)


def _inproj_prompt(x2d, g, w_rows, w_kv_t, kt_buf, vt_buf, *, layer, depth, batch, seq, tm):
    m, d = x2d.shape
    conv_ch, ret_w, mw = d // 4, (d // 256) * HEAD_DIM, (d // 128) * HEAD_DIM
    widths = (3 * conv_ch, 4 * ret_w, mw, mw)
    edges = [0]
    for wd in widths:
        edges.append(edges[-1] + wd)
    splits = tuple(zip(edges[:-1], edges[1:]))
    assert edges[-1] == w_rows.shape[1] and seq % tm == 0 and tm % MOBA_BLOCK == 0
    n_mean = tm // MOBA_BLOCK
    assert n_mean <= 8
    per_seq = seq // tm
    aliased = kt_buf is not None
    row = lambda wd: pl.BlockSpec((tm, wd), lambda i: (i, 0))
    const = lambda a: pl.BlockSpec(a.shape, lambda i: (0,) * a.ndim)
    kv_spec = pl.BlockSpec((1, 1, mw, tm), lambda i: (layer, i // per_seq, 0, i % per_seq))
    kv_shape = jax.ShapeDtypeStruct((depth, batch, mw, seq), F32)
    in_specs = [row(d), pl.BlockSpec((1, d), lambda i: (0, 0)), const(w_rows), const(w_kv_t)]
    args = [x2d, g.reshape(1, d), w_rows, w_kv_t]
    aliases = {}
    if aliased:
        in_specs += [pl.BlockSpec(memory_space=pl.ANY)] * 2
        args += [kt_buf, vt_buf]
        aliases = {4: 4, 5: 5}
    *outs, km_t = pl.pallas_call(
        functools.partial(_inproj_prompt_kernel, splits=splits, n_mean=n_mean, aliased=aliased),
        grid=(m // tm,),
        in_specs=in_specs,
        out_specs=[row(wd) for wd in widths] + [kv_spec, kv_spec,
                                                pl.BlockSpec((1, mw, n_mean), lambda i: (i, 0, 0))],
        out_shape=[jax.ShapeDtypeStruct((m, wd), F32) for wd in widths]
                  + [kv_shape, kv_shape, jax.ShapeDtypeStruct((m // tm, mw, n_mean), F32)],
        input_output_aliases=aliases,
        compiler_params=_params("parallel"),
        name="inproj_prompt",
    )(*args)
    kmean = jnp.transpose(km_t.reshape(batch, per_seq, mw, n_mean), (0, 1, 3, 2)).reshape(batch, -1, mw)
    return (*outs, kmean)


def _outproj_kernel(x_ref, c_ref, r_ref, m_ref, w_ref, fg_ref, y_ref, *, conv_ch, ret_w, final_norm):
    y = x_ref[...]
    y += jnp.dot(c_ref[...].astype(BF16), w_ref[0:conv_ch, :], preferred_element_type=F32)
    y += jnp.dot(r_ref[...].astype(BF16), w_ref[conv_ch:conv_ch + ret_w, :], preferred_element_type=F32)
    y += jnp.dot(m_ref[...].astype(BF16), w_ref[conv_ch + ret_w:, :], preferred_element_type=F32)
    if final_norm:
        y = y * lax.rsqrt(jnp.mean(y * y, axis=-1, keepdims=True) + EPS) * fg_ref[...]
    y_ref[...] = y


def _outproj(x2d, conv_out, ret_out, moba_out, w_bf16, final_g, *, tm, final_norm):
    m, d = x2d.shape
    conv_ch, ret_w, moba_w = conv_out.shape[1], ret_out.shape[1], moba_out.shape[1]
    row = lambda wd: pl.BlockSpec((tm, wd), lambda i: (i, 0))
    return pl.pallas_call(
        functools.partial(_outproj_kernel, conv_ch=conv_ch, ret_w=ret_w, final_norm=final_norm),
        grid=(m // tm,),
        in_specs=[row(d), row(conv_ch), row(ret_w), row(moba_w),
                  pl.BlockSpec(w_bf16.shape, lambda i: (0, 0)),
                  pl.BlockSpec((1, d), lambda i: (0, 0))],
        out_specs=row(d),
        out_shape=jax.ShapeDtypeStruct((m, d), F32),
        compiler_params=_params("parallel"),
        name="outproj",
    )(x2d, conv_out, ret_out, moba_out, w_bf16, final_g.reshape(1, d))


def _layernorm_silu(y, g, b):
    mu = jnp.mean(y, axis=-1, keepdims=True)
    yc = y - mu
    yn = yc * lax.rsqrt(jnp.mean(yc * yc, axis=-1, keepdims=True) + EPS)
    return _silu(yn * g + b)


CONV_PAD = 32
CONV_ROWS = 64


def _conv_kernel(in_ref, prev_ref, w_ref, b_ref, g_ref, beta_ref, o_ref, st_ref, buf, shifted, *, ts, ch):
    t = pl.program_id(1)
    keep = CONV_WIDTH - 1
    lo = CONV_PAD - keep

    @pl.when(t == 0)
    def _():
        buf[lo:CONV_PAD, :] = prev_ref[0]

    @pl.when(t > 0)
    def _():
        buf[lo:CONV_PAD, :] = buf[ts + lo:ts + CONV_PAD, :]

    buf[CONV_PAD:CONV_PAD + ts, :] = in_ref[:, 0:ch] * jax.nn.sigmoid(in_ref[:, ch:2 * ch])
    used = ts + CONV_PAD - SUBLANES
    for k in range(1, SUBLANES):
        shifted[k - 1, 0:used, :] = buf[k:k + used, :]
    for r0 in range(0, ts, CONV_ROWS):
        acc = jnp.zeros((CONV_ROWS, ch), F32)
        for j in range(CONV_WIDTH):
            k = (lo + j) % SUBLANES
            base = r0 + lo + j - k
            rows = buf[base:base + CONV_ROWS, :] if k == 0 else shifted[k - 1, base:base + CONV_ROWS, :]
            acc += rows * w_ref[j:j + 1, :]
        y = _layernorm_silu(acc + b_ref[...], g_ref[...], beta_ref[...])
        o_ref[r0:r0 + CONV_ROWS, :] = y * _silu(in_ref[r0:r0 + CONV_ROWS, 2 * ch:3 * ch])

    @pl.when(t == pl.num_programs(1) - 1)
    def _():
        st_ref[0] = buf[ts + lo:ts + CONV_PAD, :]


def _conv_prompt(conv_in, prev, w, b, g, beta, *, batch, seq, ts):
    ch = conv_in.shape[1] // 3
    nts = seq // ts
    assert seq % ts == 0 and ts % CONV_ROWS == 0 and ts >= CONV_PAD
    vec = lambda: pl.BlockSpec((1, ch), lambda bi, t: (0, 0))
    return pl.pallas_call(
        functools.partial(_conv_kernel, ts=ts, ch=ch),
        grid=(batch, nts),
        in_specs=[pl.BlockSpec((ts, 3 * ch), lambda bi, t: (bi * nts + t, 0)),
                  pl.BlockSpec((1, CONV_WIDTH - 1, ch), lambda bi, t: (bi, 0, 0)),
                  pl.BlockSpec((CONV_WIDTH, ch), lambda bi, t: (0, 0)),
                  vec(), vec(), vec()],
        out_specs=[pl.BlockSpec((ts, ch), lambda bi, t: (bi * nts + t, 0)),
                   pl.BlockSpec((1, CONV_WIDTH - 1, ch), lambda bi, t: (bi, 0, 0))],
        out_shape=[jax.ShapeDtypeStruct((batch * seq, ch), F32),
                   jax.ShapeDtypeStruct((batch, CONV_WIDTH - 1, ch), F32)],
        scratch_shapes=[pltpu.VMEM((ts + CONV_PAD, ch), F32),
                        pltpu.VMEM((SUBLANES - 1, ts + CONV_PAD, ch), F32)],
        compiler_params=_params("parallel", "arbitrary"),
        name="conv_prompt",
    )(conv_in, prev, w, b.reshape(1, ch), g.reshape(1, ch), beta.reshape(1, ch))


SAMPLE_SEQS = 8


def _seqs_per_step(db):
    return math.gcd(db, SAMPLE_SEQS)


def _conv_step_kernel(in_ref, prev_ref, w_ref, b_ref, g_ref, beta_ref, o_ref, st_ref, *, ch):
    keep = CONV_WIDTH - 1
    for s in range(in_ref.shape[0]):
        row = in_ref[s]
        u = row[:, 0:ch] * jax.nn.sigmoid(row[:, ch:2 * ch])
        prev = prev_ref[s]
        acc = jnp.sum(prev * w_ref[0:keep, :], axis=0, keepdims=True) + u * w_ref[keep:keep + 1, :]
        y = _layernorm_silu(acc + b_ref[...], g_ref[...], beta_ref[...])
        o_ref[s] = y * _silu(row[:, 2 * ch:3 * ch])
        st_ref[s, 0:keep - 1, :] = prev_ref[s, 1:keep, :]
        st_ref[s, keep - 1:keep, :] = u


def _conv_sample(conv_in, prev, w, b, g, beta):
    db, ch3 = conv_in.shape
    ch = ch3 // 3
    keep = CONV_WIDTH - 1
    n = _seqs_per_step(db)
    vec = lambda: pl.BlockSpec((1, ch), lambda bi: (0, 0))
    out, st = pl.pallas_call(
        functools.partial(_conv_step_kernel, ch=ch),
        grid=(db // n,),
        in_specs=[pl.BlockSpec((n, 1, ch3), lambda bi: (bi, 0, 0)),
                  pl.BlockSpec((n, keep, ch), lambda bi: (bi, 0, 0)),
                  pl.BlockSpec((CONV_WIDTH, ch), lambda bi: (0, 0)),
                  vec(), vec(), vec()],
        out_specs=[pl.BlockSpec((n, 1, ch), lambda bi: (bi, 0, 0)),
                   pl.BlockSpec((n, keep, ch), lambda bi: (bi, 0, 0))],
        out_shape=[jax.ShapeDtypeStruct((db, 1, ch), F32),
                   jax.ShapeDtypeStruct((db, keep, ch), F32)],
        compiler_params=_params("parallel"),
        name="conv_sample",
    )(conv_in.reshape(db, 1, ch3), prev, w, b.reshape(1, ch), g.reshape(1, ch), beta.reshape(1, ch))
    return out.reshape(db, ch), st


def _rope_rows(x, cos, sin_signed):
    w = x.shape[-1]
    half = HEAD_DIM // 2
    lane = lax.broadcasted_iota(jnp.int32, (1, w), 1)
    partner = jnp.where((lane % HEAD_DIM) < half, pltpu.roll(x, w - half, 1), pltpu.roll(x, half, 1))
    return x * cos + partner * sin_signed


def _half_sums(x, first_half):
    a = jnp.sum(jnp.where(first_half, x, 0.0), axis=-1, keepdims=True)
    b = jnp.sum(jnp.where(first_half, 0.0, x), axis=-1, keepdims=True)
    return jnp.where(first_half, a, b)


RET_SEQS = 2
RET_STEP_CHUNKS = 4


def _ret_kernel(in_ref, r0_ref, cos_ref, sin_ref, dmat_ref, cross_ref, wk_ref, decay_ref,
                o_ref, rout_ref, r_scr, *, rw):
    in_refs = [in_ref.at[b] for b in range(RET_SEQS)]
    o_refs = [o_ref.at[b] for b in range(RET_SEQS)]
    c = pl.program_id(1)
    npair = rw // LANES

    @pl.when(c == 0)
    def _():
        r_scr[...] = r0_ref[...]

    lane = lax.broadcasted_iota(jnp.int32, (1, LANES), 1)
    first = lane < HEAD_DIM
    row = lax.broadcasted_iota(jnp.int32, (LANES, 1), 0)
    blockdiag = (row < HEAD_DIM) == first
    problems = [(b, p) for b in range(RET_SEQS) for p in range(npair)]
    lanes_of = lambda p: slice(p * LANES, (p + 1) * LANES)
    for ci in range(in_ref.shape[1] // RET_CHUNK):
        rows = slice(ci * RET_CHUNK, (ci + 1) * RET_CHUNK)
        cos, sin = cos_ref[rows, :], sin_ref[rows, :]
        q = [_rope_rows(r[rows, 0:rw], cos, sin) for r in in_refs]
        k = [_rope_rows(r[rows, rw:2 * rw], cos, sin) * (HEAD_DIM ** -0.5) for r in in_refs]
        qb = {(b, p): q[b][:, lanes_of(p)] for b, p in problems}
        kp = {(b, p): k[b][:, lanes_of(p)] for b, p in problems}
        vb = {(b, p): in_refs[b][rows, 2 * rw + p * LANES:2 * rw + (p + 1) * LANES].astype(BF16)
              for b, p in problems}
        kb = {pr: kp[pr].astype(BF16) for pr in problems}
        carry = {(b, p): jnp.dot(qb[b, p].astype(BF16), r_scr[b, p].astype(BF16), preferred_element_type=F32)
                 for b, p in problems}
        scores = {}
        for b, p in problems:
            for hh in range(PAIR):
                own = first if hh == 0 else jnp.logical_not(first)
                qh = jnp.where(own, qb[b, p], 0.0).astype(BF16)
                scores[b, p, hh] = lax.dot_general(qh, kb[b, p], _CONTRACT_LAST, preferred_element_type=F32)
        kv = {(b, p): lax.dot_general((kp[b, p] * wk_ref[p]).astype(BF16), vb[b, p], _CONTRACT_FIRST,
                                      preferred_element_type=F32) for b, p in problems}
        inner = {(b, p, hh): jnp.dot((scores[b, p, hh] * dmat_ref[PAIR * p + hh]).astype(BF16), vb[b, p],
                                     preferred_element_type=F32)
                 for b, p in problems for hh in range(PAIR)}
        for b, p in problems:
            o = carry[b, p] * cross_ref[p] + jnp.where(first, inner[b, p, 0], inner[b, p, 1])
            r_scr[b, p] = r_scr[b, p] * decay_ref[p] + jnp.where(blockdiag, kv[b, p], 0.0)
            ms = _half_sums(o * o, first) * (1.0 / HEAD_DIM)
            gate = in_refs[b][rows, 3 * rw + p * LANES:3 * rw + (p + 1) * LANES]
            o_refs[b][rows, lanes_of(p)] = o * lax.rsqrt(ms + EPS) * _silu(gate)

    @pl.when(c == pl.num_programs(1) - 1)
    def _():
        rout_ref[...] = r_scr[...]


def _ret_tables(heads):
    c = RET_CHUNK
    lg = jnp.log(1.0 - 2.0 ** (-5.0 - jnp.arange(heads, dtype=F32)))
    t = jnp.arange(c, dtype=F32)
    diff = t[:, None] - t[None, :]
    dmat = jnp.where(diff >= 0, jnp.exp(lg[:, None, None] * jnp.maximum(diff, 0.0)), 0.0)
    cross = jnp.exp(lg[None, :] * (t[:, None] + 1.0))
    wk = jnp.exp(lg[:, None] * (c - 1.0 - t[None, :]))
    chunk_decay = jnp.exp(lg * c)
    per_lane = lambda a: jnp.repeat(a, HEAD_DIM, axis=-1)
    to_pairs = lambda a: jnp.swapaxes(a.reshape(c, heads // PAIR, LANES), 0, 1)
    cross_p = to_pairs(per_lane(cross))
    wk_p = to_pairs(per_lane(wk.T))
    decay_p = to_pairs(per_lane(jnp.broadcast_to(chunk_decay[None, :], (c, heads))))
    return dmat, cross_p, wk_p, decay_p


def _unpair_states(pairs):
    a = pairs[:, :, :HEAD_DIM, :HEAD_DIM]
    b = pairs[:, :, HEAD_DIM:, HEAD_DIM:]
    return jnp.stack([a, b], axis=2).reshape(pairs.shape[0], -1, HEAD_DIM, HEAD_DIM)


def _ret_prompt(ret_in, state0_pairs, cos_t, sin_t, tables, *, batch, seq):
    rw = ret_in.shape[1] // 4
    npair = rw // LANES
    c = RET_CHUNK * RET_STEP_CHUNKS
    nc = seq // c
    assert seq % c == 0 and batch % RET_SEQS == 0
    dmat, cross_p, wk_p, decay_p = tables
    const = lambda a: pl.BlockSpec(a.shape, lambda bi, ci: (0,) * a.ndim)
    state_spec = pl.BlockSpec((RET_SEQS, npair, LANES, LANES), lambda bi, ci: (bi, 0, 0, 0))
    out, rout = pl.pallas_call(
        functools.partial(_ret_kernel, rw=rw),
        grid=(batch // RET_SEQS, nc),
        in_specs=[pl.BlockSpec((RET_SEQS, c, 4 * rw), lambda bi, ci: (bi, ci, 0)),
                  state_spec,
                  pl.BlockSpec((c, rw), lambda bi, ci: (ci, 0)),
                  pl.BlockSpec((c, rw), lambda bi, ci: (ci, 0)),
                  const(dmat), const(cross_p), const(wk_p), const(decay_p)],
        out_specs=[pl.BlockSpec((RET_SEQS, c, rw), lambda bi, ci: (bi, ci, 0)), state_spec],
        out_shape=[jax.ShapeDtypeStruct((batch, seq, rw), F32),
                   jax.ShapeDtypeStruct((batch, npair, LANES, LANES), F32)],
        scratch_shapes=[pltpu.VMEM((RET_SEQS, npair, LANES, LANES), F32)],
        compiler_params=_params("parallel", "arbitrary"),
        name="ret_prompt",
    )(ret_in.reshape(batch, seq, 4 * rw), state0_pairs, cos_t, sin_t, dmat, cross_p, wk_p, decay_p)
    return out.reshape(batch * seq, rw), rout


def _ret_step_kernel(in_ref, r_ref, cos_ref, sin_ref, decay_ref, o_ref, rout_ref, *, rw):
    heads = rw // HEAD_DIM
    eye = (lax.broadcasted_iota(jnp.int32, (HEAD_DIM, HEAD_DIM), 0)
           == lax.broadcasted_iota(jnp.int32, (HEAD_DIM, HEAD_DIM), 1))
    col = lambda r: jnp.sum(jnp.where(eye, r, 0.0), axis=-1, keepdims=True)
    for s in range(in_ref.shape[0]):
        row = in_ref[s]
        q = _rope_rows(row[:, 0:rw], cos_ref[...], sin_ref[...])
        k = _rope_rows(row[:, rw:2 * rw], cos_ref[...], sin_ref[...]) * (HEAD_DIM ** -0.5)
        v = row[:, 2 * rw:3 * rw]
        gate = row[:, 3 * rw:4 * rw]
        outs = []
        for h in range(heads):
            sl = slice(h * HEAD_DIM, (h + 1) * HEAD_DIM)
            qh, kh, vh = q[:, sl], k[:, sl], v[:, sl]
            decay = decay_ref[h:h + 1, :]
            state = r_ref[s, h]
            o = jnp.sum(qh * kh, axis=-1, keepdims=True) * vh
            o = o + jnp.sum(col(qh) * state, axis=0, keepdims=True) * decay
            rout_ref[s, h] = state * decay + col(kh) * vh
            o = o * lax.rsqrt(jnp.mean(o * o, axis=-1, keepdims=True) + EPS)
            outs.append(o * _silu(gate[:, sl]))
        o_ref[s] = jnp.concatenate(outs, axis=-1)


def _ret_sample(ret_in, state, cos_row, sin_row, decay_rows):
    db, rw4 = ret_in.shape
    rw = rw4 // 4
    heads = rw // HEAD_DIM
    n = _seqs_per_step(db)
    out, rout = pl.pallas_call(
        functools.partial(_ret_step_kernel, rw=rw),
        grid=(db // n,),
        in_specs=[pl.BlockSpec((n, 1, rw4), lambda bi: (bi, 0, 0)),
                  pl.BlockSpec((n, heads, HEAD_DIM, HEAD_DIM), lambda bi: (bi, 0, 0, 0)),
                  pl.BlockSpec((1, rw), lambda bi: (0, 0)),
                  pl.BlockSpec((1, rw), lambda bi: (0, 0)),
                  pl.BlockSpec((heads, HEAD_DIM), lambda bi: (0, 0))],
        out_specs=[pl.BlockSpec((n, 1, rw), lambda bi: (bi, 0, 0)),
                   pl.BlockSpec((n, heads, HEAD_DIM, HEAD_DIM), lambda bi: (bi, 0, 0, 0))],
        out_shape=[jax.ShapeDtypeStruct((db, 1, rw), F32),
                   jax.ShapeDtypeStruct(state.shape, F32)],
        compiler_params=_params("parallel"),
        name="ret_sample",
    )(ret_in.reshape(db, 1, rw4), state, cos_row, sin_row, decay_rows)
    return out.reshape(db, rw), rout


def _rope_tables(pos, heads):
    half = HEAD_DIM // 2
    freqs = ROPE_BASE ** (-jnp.arange(half, dtype=F32) / half)
    ang = pos.astype(F32)[:, None] * freqs[None, :]
    cos, sin = jnp.cos(ang), jnp.sin(ang)
    cos_t = jnp.tile(jnp.concatenate([cos, cos], axis=-1), (1, heads))
    sin_t = jnp.tile(jnp.concatenate([-sin, sin], axis=-1), (1, heads))
    return cos_t, sin_t


def _rel_bucket(d):
    n = jnp.maximum(d, 0)
    max_exact = N_BUCKETS // 2
    large = max_exact + (jnp.log(jnp.maximum(n, 1).astype(F32) / max_exact)
                         / math.log(MAX_DISTANCE / max_exact) * (N_BUCKETS - max_exact)).astype(jnp.int32)
    large = jnp.minimum(large, N_BUCKETS - 1)
    return jnp.where(n < max_exact, n, large)


def _block_rank_penalty(scores, past):
    nb = scores.shape[0]
    blk = lax.broadcasted_iota(jnp.int32, scores.shape, 0)
    sm = jnp.where(past, scores, -jnp.inf)
    rank = jnp.zeros(scores.shape, F32)
    for m in range(nb):
        other = sm[m:m + 1, :]
        beats = (other > sm) | ((other == sm) & (m < blk))
        rank += jnp.where(beats, 1.0, 0.0)
    return jnp.where(past & (rank < TOP_K), 0.0, MASKED)


def _moba_kernel(q_ref, kt_ref, vt_ref, km_ref, gate_ref, bias_ref, o_ref,
                 kext_scr, vext_scr, qext_scr, m_scr, acc_scr, *, nb):
    tq = MOBA_BLOCK
    seq = kt_ref.shape[-1]
    nq = seq // tq
    lane = lax.broadcasted_iota(jnp.int32, (1, LANES), 1)
    row = lax.broadcasted_iota(jnp.int32, (LANES, 1), 0)

    for c0 in range(0, seq, tq):
        kt = kt_ref[0, 0, :, c0:c0 + tq]
        vt = vt_ref[0, 0, :, c0:c0 + tq]
        for hh in range(PAIR):
            own_rows = (row < HEAD_DIM) if hh == 0 else (row >= HEAD_DIM)
            flag_row = c0 // tq + (HEAD_DIM if hh == 0 else 0)
            kext_scr[hh, :, c0:c0 + tq] = jnp.where(own_rows, kt, jnp.where(row == flag_row, 1.0, 0.0)).astype(BF16)
            vext_scr[hh, :, c0:c0 + tq] = jnp.where(own_rows, vt, 1.0).astype(BF16)

    km = km_ref[0]
    blk_id = lax.broadcasted_iota(jnp.int32, (nb, seq), 0)
    q_blk = lax.broadcasted_iota(jnp.int32, (nb, seq), 1) // tq
    for hh in range(PAIR):
        own = (lane < HEAD_DIM) if hh == 0 else (lane >= HEAD_DIM)
        scores = lax.dot_general(jnp.where(own, km, 0.0), q_ref[...], _CONTRACT_LAST,
                                 precision=lax.Precision.HIGHEST, preferred_element_type=F32)
        pen = jnp.where(blk_id == q_blk, 0.0, _block_rank_penalty(scores, blk_id < q_blk))
        pad = jnp.zeros((HEAD_DIM, tq), F32)
        tail = jnp.zeros((HEAD_DIM - nb, tq), F32)
        for c0 in range(0, seq, tq):
            pen_c = pen[:, c0:c0 + tq]
            pen_rows = [pad, pen_c, tail] if hh == 0 else [pen_c, tail, pad]
            pen_t = jnp.transpose(jnp.concatenate(pen_rows, axis=0))
            q_c = q_ref[c0:c0 + tq, :] * (HEAD_DIM ** -0.5 * LOG2E)
            qext_scr[hh, c0:c0 + tq, :] = jnp.where(own, q_c, pen_t).astype(BF16)

    def steps_of(t):
        pairs = [(c, 2) for c in range(t - 1, -1, -2)]
        return pairs + ([(0, 1)] if t % 2 == 0 else [])

    schedule = [(t, c, n) for t in range(nq) for c, n in steps_of(t)]

    def logits(t, c, n):
        out = []
        for hh in range(PAIR):
            qx = qext_scr[hh, t * tq:(t + 1) * tq, :]
            kx = kext_scr[hh, :, c * tq:(c + n) * tq]
            s = jnp.dot(qx, kx, preferred_element_type=F32)
            out.append([s[:, i * tq:(i + 1) * tq] + bias_ref[hh, min(t - c - i, 2)] for i in range(n)])
        return out

    def softmax_pv(c, n, s_pair, first):
        for hh in range(PAIR):
            parts = [blk[:, k * LANES:(k + 1) * LANES] for blk in s_pair[hh] for k in range(tq // LANES)]
            top = jnp.max(functools.reduce(jnp.maximum, parts), axis=-1, keepdims=True)
            m_new = jnp.broadcast_to(top, (tq, LANES)) if first else jnp.maximum(m_scr[hh], top)
            probs = jnp.concatenate([jnp.exp2((part - m_new).astype(BF16)) for part in parts], axis=1)
            pv = lax.dot_general(probs, vext_scr[hh, :, c * tq:(c + n) * tq], _CONTRACT_LAST,
                                 preferred_element_type=F32)
            acc_scr[hh] = pv if first else jnp.exp2(m_scr[hh] - m_new) * acc_scr[hh] + pv
            m_scr[hh] = m_new

    s_cur = logits(*schedule[0])
    for idx, (t, c, n) in enumerate(schedule):
        s_next = logits(*schedule[idx + 1]) if idx + 1 < len(schedule) else None
        softmax_pv(c, n, s_cur, first=(c + n == t + 1))
        if c == 0:
            numer = jnp.where(lane < HEAD_DIM, acc_scr[0], acc_scr[1])
            denom = jnp.where(lane < HEAD_DIM, pltpu.roll(acc_scr[0], HEAD_DIM, 1),
                              pltpu.roll(acc_scr[1], HEAD_DIM, 1))
            rows = slice(t * tq, (t + 1) * tq)
            o_ref[rows, :] = numer / denom * _silu(gate_ref[rows, :])
        s_cur = s_next


def _toeplitz(vec, rows, cols):
    h, n = vec.shape
    assert n == rows + cols - 1
    flat = jnp.tile(vec, (1, rows + 1))[:, :rows * (n + 1)]
    return flat.reshape(h, rows, n + 1)[:, :, :cols]


def _moba_bias_tiles(rel_bias):
    blk = MOBA_BLOCK
    bias_t = rel_bias.astype(F32).T
    tiles = []
    for dist in range(3):
        d = dist * blk + jnp.arange(2 * blk - 1) - (blk - 1)
        g = bias_t[:, _rel_bucket(d)]
        if dist == 0:
            g = jnp.where((d >= 0)[None], g, -jnp.inf)
        tiles.append(_toeplitz(g, blk, blk)[:, :, ::-1])
    return jnp.stack(tiles, axis=1) * LOG2E


def _moba_prompt(mq, kt_buf, vt_buf, kmean, mgate, bias_tiles, *, layer, batch, seq):
    mw = mq.shape[1]
    npair = mw // LANES
    nq = seq // MOBA_BLOCK
    nb = kmean.shape[1]
    assert seq % MOBA_BLOCK == 0 and nb % 8 == 0 and 2 <= nq <= nb <= HEAD_DIM
    assert MOBA_BLOCK + 1 >= MAX_DISTANCE
    rows = lambda: pl.BlockSpec((seq, LANES), lambda b, p: (b, p))
    cols = lambda: pl.BlockSpec((1, 1, LANES, seq), lambda b, p: (layer, b, p, 0))
    return pl.pallas_call(
        functools.partial(_moba_kernel, nb=nb),
        grid=(batch, npair),
        in_specs=[rows(), cols(), cols(),
                  pl.BlockSpec((1, nb, LANES), lambda b, p: (b, 0, p)),
                  rows(),
                  pl.BlockSpec((PAIR, 3, MOBA_BLOCK, MOBA_BLOCK), lambda b, p: (p, 0, 0, 0))],
        out_specs=rows(),
        out_shape=jax.ShapeDtypeStruct((batch * seq, mw), F32),
        scratch_shapes=[pltpu.VMEM((PAIR, LANES, seq), BF16), pltpu.VMEM((PAIR, LANES, seq), BF16),
                        pltpu.VMEM((PAIR, seq, LANES), BF16)]
                       + [pltpu.VMEM((PAIR, MOBA_BLOCK, LANES), F32)] * 2,
        compiler_params=_params("parallel", "parallel"),
        name="moba_prompt",
    )(mq, kt_buf, vt_buf, kmean, mgate, bias_tiles)


PAGES_PER_STEP = 32


def _page_score_kernel(pt_ref, q_ref, *refs, pages_per_block, heads):
    n_step_pages = len(refs) - 1
    page_refs, s_ref = refs[:n_step_pages], refs[n_step_pages]
    g = pl.program_id(1)

    @pl.when(g == 0)
    def _():
        s_ref[...] = jnp.zeros(s_ref.shape, F32)

    lane = lax.broadcasted_iota(jnp.int32, (1, LANES), 1)
    head_row = lax.broadcasted_iota(jnp.int32, (heads, 1), 0)
    blocks_per_step = n_step_pages // pages_per_block
    for bi in range(blocks_per_step):
        blk = g * blocks_per_step + bi
        pages = page_refs[bi * pages_per_block:(bi + 1) * pages_per_block]
        upd = jnp.zeros((heads, LANES), F32)
        for h in range(heads):
            rows = slice(h * HEAD_DIM, (h + 1) * HEAD_DIM)
            ksum = pages[0][0, rows, :]
            for page in pages[1:]:
                ksum = ksum + page[0, rows, :]
            part = jnp.sum(ksum * q_ref[0, rows, :], axis=0, keepdims=True)
            score = jnp.sum(part, axis=-1, keepdims=True) * (1.0 / MOBA_BLOCK)
            upd = jnp.where((head_row == h) & (lane == blk), score, upd)
        s_ref[0] += upd


def _page_scores(pt_flat, q_rep, cache_pages, *, n_pages, page_size, page0):
    db, width, _ = q_rep.shape
    heads = width // HEAD_DIM
    pages_per_block = MOBA_BLOCK // page_size
    step_pages = math.gcd(n_pages, PAGES_PER_STEP)
    assert MOBA_BLOCK % page_size == 0 and step_pages % pages_per_block == 0
    assert n_pages // pages_per_block <= LANES and page_size == LANES
    steps = n_pages // step_pages

    def page_spec(i):
        return pl.BlockSpec((1, width, page_size),
                            lambda b, g, pt: (page0 + pt[b * n_pages + g * step_pages + i], 0, 0))

    grid_spec = pltpu.PrefetchScalarGridSpec(
        num_scalar_prefetch=1,
        grid=(db, steps),
        in_specs=[pl.BlockSpec((1, width, LANES), lambda b, g, pt: (b, 0, 0))]
                 + [page_spec(i) for i in range(step_pages)],
        out_specs=pl.BlockSpec((1, heads, LANES), lambda b, g, pt: (b, 0, 0)),
    )
    return pl.pallas_call(
        functools.partial(_page_score_kernel, pages_per_block=pages_per_block, heads=heads),
        grid_spec=grid_spec,
        out_shape=jax.ShapeDtypeStruct((db, heads, LANES), F32),
        compiler_params=_params("parallel", "arbitrary"),
        name="page_scores",
    )(pt_flat, q_rep, *([cache_pages] * step_pages))


def _topk_kernel(s_ref, sel_ref, *, n_blocks):
    s = s_ref[...]
    lane = lax.broadcasted_iota(jnp.int32, s.shape, 1).astype(F32)
    s = jnp.where(lane < n_blocks, s, -jnp.inf)
    sel = jnp.zeros(s.shape, F32)
    for j in range(TOP_K):
        best = jnp.max(s, axis=-1, keepdims=True)
        idx = jnp.min(jnp.where(s == best, lane, float(LANES)), axis=-1, keepdims=True)
        sel = jnp.where(lane == j, idx, sel)
        s = jnp.where(lane == idx, -jnp.inf, s)
    sel_ref[...] = sel.astype(jnp.int32)


def _topk_blocks(scores, *, n_blocks):
    db, heads, _ = scores.shape
    assert n_blocks >= TOP_K
    spec = pl.BlockSpec((db * heads, LANES), lambda i: (0, 0))
    return pl.pallas_call(
        functools.partial(_topk_kernel, n_blocks=n_blocks),
        grid=(1,),
        in_specs=[spec],
        out_specs=spec,
        out_shape=jax.ShapeDtypeStruct((db * heads, LANES), jnp.int32),
        compiler_params=_params("arbitrary"),
        name="topk_blocks",
    )(scores.reshape(db * heads, LANES)).reshape(db, heads, LANES)


ATTEND_HEADS = 4


def _page_attend_kernel(pt_ref, seqp_ref, q_ref, kn_ref, vn_ref, gate_ref, bown_ref, bias_ref, *refs,
                        heads, n_sel_pages):
    n_tiles = ATTEND_HEADS * n_sel_pages
    k_refs, v_refs, o_ref = refs[:n_tiles], refs[n_tiles:2 * n_tiles], refs[2 * n_tiles]
    b = pl.program_id(0)
    h0 = pl.program_id(1) * ATTEND_HEADS
    diag = (lax.broadcasted_iota(jnp.int32, (HEAD_DIM, LANES), 0)
            == lax.broadcasted_iota(jnp.int32, (HEAD_DIM, LANES), 1))
    for i in range(ATTEND_HEADS):
        h = h0 + i
        dims = slice(i * HEAD_DIM, (i + 1) * HEAD_DIM)
        q = q_ref[0, dims, :] * (HEAD_DIM ** -0.5)
        logits = []
        for j in range(n_sel_pages):
            seq_page = seqp_ref[(b * heads + h) * n_sel_pages + j]
            bias = bias_ref[h, pl.ds(seq_page, 1), :]
            logits.append(jnp.sum(k_refs[i * n_sel_pages + j][0] * q, axis=0, keepdims=True) + bias)
        s_own = jnp.sum(kn_ref[0, dims, :] * q, axis=0, keepdims=True) + bown_ref[pl.ds(h, 1), :]
        top = functools.reduce(jnp.maximum, logits)
        m = jnp.maximum(jnp.max(top, axis=-1, keepdims=True), s_own)
        p_own = jnp.exp(s_own - m)
        psum = jnp.zeros_like(m)
        acc = jnp.zeros((HEAD_DIM, LANES), F32)
        for j in range(n_sel_pages):
            p = jnp.exp(logits[j] - m)
            psum += p
            acc += v_refs[i * n_sel_pages + j][0] * p
        denom = jnp.sum(psum, axis=-1, keepdims=True) + p_own
        out_rep = (jnp.sum(acc, axis=-1, keepdims=True) + p_own * vn_ref[0, dims, :]) / denom
        out_row = jnp.sum(jnp.where(diag, out_rep, 0.0), axis=0, keepdims=True)[:, :HEAD_DIM]
        o_ref[0, pl.ds(h, 1), :] = out_row * _silu(gate_ref[0, pl.ds(h, 1), :])


def _page_attend(pt_flat, seq_pages, q_rep, kn_rep, vn_rep, gate8, bown, bias_rows, k_tiles, v_tiles,
                 *, n_sel_pages, n_pages, page0):
    db, heads, _ = gate8.shape
    page_size = k_tiles.shape[-1]
    assert heads % ATTEND_HEADS == 0
    head_rep = lambda: pl.BlockSpec((1, ATTEND_HEADS * HEAD_DIM, LANES), lambda b, g, pt, sp: (b, g, 0))
    whole = lambda a: pl.BlockSpec(a.shape, lambda b, g, pt, sp: (0,) * a.ndim)

    def tile_spec(i, j):
        def index(b, g, pt, sp):
            h = g * ATTEND_HEADS + i
            page = page0 + pt[b * n_pages + sp[(b * heads + h) * n_sel_pages + j]]
            return (page * heads + h, 0, 0)
        return pl.BlockSpec((1, HEAD_DIM, page_size), index)

    tile_specs = [tile_spec(i, j) for i in range(ATTEND_HEADS) for j in range(n_sel_pages)]
    grid_spec = pltpu.PrefetchScalarGridSpec(
        num_scalar_prefetch=2,
        grid=(db, heads // ATTEND_HEADS),
        in_specs=[head_rep(), head_rep(), head_rep(),
                  pl.BlockSpec((1, heads, HEAD_DIM), lambda b, g, pt, sp: (b, 0, 0)),
                  whole(bown), whole(bias_rows)] + tile_specs * 2,
        out_specs=pl.BlockSpec((1, heads, HEAD_DIM), lambda b, g, pt, sp: (b, 0, 0)),
    )
    return pl.pallas_call(
        functools.partial(_page_attend_kernel, heads=heads, n_sel_pages=n_sel_pages),
        grid_spec=grid_spec,
        out_shape=jax.ShapeDtypeStruct((db, heads, HEAD_DIM), F32),
        compiler_params=_params("parallel", "arbitrary"),
        name="page_attend",
    )(pt_flat, seq_pages, q_rep, kn_rep, vn_rep, gate8, bown, bias_rows,
      *([k_tiles] * len(tile_specs)), *([v_tiles] * len(tile_specs)))


def _sample_bias_rows(rel_bias, *, past_len, page_size):
    kpos = jnp.arange(past_len)
    onehot = (_rel_bucket(past_len - kpos)[:, None] == jnp.arange(N_BUCKETS)[None, :]).astype(F32)
    bias = jnp.dot(onehot, rel_bias.astype(F32), precision=lax.Precision.HIGHEST)
    return bias.T.reshape(rel_bias.shape[1], past_len // page_size, page_size)


def _lane_replicated(x):
    return jnp.broadcast_to(x[:, :, None], x.shape + (LANES,))


def _moba_sample(mq, mk, mv, mgate, k_cache_t, v_cache_t, page0, page_table, rel_bias, bias_rows, *, page_size):
    db, mw = mq.shape
    heads = mw // HEAD_DIM
    n_pages = page_table.shape[1]
    past_len = n_pages * page_size
    pages_per_block = MOBA_BLOCK // page_size
    n_blocks = past_len // MOBA_BLOCK
    assert past_len % MOBA_BLOCK == 0
    n_rows = k_cache_t.shape[0]
    pt_flat = page_table.reshape(-1)
    q_rep = _lane_replicated(mq)
    scores = _page_scores(pt_flat, q_rep, k_cache_t.reshape(n_rows, mw, page_size), n_pages=n_pages,
                          page_size=page_size, page0=page0)
    sel = _topk_blocks(scores, n_blocks=n_blocks)[:, :, :TOP_K]
    seq_pages = (sel[..., None] * pages_per_block + jnp.arange(pages_per_block, dtype=jnp.int32)).reshape(-1)
    bown = jnp.broadcast_to(rel_bias.astype(F32)[0][:, None], (heads, LANES))
    out = _page_attend(pt_flat, seq_pages, q_rep, _lane_replicated(mk), _lane_replicated(mv),
                       mgate.reshape(db, heads, HEAD_DIM), bown, bias_rows,
                       k_cache_t.reshape(n_rows * heads, HEAD_DIM, page_size),
                       v_cache_t.reshape(n_rows * heads, HEAD_DIM, page_size),
                       n_sel_pages=TOP_K * pages_per_block, n_pages=n_pages, page0=page0)
    return out.reshape(db, mw)


PROMPT_TM = 512
CONV_TS = 512


def kernel(x_prompt, x_sample, cache_k, cache_v, page_table, state_ret, state_conv, norm_g, w_in, conv_w,
           conv_b, conv_ln_g, conv_ln_b, w_out, rel_bias, final_g):
    batch, seq, d = x_prompt.shape
    db, dec_seq, _ = x_sample.shape
    assert dec_seq == 1, "the sample path handles one new token per sequence"
    depth = w_in.shape[0]
    ret_heads = d // 256
    moba_heads = d // 128
    conv_ch = d // 4
    n_pages = page_table.shape[1]
    page_size = cache_k.shape[2]
    past_len = n_pages * page_size
    n_phys = cache_k.shape[1]
    tokens_minor = lambda c: jnp.transpose(c, (0, 1, 3, 4, 2)).reshape(depth * n_phys, moba_heads, HEAD_DIM,
                                                                         page_size)
    k_cache_t, v_cache_t = tokens_minor(cache_k), tokens_minor(cache_v)

    mw = moba_heads * HEAD_DIM
    k0 = 3 * conv_ch + 4 * ret_heads * HEAD_DIM + mw
    w_rows = jnp.concatenate([w_in[:, :, :k0], w_in[:, :, k0 + 2 * mw:]], axis=-1).astype(BF16)
    w_kv_t = jnp.swapaxes(w_in[:, :, k0:k0 + 2 * mw], 1, 2).astype(BF16)
    w_in_b = w_in.astype(BF16)
    w_out_b = w_out.astype(BF16)
    cos_p, sin_p = _rope_tables(jnp.arange(seq, dtype=jnp.int32), ret_heads)
    cos_s, sin_s = _rope_tables(past_len + jnp.arange(1, dtype=jnp.int32), ret_heads)
    ret_tables = _ret_tables(ret_heads)
    lg = jnp.log(1.0 - 2.0 ** (-5.0 - jnp.arange(ret_heads, dtype=F32)))
    decay_rows = jnp.broadcast_to(jnp.exp(lg)[:, None], (ret_heads, HEAD_DIM))
    bias_tiles = _moba_bias_tiles(rel_bias)
    bias_rows = _sample_bias_rows(rel_bias, past_len=past_len, page_size=page_size)
    conv0 = jnp.zeros((batch, CONV_WIDTH - 1, conv_ch), F32)
    ret0 = jnp.zeros((batch, ret_heads // PAIR, LANES, LANES), F32)
    nb = seq // MOBA_BLOCK
    nb_pad = -(-nb // 8) * 8

    hp = x_prompt.reshape(batch * seq, d)
    hs = x_sample.reshape(db, d)
    outs = {name: [] for name in ("ks", "vs", "rp", "rs", "cp", "cs")}
    kt_buf = vt_buf = None
    for l in range(depth):
        last = l == depth - 1
        conv_in, ret_in, mq, mgate, kt_buf, vt_buf, kmean = _inproj_prompt(
            hp, norm_g[l], w_rows[l], w_kv_t[l], kt_buf, vt_buf, layer=l, depth=depth, batch=batch,
            seq=seq, tm=PROMPT_TM)
        conv_out, conv_state = _conv_prompt(conv_in, conv0, conv_w[l], conv_b[l], conv_ln_g[l], conv_ln_b[l],
                                            batch=batch, seq=seq, ts=CONV_TS)
        ret_out, ret_state = _ret_prompt(ret_in, ret0, cos_p, sin_p, ret_tables, batch=batch, seq=seq)
        kmean = jnp.pad(kmean, ((0, 0), (0, nb_pad - nb), (0, 0)))
        moba_out = _moba_prompt(mq, kt_buf, vt_buf, kmean, mgate, bias_tiles, layer=l, batch=batch, seq=seq)
        hp = _outproj(hp, conv_out, ret_out, moba_out, w_out_b[l], final_g, tm=PROMPT_TM, final_norm=last)
        outs["rp"].append(_unpair_states(ret_state))
        outs["cp"].append(conv_state)
        conv_in, ret_in, mq, mk, mv, mgate = _inproj(hs, norm_g[l], w_in_b[l])
        conv_out, conv_state = _conv_sample(conv_in, state_conv[l], conv_w[l], conv_b[l], conv_ln_g[l],
                                            conv_ln_b[l])
        ret_out, ret_state = _ret_sample(ret_in, state_ret[l], cos_s, sin_s, decay_rows)
        moba_out = _moba_sample(mq, mk, mv, mgate, k_cache_t, v_cache_t, l * n_phys, page_table, rel_bias,
                                bias_rows, page_size=page_size)
        hs = _outproj(hs, conv_out, ret_out, moba_out, w_out_b[l], final_g, tm=db, final_norm=last)
        outs["ks"].append(mk.reshape(db, 1, moba_heads, HEAD_DIM))
        outs["vs"].append(mv.reshape(db, 1, moba_heads, HEAD_DIM))
        outs["rs"].append(ret_state)
        outs["cs"].append(conv_state)

    st = {name: jnp.stack(vals) for name, vals in outs.items()}
    rows_major = lambda t: jnp.transpose(t.reshape(depth, batch, moba_heads, HEAD_DIM, seq), (0, 1, 4, 2, 3))
    return (hp.reshape(batch, seq, d), hs.reshape(db, 1, d), rows_major(kt_buf), rows_major(vt_buf),
            st["ks"], st["vs"], st["rp"], st["rs"], st["cp"], st["cs"])
```

```python
import functools
import math

import jax
import jax.numpy as jnp
from jax import lax
from jax.experimental import pallas as pl
from jax.experimental.pallas import tpu as pltpu

F32 = jnp.float32
BF16 = jnp.bfloat16
MIX_DTYPE = BF16

HEAD_DIM = 64
CONV_WIDTH = 31
MOBA_BLOCK = 256
TOP_K = 3
N_BUCKETS = 32
MAX_DISTANCE = 128
RET_CHUNK = 128
ROPE_BASE = 10000.0
EPS = 1e-6

LANES = 128
SUBLANES = 8
PAIR = LANES // HEAD_DIM
MASKED = -1e30
LOG2E = math.log2(math.e)
VMEM_LIMIT = 56 * 1024 * 1024
_CONTRACT_LAST = (((1,), (1,)), ((), ()))
_CONTRACT_FIRST = (((0,), (0,)), ((), ()))


def _silu(x):
    return x * jax.nn.sigmoid(x)


def _params(*sem):
    return pltpu.CompilerParams(dimension_semantics=sem, vmem_limit_bytes=VMEM_LIMIT)


def _inproj_kernel(x_ref, g_ref, w_ref, *out_refs, splits):
    x = x_ref[...]
    h = x * lax.rsqrt(jnp.mean(x * x, axis=-1, keepdims=True) + EPS) * g_ref[...]
    hb = h.astype(BF16)
    for (c0, c1), o_ref in zip(splits, out_refs):
        o_ref[...] = jnp.dot(hb, w_ref[:, c0:c1], preferred_element_type=F32)


def _inproj(x2d, g, w_bf16):
    m, d = x2d.shape
    conv_ch, ret_w, moba_w = d // 4, (d // 256) * HEAD_DIM, (d // 128) * HEAD_DIM
    widths = (3 * conv_ch, 4 * ret_w, moba_w, moba_w, moba_w, moba_w)
    edges = [0]
    for wd in widths:
        edges.append(edges[-1] + wd)
    splits = tuple(zip(edges[:-1], edges[1:]))
    assert edges[-1] == w_bf16.shape[1]
    return pl.pallas_call(
        functools.partial(_inproj_kernel, splits=splits),
        grid=(1,),
        in_specs=[pl.BlockSpec((m, d), lambda i: (0, 0)),
                  pl.BlockSpec((1, d), lambda i: (0, 0)),
                  pl.BlockSpec(w_bf16.shape, lambda i: (0, 0))],
        out_specs=[pl.BlockSpec((m, wd), lambda i: (0, 0)) for wd in widths],
        out_shape=[jax.ShapeDtypeStruct((m, wd), F32) for wd in widths],
        compiler_params=_params("arbitrary"),
        name="inproj_sample",
    )(x2d, g.reshape(1, d), w_bf16)


def _inproj_prompt_kernel(x_ref, g_ref, w_ref, wkv_ref, *refs, splits, n_mean, aliased):
    if aliased:
        refs = refs[2:]
    conv_ref, ret_ref, q_ref, gate_ref, kt_ref, vt_ref, km_ref = refs
    x = x_ref[...]
    h = x * lax.rsqrt(jnp.mean(x * x, axis=-1, keepdims=True) + EPS) * g_ref[...]
    hb = h.astype(BF16)
    for (c0, c1), o_ref in zip(splits, (conv_ref, ret_ref, q_ref, gate_ref)):
        o_ref[...] = jnp.dot(hb, w_ref[:, c0:c1], preferred_element_type=F32)
    mw = kt_ref.shape[2]
    kt = lax.dot_general(wkv_ref[0:mw, :], hb, _CONTRACT_LAST, preferred_element_type=F32)
    kt_ref[0, 0] = kt
    vt_ref[0, 0] = lax.dot_general(wkv_ref[mw:2 * mw, :], hb, _CONTRACT_LAST, preferred_element_type=F32)
    for i in range(n_mean):
        blk = kt[:, i * MOBA_BLOCK:(i + 1) * MOBA_BLOCK]
        km_ref[0, :, i:i + 1] = jnp.sum(blk, axis=-1, keepdims=True) * (1.0 / MOBA_BLOCK)


def _inproj_prompt(x2d, g, w_rows, w_kv_t, kt_buf, vt_buf, *, layer, depth, batch, seq, tm):
    m, d = x2d.shape
    conv_ch, ret_w, mw = d // 4, (d // 256) * HEAD_DIM, (d // 128) * HEAD_DIM
    widths = (3 * conv_ch, 4 * ret_w, mw, mw)
    edges = [0]
    for wd in widths:
        edges.append(edges[-1] + wd)
    splits = tuple(zip(edges[:-1], edges[1:]))
    assert edges[-1] == w_rows.shape[1] and seq % tm == 0 and tm % MOBA_BLOCK == 0
    n_mean = tm // MOBA_BLOCK
    assert n_mean <= 8
    per_seq = seq // tm
    aliased = kt_buf is not None
    row = lambda wd: pl.BlockSpec((tm, wd), lambda i: (i, 0))
    const = lambda a: pl.BlockSpec(a.shape, lambda i: (0,) * a.ndim)
    kv_spec = pl.BlockSpec((1, 1, mw, tm), lambda i: (layer, i // per_seq, 0, i % per_seq))
    kv_shape = jax.ShapeDtypeStruct((depth, batch, mw, seq), F32)
    in_specs = [row(d), pl.BlockSpec((1, d), lambda i: (0, 0)), const(w_rows), const(w_kv_t)]
    args = [x2d, g.reshape(1, d), w_rows, w_kv_t]
    aliases = {}
    if aliased:
        in_specs += [pl.BlockSpec(memory_space=pl.ANY)] * 2
        args += [kt_buf, vt_buf]
        aliases = {4: 4, 5: 5}
    *outs, km_t = pl.pallas_call(
        functools.partial(_inproj_prompt_kernel, splits=splits, n_mean=n_mean, aliased=aliased),
        grid=(m // tm,),
        in_specs=in_specs,
        out_specs=[row(wd) for wd in widths] + [kv_spec, kv_spec,
                                                pl.BlockSpec((1, mw, n_mean), lambda i: (i, 0, 0))],
        out_shape=[jax.ShapeDtypeStruct((m, wd), F32) for wd in widths]
                  + [kv_shape, kv_shape, jax.ShapeDtypeStruct((m // tm, mw, n_mean), F32)],
        input_output_aliases=aliases,
        compiler_params=_params("parallel"),
        name="inproj_prompt",
    )(*args)
    kmean = jnp.transpose(km_t.reshape(batch, per_seq, mw, n_mean), (0, 1, 3, 2)).reshape(batch, -1, mw)
    return (*outs, kmean)


def _outproj_kernel(x_ref, c_ref, r_ref, m_ref, w_ref, fg_ref, y_ref, *, conv_ch, ret_w, final_norm):
    y = x_ref[...]
    y += jnp.dot(c_ref[...].astype(BF16), w_ref[0:conv_ch, :], preferred_element_type=F32)
    y += jnp.dot(r_ref[...].astype(BF16), w_ref[conv_ch:conv_ch + ret_w, :], preferred_element_type=F32)
    y += jnp.dot(m_ref[...].astype(BF16), w_ref[conv_ch + ret_w:, :], preferred_element_type=F32)
    if final_norm:
        y = y * lax.rsqrt(jnp.mean(y * y, axis=-1, keepdims=True) + EPS) * fg_ref[...]
    y_ref[...] = y


def _outproj(x2d, conv_out, ret_out, moba_out, w_bf16, final_g, *, tm, final_norm):
    m, d = x2d.shape
    conv_ch, ret_w, moba_w = conv_out.shape[1], ret_out.shape[1], moba_out.shape[1]
    row = lambda wd: pl.BlockSpec((tm, wd), lambda i: (i, 0))
    return pl.pallas_call(
        functools.partial(_outproj_kernel, conv_ch=conv_ch, ret_w=ret_w, final_norm=final_norm),
        grid=(m // tm,),
        in_specs=[row(d), row(conv_ch), row(ret_w), row(moba_w),
                  pl.BlockSpec(w_bf16.shape, lambda i: (0, 0)),
                  pl.BlockSpec((1, d), lambda i: (0, 0))],
        out_specs=row(d),
        out_shape=jax.ShapeDtypeStruct((m, d), F32),
        compiler_params=_params("parallel"),
        name="outproj",
    )(x2d, conv_out, ret_out, moba_out, w_bf16, final_g.reshape(1, d))


def _layernorm_silu(y, g, b):
    mu = jnp.mean(y, axis=-1, keepdims=True)
    yc = y - mu
    yn = yc * lax.rsqrt(jnp.mean(yc * yc, axis=-1, keepdims=True) + EPS)
    return _silu(yn * g + b)


CONV_PAD = 32
CONV_ROWS = 64


def _conv_kernel(in_ref, prev_ref, w_ref, b_ref, g_ref, beta_ref, o_ref, st_ref, buf, shifted, *, ts, ch):
    t = pl.program_id(1)
    keep = CONV_WIDTH - 1
    lo = CONV_PAD - keep

    @pl.when(t == 0)
    def _():
        buf[lo:CONV_PAD, :] = prev_ref[0]

    @pl.when(t > 0)
    def _():
        buf[lo:CONV_PAD, :] = buf[ts + lo:ts + CONV_PAD, :]

    buf[CONV_PAD:CONV_PAD + ts, :] = in_ref[:, 0:ch] * jax.nn.sigmoid(in_ref[:, ch:2 * ch])
    used = ts + CONV_PAD - SUBLANES
    for k in range(1, SUBLANES):
        shifted[k - 1, 0:used, :] = buf[k:k + used, :]
    for r0 in range(0, ts, CONV_ROWS):
        acc = jnp.zeros((CONV_ROWS, ch), F32)
        for j in range(CONV_WIDTH):
            k = (lo + j) % SUBLANES
            base = r0 + lo + j - k
            rows = buf[base:base + CONV_ROWS, :] if k == 0 else shifted[k - 1, base:base + CONV_ROWS, :]
            acc += rows * w_ref[j:j + 1, :]
        y = _layernorm_silu(acc + b_ref[...], g_ref[...], beta_ref[...])
        gated = y * _silu(in_ref[r0:r0 + CONV_ROWS, 2 * ch:3 * ch])
        o_ref[r0:r0 + CONV_ROWS, :] = gated.astype(o_ref.dtype)

    @pl.when(t == pl.num_programs(1) - 1)
    def _():
        st_ref[0] = buf[ts + lo:ts + CONV_PAD, :]


def _conv_prompt(conv_in, prev, w, b, g, beta, *, batch, seq, ts):
    ch = conv_in.shape[1] // 3
    nts = seq // ts
    assert seq % ts == 0 and ts % CONV_ROWS == 0 and ts >= CONV_PAD
    vec = lambda: pl.BlockSpec((1, ch), lambda bi, t: (0, 0))
    return pl.pallas_call(
        functools.partial(_conv_kernel, ts=ts, ch=ch),
        grid=(batch, nts),
        in_specs=[pl.BlockSpec((ts, 3 * ch), lambda bi, t: (bi * nts + t, 0)),
                  pl.BlockSpec((1, CONV_WIDTH - 1, ch), lambda bi, t: (bi, 0, 0)),
                  pl.BlockSpec((CONV_WIDTH, ch), lambda bi, t: (0, 0)),
                  vec(), vec(), vec()],
        out_specs=[pl.BlockSpec((ts, ch), lambda bi, t: (bi * nts + t, 0)),
                   pl.BlockSpec((1, CONV_WIDTH - 1, ch), lambda bi, t: (bi, 0, 0))],
        out_shape=[jax.ShapeDtypeStruct((batch * seq, ch), MIX_DTYPE),
                   jax.ShapeDtypeStruct((batch, CONV_WIDTH - 1, ch), F32)],
        scratch_shapes=[pltpu.VMEM((ts + CONV_PAD, ch), F32),
                        pltpu.VMEM((SUBLANES - 1, ts + CONV_PAD, ch), F32)],
        compiler_params=_params("parallel", "arbitrary"),
        name="conv_prompt",
    )(conv_in, prev, w, b.reshape(1, ch), g.reshape(1, ch), beta.reshape(1, ch))


SAMPLE_SEQS = 8


def _seqs_per_step(db):
    return math.gcd(db, SAMPLE_SEQS)


def _conv_step_kernel(in_ref, prev_ref, w_ref, b_ref, g_ref, beta_ref, o_ref, st_ref, *, ch):
    keep = CONV_WIDTH - 1
    for s in range(in_ref.shape[0]):
        row = in_ref[s]
        u = row[:, 0:ch] * jax.nn.sigmoid(row[:, ch:2 * ch])
        prev = prev_ref[s]
        acc = jnp.sum(prev * w_ref[0:keep, :], axis=0, keepdims=True) + u * w_ref[keep:keep + 1, :]
        y = _layernorm_silu(acc + b_ref[...], g_ref[...], beta_ref[...])
        o_ref[s] = y * _silu(row[:, 2 * ch:3 * ch])
        st_ref[s, 0:keep - 1, :] = prev_ref[s, 1:keep, :]
        st_ref[s, keep - 1:keep, :] = u


def _conv_sample(conv_in, prev, w, b, g, beta):
    db, ch3 = conv_in.shape
    ch = ch3 // 3
    keep = CONV_WIDTH - 1
    n = _seqs_per_step(db)
    vec = lambda: pl.BlockSpec((1, ch), lambda bi: (0, 0))
    out, st = pl.pallas_call(
        functools.partial(_conv_step_kernel, ch=ch),
        grid=(db // n,),
        in_specs=[pl.BlockSpec((n, 1, ch3), lambda bi: (bi, 0, 0)),
                  pl.BlockSpec((n, keep, ch), lambda bi: (bi, 0, 0)),
                  pl.BlockSpec((CONV_WIDTH, ch), lambda bi: (0, 0)),
                  vec(), vec(), vec()],
        out_specs=[pl.BlockSpec((n, 1, ch), lambda bi: (bi, 0, 0)),
                   pl.BlockSpec((n, keep, ch), lambda bi: (bi, 0, 0))],
        out_shape=[jax.ShapeDtypeStruct((db, 1, ch), F32),
                   jax.ShapeDtypeStruct((db, keep, ch), F32)],
        compiler_params=_params("parallel"),
        name="conv_sample",
    )(conv_in.reshape(db, 1, ch3), prev, w, b.reshape(1, ch), g.reshape(1, ch), beta.reshape(1, ch))
    return out.reshape(db, ch), st


def _rope_rows(x, cos, sin_signed):
    w = x.shape[-1]
    half = HEAD_DIM // 2
    lane = lax.broadcasted_iota(jnp.int32, (1, w), 1)
    partner = jnp.where((lane % HEAD_DIM) < half, pltpu.roll(x, w - half, 1), pltpu.roll(x, half, 1))
    return x * cos + partner * sin_signed


def _half_sums(x, first_half):
    a = jnp.sum(jnp.where(first_half, x, 0.0), axis=-1, keepdims=True)
    b = jnp.sum(jnp.where(first_half, 0.0, x), axis=-1, keepdims=True)
    return jnp.where(first_half, a, b)


RET_SEQS = 2
RET_STEP_CHUNKS = 4


def _ret_kernel(in_ref, r0_ref, cos_ref, sin_ref, dmat_ref, cross_ref, wk_ref, decay_ref,
                o_ref, rout_ref, r_scr, *, rw):
    in_refs = [in_ref.at[b] for b in range(RET_SEQS)]
    o_refs = [o_ref.at[b] for b in range(RET_SEQS)]
    c = pl.program_id(1)
    npair = rw // LANES

    @pl.when(c == 0)
    def _():
        r_scr[...] = r0_ref[...]

    lane = lax.broadcasted_iota(jnp.int32, (1, LANES), 1)
    first = lane < HEAD_DIM
    row = lax.broadcasted_iota(jnp.int32, (LANES, 1), 0)
    blockdiag = (row < HEAD_DIM) == first
    problems = [(b, p) for b in range(RET_SEQS) for p in range(npair)]
    lanes_of = lambda p: slice(p * LANES, (p + 1) * LANES)
    for ci in range(in_ref.shape[1] // RET_CHUNK):
        rows = slice(ci * RET_CHUNK, (ci + 1) * RET_CHUNK)
        cos, sin = cos_ref[rows, :], sin_ref[rows, :]
        q = [_rope_rows(r[rows, 0:rw], cos, sin) for r in in_refs]
        k = [_rope_rows(r[rows, rw:2 * rw], cos, sin) * (HEAD_DIM ** -0.5) for r in in_refs]
        qb = {(b, p): q[b][:, lanes_of(p)] for b, p in problems}
        kp = {(b, p): k[b][:, lanes_of(p)] for b, p in problems}
        vb = {(b, p): in_refs[b][rows, 2 * rw + p * LANES:2 * rw + (p + 1) * LANES].astype(BF16)
              for b, p in problems}
        kb = {pr: kp[pr].astype(BF16) for pr in problems}
        carry = {(b, p): jnp.dot(qb[b, p].astype(BF16), r_scr[b, p].astype(BF16), preferred_element_type=F32)
                 for b, p in problems}
        scores = {}
        for b, p in problems:
            for hh in range(PAIR):
                own = first if hh == 0 else jnp.logical_not(first)
                qh = jnp.where(own, qb[b, p], 0.0).astype(BF16)
                scores[b, p, hh] = lax.dot_general(qh, kb[b, p], _CONTRACT_LAST, preferred_element_type=F32)
        kv = {(b, p): lax.dot_general((kp[b, p] * wk_ref[p]).astype(BF16), vb[b, p], _CONTRACT_FIRST,
                                      preferred_element_type=F32) for b, p in problems}
        inner = {(b, p, hh): jnp.dot((scores[b, p, hh] * dmat_ref[PAIR * p + hh]).astype(BF16), vb[b, p],
                                     preferred_element_type=F32)
                 for b, p in problems for hh in range(PAIR)}
        for b, p in problems:
            o = carry[b, p] * cross_ref[p] + jnp.where(first, inner[b, p, 0], inner[b, p, 1])
            r_scr[b, p] = r_scr[b, p] * decay_ref[p] + jnp.where(blockdiag, kv[b, p], 0.0)
            ms = _half_sums(o * o, first) * (1.0 / HEAD_DIM)
            gate = in_refs[b][rows, 3 * rw + p * LANES:3 * rw + (p + 1) * LANES]
            o_refs[b][rows, lanes_of(p)] = (o * lax.rsqrt(ms + EPS) * _silu(gate)).astype(o_ref.dtype)

    @pl.when(c == pl.num_programs(1) - 1)
    def _():
        rout_ref[...] = r_scr[...]


def _ret_tables(heads):
    c = RET_CHUNK
    lg = jnp.log(1.0 - 2.0 ** (-5.0 - jnp.arange(heads, dtype=F32)))
    t = jnp.arange(c, dtype=F32)
    diff = t[:, None] - t[None, :]
    dmat = jnp.where(diff >= 0, jnp.exp(lg[:, None, None] * jnp.maximum(diff, 0.0)), 0.0)
    cross = jnp.exp(lg[None, :] * (t[:, None] + 1.0))
    wk = jnp.exp(lg[:, None] * (c - 1.0 - t[None, :]))
    chunk_decay = jnp.exp(lg * c)
    per_lane = lambda a: jnp.repeat(a, HEAD_DIM, axis=-1)
    to_pairs = lambda a: jnp.swapaxes(a.reshape(c, heads // PAIR, LANES), 0, 1)
    cross_p = to_pairs(per_lane(cross))
    wk_p = to_pairs(per_lane(wk.T))
    decay_p = to_pairs(per_lane(jnp.broadcast_to(chunk_decay[None, :], (c, heads))))
    return dmat, cross_p, wk_p, decay_p


def _unpair_states(pairs):
    a = pairs[:, :, :HEAD_DIM, :HEAD_DIM]
    b = pairs[:, :, HEAD_DIM:, HEAD_DIM:]
    return jnp.stack([a, b], axis=2).reshape(pairs.shape[0], -1, HEAD_DIM, HEAD_DIM)


def _ret_prompt(ret_in, state0_pairs, cos_t, sin_t, tables, *, batch, seq):
    rw = ret_in.shape[1] // 4
    npair = rw // LANES
    c = RET_CHUNK * RET_STEP_CHUNKS
    nc = seq // c
    assert seq % c == 0 and batch % RET_SEQS == 0
    dmat, cross_p, wk_p, decay_p = tables
    const = lambda a: pl.BlockSpec(a.shape, lambda bi, ci: (0,) * a.ndim)
    state_spec = pl.BlockSpec((RET_SEQS, npair, LANES, LANES), lambda bi, ci: (bi, 0, 0, 0))
    out, rout = pl.pallas_call(
        functools.partial(_ret_kernel, rw=rw),
        grid=(batch // RET_SEQS, nc),
        in_specs=[pl.BlockSpec((RET_SEQS, c, 4 * rw), lambda bi, ci: (bi, ci, 0)),
                  state_spec,
                  pl.BlockSpec((c, rw), lambda bi, ci: (ci, 0)),
                  pl.BlockSpec((c, rw), lambda bi, ci: (ci, 0)),
                  const(dmat), const(cross_p), const(wk_p), const(decay_p)],
        out_specs=[pl.BlockSpec((RET_SEQS, c, rw), lambda bi, ci: (bi, ci, 0)), state_spec],
        out_shape=[jax.ShapeDtypeStruct((batch, seq, rw), MIX_DTYPE),
                   jax.ShapeDtypeStruct((batch, npair, LANES, LANES), F32)],
        scratch_shapes=[pltpu.VMEM((RET_SEQS, npair, LANES, LANES), F32)],
        compiler_params=_params("parallel", "arbitrary"),
        name="ret_prompt",
    )(ret_in.reshape(batch, seq, 4 * rw), state0_pairs, cos_t, sin_t, dmat, cross_p, wk_p, decay_p)
    return out.reshape(batch * seq, rw), rout


def _ret_step_kernel(in_ref, r_ref, cos_ref, sin_ref, decay_ref, o_ref, rout_ref, *, rw):
    heads = rw // HEAD_DIM
    eye = (lax.broadcasted_iota(jnp.int32, (HEAD_DIM, HEAD_DIM), 0)
           == lax.broadcasted_iota(jnp.int32, (HEAD_DIM, HEAD_DIM), 1))
    col = lambda r: jnp.sum(jnp.where(eye, r, 0.0), axis=-1, keepdims=True)
    for s in range(in_ref.shape[0]):
        row = in_ref[s]
        q = _rope_rows(row[:, 0:rw], cos_ref[...], sin_ref[...])
        k = _rope_rows(row[:, rw:2 * rw], cos_ref[...], sin_ref[...]) * (HEAD_DIM ** -0.5)
        v = row[:, 2 * rw:3 * rw]
        gate = row[:, 3 * rw:4 * rw]
        outs = []
        for h in range(heads):
            sl = slice(h * HEAD_DIM, (h + 1) * HEAD_DIM)
            qh, kh, vh = q[:, sl], k[:, sl], v[:, sl]
            decay = decay_ref[h:h + 1, :]
            state = r_ref[s, h]
            o = jnp.sum(qh * kh, axis=-1, keepdims=True) * vh
            o = o + jnp.sum(col(qh) * state, axis=0, keepdims=True) * decay
            rout_ref[s, h] = state * decay + col(kh) * vh
            o = o * lax.rsqrt(jnp.mean(o * o, axis=-1, keepdims=True) + EPS)
            outs.append(o * _silu(gate[:, sl]))
        o_ref[s] = jnp.concatenate(outs, axis=-1)


def _ret_sample(ret_in, state, cos_row, sin_row, decay_rows):
    db, rw4 = ret_in.shape
    rw = rw4 // 4
    heads = rw // HEAD_DIM
    n = _seqs_per_step(db)
    out, rout = pl.pallas_call(
        functools.partial(_ret_step_kernel, rw=rw),
        grid=(db // n,),
        in_specs=[pl.BlockSpec((n, 1, rw4), lambda bi: (bi, 0, 0)),
                  pl.BlockSpec((n, heads, HEAD_DIM, HEAD_DIM), lambda bi: (bi, 0, 0, 0)),
                  pl.BlockSpec((1, rw), lambda bi: (0, 0)),
                  pl.BlockSpec((1, rw), lambda bi: (0, 0)),
                  pl.BlockSpec((heads, HEAD_DIM), lambda bi: (0, 0))],
        out_specs=[pl.BlockSpec((n, 1, rw), lambda bi: (bi, 0, 0)),
                   pl.BlockSpec((n, heads, HEAD_DIM, HEAD_DIM), lambda bi: (bi, 0, 0, 0))],
        out_shape=[jax.ShapeDtypeStruct((db, 1, rw), F32),
                   jax.ShapeDtypeStruct(state.shape, F32)],
        compiler_params=_params("parallel"),
        name="ret_sample",
    )(ret_in.reshape(db, 1, rw4), state, cos_row, sin_row, decay_rows)
    return out.reshape(db, rw), rout


def _rope_tables(pos, heads):
    half = HEAD_DIM // 2
    freqs = ROPE_BASE ** (-jnp.arange(half, dtype=F32) / half)
    ang = pos.astype(F32)[:, None] * freqs[None, :]
    cos, sin = jnp.cos(ang), jnp.sin(ang)
    cos_t = jnp.tile(jnp.concatenate([cos, cos], axis=-1), (1, heads))
    sin_t = jnp.tile(jnp.concatenate([-sin, sin], axis=-1), (1, heads))
    return cos_t, sin_t


def _rel_bucket(d):
    n = jnp.maximum(d, 0)
    max_exact = N_BUCKETS // 2
    large = max_exact + (jnp.log(jnp.maximum(n, 1).astype(F32) / max_exact)
                         / math.log(MAX_DISTANCE / max_exact) * (N_BUCKETS - max_exact)).astype(jnp.int32)
    large = jnp.minimum(large, N_BUCKETS - 1)
    return jnp.where(n < max_exact, n, large)


def _block_rank_penalty(scores, n_past):
    blk = lax.broadcasted_iota(jnp.int32, scores.shape, 0)
    past = blk < n_past
    if n_past <= TOP_K:
        return jnp.where(past, 0.0, MASKED)
    rank = jnp.zeros(scores.shape, F32)
    for m in range(n_past):
        other = scores[m:m + 1, :]
        beats = (other > scores) | ((other == scores) & (m < blk))
        rank += jnp.where(beats, 1.0, 0.0)
    return jnp.where(past & (rank < TOP_K), 0.0, MASKED)


def _moba_kernel(q_ref, kt_ref, vt_ref, km_ref, gate_ref, bias_ref, o_ref,
                 kext_scr, vext_scr, m_scr, acc_scr, *, nb):
    tq = MOBA_BLOCK
    nq = kt_ref.shape[-1] // tq
    lane = lax.broadcasted_iota(jnp.int32, (1, LANES), 1)
    row = lax.broadcasted_iota(jnp.int32, (LANES, 1), 0)
    km = km_ref[0]

    def build_block(c):
        cols = slice(c * tq, (c + 1) * tq)
        kt = kt_ref[0, 0, :, cols]
        vt = vt_ref[0, 0, :, cols]
        for hh in range(PAIR):
            own_rows = (row < HEAD_DIM) if hh == 0 else (row >= HEAD_DIM)
            flag_row = c + (HEAD_DIM if hh == 0 else 0)
            kext_scr[hh, :, cols] = jnp.where(own_rows, kt, jnp.where(row == flag_row, 1.0, 0.0)).astype(BF16)
            vext_scr[hh, :, cols] = jnp.where(own_rows, vt, 1.0).astype(BF16)

    def extended_queries(t):
        q2 = q_ref[t * tq:(t + 1) * tq, :]
        blk = lax.broadcasted_iota(jnp.int32, (nb, tq), 0)
        pad = jnp.zeros((HEAD_DIM, tq), F32)
        tail = jnp.zeros((HEAD_DIM - nb, tq), F32)
        out = []
        for hh in range(PAIR):
            own = (lane < HEAD_DIM) if hh == 0 else (lane >= HEAD_DIM)
            if t > TOP_K:
                scores = lax.dot_general(jnp.where(own, km, 0.0), q2, _CONTRACT_LAST,
                                         precision=lax.Precision.HIGHEST, preferred_element_type=F32)
            else:
                scores = jnp.zeros((nb, tq), F32)
            pen = jnp.where(blk == t, 0.0, _block_rank_penalty(scores, t))
            pen_rows = [pad, pen, tail] if hh == 0 else [pen, tail, pad]
            pen_t = jnp.transpose(jnp.concatenate(pen_rows, axis=0))
            out.append(jnp.where(own, q2 * (HEAD_DIM ** -0.5 * LOG2E), pen_t).astype(BF16))
        return out

    def steps_of(t):
        pairs = [(c, 2) for c in range(t - 1, -1, -2)]
        return pairs + ([(0, 1)] if t % 2 == 0 else [])

    schedule = [(t, c, n) for t in range(nq) for c, n in steps_of(t)]
    q_ext = {}

    def logits(t, c, n):
        if t not in q_ext:
            build_block(t)
            q_ext.clear()
            q_ext[t] = extended_queries(t)
        out = []
        for hh in range(PAIR):
            kx = kext_scr[hh, :, c * tq:(c + n) * tq]
            s = jnp.dot(q_ext[t][hh], kx, preferred_element_type=F32)
            out.append([s[:, i * tq:(i + 1) * tq] + bias_ref[hh, min(t - c - i, 2)] for i in range(n)])
        return out

    def softmax_pv(c, n, s_pair, first):
        for hh in range(PAIR):
            parts = [blk[:, k * LANES:(k + 1) * LANES] for blk in s_pair[hh] for k in range(tq // LANES)]
            top = jnp.max(functools.reduce(jnp.maximum, parts), axis=-1, keepdims=True)
            m_new = jnp.broadcast_to(top, (tq, LANES)) if first else jnp.maximum(m_scr[hh], top)
            probs = jnp.concatenate([jnp.exp2((part - m_new).astype(BF16)) for part in parts], axis=1)
            pv = lax.dot_general(probs, vext_scr[hh, :, c * tq:(c + n) * tq], _CONTRACT_LAST,
                                 preferred_element_type=F32)
            acc_scr[hh] = pv if first else jnp.exp2(m_scr[hh] - m_new) * acc_scr[hh] + pv
            m_scr[hh] = m_new

    s_cur = logits(*schedule[0])
    for idx, (t, c, n) in enumerate(schedule):
        s_next = logits(*schedule[idx + 1]) if idx + 1 < len(schedule) else None
        softmax_pv(c, n, s_cur, first=(c + n == t + 1))
        if c == 0:
            numer = jnp.where(lane < HEAD_DIM, acc_scr[0], acc_scr[1])
            denom = jnp.where(lane < HEAD_DIM, pltpu.roll(acc_scr[0], HEAD_DIM, 1),
                              pltpu.roll(acc_scr[1], HEAD_DIM, 1))
            rows = slice(t * tq, (t + 1) * tq)
            o_ref[rows, :] = (numer / denom * _silu(gate_ref[rows, :])).astype(o_ref.dtype)
        s_cur = s_next


def _toeplitz(vec, rows, cols):
    h, n = vec.shape
    assert cols <= n - 1 and rows <= n
    flat = jnp.tile(vec, (1, rows))[:, :rows * (n - 1)]
    return flat.reshape(h, rows, n - 1)[:, :, :cols]


def _moba_bias_tiles(rel_bias):
    blk = MOBA_BLOCK
    bias_t = rel_bias.astype(F32).T
    tiles = []
    n = 2 * blk - 1
    k = jnp.arange(n)
    col_minus_row = jnp.where(k < blk, k, k - n)
    for dist in range(3):
        d = dist * blk - col_minus_row
        g = bias_t[:, _rel_bucket(d)]
        if dist == 0:
            g = jnp.where((d >= 0)[None], g, -jnp.inf)
        tiles.append(_toeplitz(g, blk, blk))
    return jnp.stack(tiles, axis=1) * LOG2E


def _moba_prompt(mq, kt_buf, vt_buf, kmean, mgate, bias_tiles, *, layer, batch, seq):
    mw = mq.shape[1]
    npair = mw // LANES
    nq = seq // MOBA_BLOCK
    nb = kmean.shape[1]
    assert seq % MOBA_BLOCK == 0 and nb % 8 == 0 and 2 <= nq <= nb <= HEAD_DIM
    assert MOBA_BLOCK + 1 >= MAX_DISTANCE
    rows = lambda: pl.BlockSpec((seq, LANES), lambda b, p: (b, p))
    cols = lambda: pl.BlockSpec((1, 1, LANES, seq), lambda b, p: (layer, b, p, 0))
    return pl.pallas_call(
        functools.partial(_moba_kernel, nb=nb),
        grid=(batch, npair),
        in_specs=[rows(), cols(), cols(),
                  pl.BlockSpec((1, nb, LANES), lambda b, p: (b, 0, p)),
                  rows(),
                  pl.BlockSpec((PAIR, 3, MOBA_BLOCK, MOBA_BLOCK), lambda b, p: (p, 0, 0, 0))],
        out_specs=rows(),
        out_shape=jax.ShapeDtypeStruct((batch * seq, mw), MIX_DTYPE),
        scratch_shapes=[pltpu.VMEM((PAIR, LANES, seq), BF16)] * 2
                       + [pltpu.VMEM((PAIR, MOBA_BLOCK, LANES), F32)] * 2,
        compiler_params=_params("parallel", "parallel"),
        name="moba_prompt",
    )(mq, kt_buf, vt_buf, kmean, mgate, bias_tiles)


PAGES_PER_STEP = 32


def _page_score_kernel(pt_ref, q_ref, *refs, pages_per_block, heads):
    n_step_pages = len(refs) - 1
    page_refs, s_ref = refs[:n_step_pages], refs[n_step_pages]
    g = pl.program_id(1)

    @pl.when(g == 0)
    def _():
        s_ref[...] = jnp.zeros(s_ref.shape, F32)

    lane = lax.broadcasted_iota(jnp.int32, (1, LANES), 1)
    head_row = lax.broadcasted_iota(jnp.int32, (heads, 1), 0)
    blocks_per_step = n_step_pages // pages_per_block
    for bi in range(blocks_per_step):
        blk = g * blocks_per_step + bi
        pages = page_refs[bi * pages_per_block:(bi + 1) * pages_per_block]
        upd = jnp.zeros((heads, LANES), F32)
        for h in range(heads):
            rows = slice(h * HEAD_DIM, (h + 1) * HEAD_DIM)
            ksum = pages[0][0, rows, :]
            for page in pages[1:]:
                ksum = ksum + page[0, rows, :]
            part = jnp.sum(ksum * q_ref[0, rows, :], axis=0, keepdims=True)
            score = jnp.sum(part, axis=-1, keepdims=True) * (1.0 / MOBA_BLOCK)
            upd = jnp.where((head_row == h) & (lane == blk), score, upd)
        s_ref[0] += upd


def _page_scores(pt_flat, q_rep, cache_pages, *, n_pages, page_size, page0):
    db, width, _ = q_rep.shape
    heads = width // HEAD_DIM
    pages_per_block = MOBA_BLOCK // page_size
    step_pages = math.gcd(n_pages, PAGES_PER_STEP)
    assert MOBA_BLOCK % page_size == 0 and step_pages % pages_per_block == 0
    assert n_pages // pages_per_block <= LANES and page_size == LANES
    steps = n_pages // step_pages

    def page_spec(i):
        return pl.BlockSpec((1, width, page_size),
                            lambda b, g, pt: (page0 + pt[b * n_pages + g * step_pages + i], 0, 0))

    grid_spec = pltpu.PrefetchScalarGridSpec(
        num_scalar_prefetch=1,
        grid=(db, steps),
        in_specs=[pl.BlockSpec((1, width, LANES), lambda b, g, pt: (b, 0, 0))]
                 + [page_spec(i) for i in range(step_pages)],
        out_specs=pl.BlockSpec((1, heads, LANES), lambda b, g, pt: (b, 0, 0)),
    )
    return pl.pallas_call(
        functools.partial(_page_score_kernel, pages_per_block=pages_per_block, heads=heads),
        grid_spec=grid_spec,
        out_shape=jax.ShapeDtypeStruct((db, heads, LANES), F32),
        compiler_params=_params("parallel", "arbitrary"),
        name="page_scores",
    )(pt_flat, q_rep, *([cache_pages] * step_pages))


def _topk_kernel(s_ref, sel_ref, *, n_blocks):
    s = s_ref[...]
    lane = lax.broadcasted_iota(jnp.int32, s.shape, 1).astype(F32)
    s = jnp.where(lane < n_blocks, s, -jnp.inf)
    sel = jnp.zeros(s.shape, F32)
    for j in range(TOP_K):
        best = jnp.max(s, axis=-1, keepdims=True)
        idx = jnp.min(jnp.where(s == best, lane, float(LANES)), axis=-1, keepdims=True)
        sel = jnp.where(lane == j, idx, sel)
        s = jnp.where(lane == idx, -jnp.inf, s)
    sel_ref[...] = sel.astype(jnp.int32)


def _topk_blocks(scores, *, n_blocks):
    db, heads, _ = scores.shape
    assert n_blocks >= TOP_K
    spec = pl.BlockSpec((db * heads, LANES), lambda i: (0, 0))
    return pl.pallas_call(
        functools.partial(_topk_kernel, n_blocks=n_blocks),
        grid=(1,),
        in_specs=[spec],
        out_specs=spec,
        out_shape=jax.ShapeDtypeStruct((db * heads, LANES), jnp.int32),
        compiler_params=_params("arbitrary"),
        name="topk_blocks",
    )(scores.reshape(db * heads, LANES)).reshape(db, heads, LANES)


ATTEND_HEADS = 4


def _page_attend_kernel(pt_ref, seqp_ref, q_ref, kn_ref, vn_ref, gate_ref, bown_ref, bias_ref, *refs,
                        heads, n_sel_pages):
    n_tiles = ATTEND_HEADS * n_sel_pages
    k_refs, v_refs, o_ref = refs[:n_tiles], refs[n_tiles:2 * n_tiles], refs[2 * n_tiles]
    b = pl.program_id(0)
    h0 = pl.program_id(1) * ATTEND_HEADS
    diag = (lax.broadcasted_iota(jnp.int32, (HEAD_DIM, LANES), 0)
            == lax.broadcasted_iota(jnp.int32, (HEAD_DIM, LANES), 1))
    for i in range(ATTEND_HEADS):
        h = h0 + i
        dims = slice(i * HEAD_DIM, (i + 1) * HEAD_DIM)
        q = q_ref[0, dims, :] * (HEAD_DIM ** -0.5)
        logits = []
        for j in range(n_sel_pages):
            seq_page = seqp_ref[(b * heads + h) * n_sel_pages + j]
            bias = bias_ref[h, pl.ds(seq_page, 1), :]
            logits.append(jnp.sum(k_refs[i * n_sel_pages + j][0] * q, axis=0, keepdims=True) + bias)
        s_own = jnp.sum(kn_ref[0, dims, :] * q, axis=0, keepdims=True) + bown_ref[pl.ds(h, 1), :]
        top = functools.reduce(jnp.maximum, logits)
        m = jnp.maximum(jnp.max(top, axis=-1, keepdims=True), s_own)
        p_own = jnp.exp(s_own - m)
        psum = jnp.zeros_like(m)
        acc = jnp.zeros((HEAD_DIM, LANES), F32)
        for j in range(n_sel_pages):
            p = jnp.exp(logits[j] - m)
            psum += p
            acc += v_refs[i * n_sel_pages + j][0] * p
        denom = jnp.sum(psum, axis=-1, keepdims=True) + p_own
        out_rep = (jnp.sum(acc, axis=-1, keepdims=True) + p_own * vn_ref[0, dims, :]) / denom
        out_row = jnp.sum(jnp.where(diag, out_rep, 0.0), axis=0, keepdims=True)[:, :HEAD_DIM]
        o_ref[0, pl.ds(h, 1), :] = out_row * _silu(gate_ref[0, pl.ds(h, 1), :])


def _page_attend(pt_flat, seq_pages, q_rep, kn_rep, vn_rep, gate8, bown, bias_rows, k_tiles, v_tiles,
                 *, n_sel_pages, n_pages, page0):
    db, heads, _ = gate8.shape
    page_size = k_tiles.shape[-1]
    assert heads % ATTEND_HEADS == 0
    head_rep = lambda: pl.BlockSpec((1, ATTEND_HEADS * HEAD_DIM, LANES), lambda b, g, pt, sp: (b, g, 0))
    whole = lambda a: pl.BlockSpec(a.shape, lambda b, g, pt, sp: (0,) * a.ndim)

    def tile_spec(i, j):
        def index(b, g, pt, sp):
            h = g * ATTEND_HEADS + i
            page = page0 + pt[b * n_pages + sp[(b * heads + h) * n_sel_pages + j]]
            return (page * heads + h, 0, 0)
        return pl.BlockSpec((1, HEAD_DIM, page_size), index)

    tile_specs = [tile_spec(i, j) for i in range(ATTEND_HEADS) for j in range(n_sel_pages)]
    grid_spec = pltpu.PrefetchScalarGridSpec(
        num_scalar_prefetch=2,
        grid=(db, heads // ATTEND_HEADS),
        in_specs=[head_rep(), head_rep(), head_rep(),
                  pl.BlockSpec((1, heads, HEAD_DIM), lambda b, g, pt, sp: (b, 0, 0)),
                  whole(bown), whole(bias_rows)] + tile_specs * 2,
        out_specs=pl.BlockSpec((1, heads, HEAD_DIM), lambda b, g, pt, sp: (b, 0, 0)),
    )
    return pl.pallas_call(
        functools.partial(_page_attend_kernel, heads=heads, n_sel_pages=n_sel_pages),
        grid_spec=grid_spec,
        out_shape=jax.ShapeDtypeStruct((db, heads, HEAD_DIM), F32),
        compiler_params=_params("parallel", "arbitrary"),
        name="page_attend",
    )(pt_flat, seq_pages, q_rep, kn_rep, vn_rep, gate8, bown, bias_rows,
      *([k_tiles] * len(tile_specs)), *([v_tiles] * len(tile_specs)))


def _sample_bias_rows(rel_bias, *, past_len, page_size):
    kpos = jnp.arange(past_len)
    onehot = (_rel_bucket(past_len - kpos)[:, None] == jnp.arange(N_BUCKETS)[None, :]).astype(F32)
    bias = jnp.dot(onehot, rel_bias.astype(F32), precision=lax.Precision.HIGHEST)
    return bias.T.reshape(rel_bias.shape[1], past_len // page_size, page_size)


def _lane_replicated(x):
    return jnp.broadcast_to(x[:, :, None], x.shape + (LANES,))


def _moba_sample(mq, mk, mv, mgate, k_cache_t, v_cache_t, page0, page_table, rel_bias, bias_rows, *, page_size):
    db, mw = mq.shape
    heads = mw // HEAD_DIM
    n_pages = page_table.shape[1]
    past_len = n_pages * page_size
    pages_per_block = MOBA_BLOCK // page_size
    n_blocks = past_len // MOBA_BLOCK
    assert past_len % MOBA_BLOCK == 0
    n_rows = k_cache_t.shape[0]
    pt_flat = page_table.reshape(-1)
    q_rep = _lane_replicated(mq)
    scores = _page_scores(pt_flat, q_rep, k_cache_t.reshape(n_rows, mw, page_size), n_pages=n_pages,
                          page_size=page_size, page0=page0)
    sel = _topk_blocks(scores, n_blocks=n_blocks)[:, :, :TOP_K]
    seq_pages = (sel[..., None] * pages_per_block + jnp.arange(pages_per_block, dtype=jnp.int32)).reshape(-1)
    bown = jnp.broadcast_to(rel_bias.astype(F32)[0][:, None], (heads, LANES))
    out = _page_attend(pt_flat, seq_pages, q_rep, _lane_replicated(mk), _lane_replicated(mv),
                       mgate.reshape(db, heads, HEAD_DIM), bown, bias_rows,
                       k_cache_t.reshape(n_rows * heads, HEAD_DIM, page_size),
                       v_cache_t.reshape(n_rows * heads, HEAD_DIM, page_size),
                       n_sel_pages=TOP_K * pages_per_block, n_pages=n_pages, page0=page0)
    return out.reshape(db, mw)


PROMPT_TM = 512
CONV_TS = 512


def kernel(x_prompt, x_sample, cache_k, cache_v, page_table, state_ret, state_conv, norm_g, w_in, conv_w,
           conv_b, conv_ln_g, conv_ln_b, w_out, rel_bias, final_g):
    batch, seq, d = x_prompt.shape
    db, dec_seq, _ = x_sample.shape
    assert dec_seq == 1, "the sample path handles one new token per sequence"
    depth = w_in.shape[0]
    ret_heads = d // 256
    moba_heads = d // 128
    conv_ch = d // 4
    n_pages = page_table.shape[1]
    page_size = cache_k.shape[2]
    past_len = n_pages * page_size
    n_phys = cache_k.shape[1]
    tokens_minor = lambda c: jnp.transpose(c, (0, 1, 3, 4, 2)).reshape(depth * n_phys, moba_heads, HEAD_DIM,
                                                                         page_size)
    k_cache_t, v_cache_t = tokens_minor(cache_k), tokens_minor(cache_v)

    mw = moba_heads * HEAD_DIM
    k0 = 3 * conv_ch + 4 * ret_heads * HEAD_DIM + mw
    w_rows = jnp.concatenate([w_in[:, :, :k0], w_in[:, :, k0 + 2 * mw:]], axis=-1).astype(BF16)
    w_kv_t = jnp.swapaxes(w_in[:, :, k0:k0 + 2 * mw], 1, 2).astype(BF16)
    w_in_b = w_in.astype(BF16)
    w_out_b = w_out.astype(BF16)
    cos_p, sin_p = _rope_tables(jnp.arange(seq, dtype=jnp.int32), ret_heads)
    cos_s, sin_s = _rope_tables(past_len + jnp.arange(1, dtype=jnp.int32), ret_heads)
    ret_tables = _ret_tables(ret_heads)
    lg = jnp.log(1.0 - 2.0 ** (-5.0 - jnp.arange(ret_heads, dtype=F32)))
    decay_rows = jnp.broadcast_to(jnp.exp(lg)[:, None], (ret_heads, HEAD_DIM))
    bias_tiles = _moba_bias_tiles(rel_bias)
    bias_rows = _sample_bias_rows(rel_bias, past_len=past_len, page_size=page_size)
    conv0 = jnp.zeros((batch, CONV_WIDTH - 1, conv_ch), F32)
    ret0 = jnp.zeros((batch, ret_heads // PAIR, LANES, LANES), F32)
    nb = seq // MOBA_BLOCK
    nb_pad = -(-nb // 8) * 8

    hp = x_prompt.reshape(batch * seq, d)
    hs = x_sample.reshape(db, d)
    outs = {name: [] for name in ("ks", "vs", "rp", "rs", "cp", "cs")}
    kt_buf = vt_buf = None
    for l in range(depth):
        last = l == depth - 1
        conv_in, ret_in, mq, mgate, kt_buf, vt_buf, kmean = _inproj_prompt(
            hp, norm_g[l], w_rows[l], w_kv_t[l], kt_buf, vt_buf, layer=l, depth=depth, batch=batch,
            seq=seq, tm=PROMPT_TM)
        conv_out, conv_state = _conv_prompt(conv_in, conv0, conv_w[l], conv_b[l], conv_ln_g[l], conv_ln_b[l],
                                            batch=batch, seq=seq, ts=CONV_TS)
        ret_out, ret_state = _ret_prompt(ret_in, ret0, cos_p, sin_p, ret_tables, batch=batch, seq=seq)
        kmean = jnp.pad(kmean, ((0, 0), (0, nb_pad - nb), (0, 0)))
        moba_out = _moba_prompt(mq, kt_buf, vt_buf, kmean, mgate, bias_tiles, layer=l, batch=batch, seq=seq)
        hp = _outproj(hp, conv_out, ret_out, moba_out, w_out_b[l], final_g, tm=PROMPT_TM, final_norm=last)
        outs["rp"].append(_unpair_states(ret_state))
        outs["cp"].append(conv_state)
        conv_in, ret_in, mq, mk, mv, mgate = _inproj(hs, norm_g[l], w_in_b[l])
        conv_out, conv_state = _conv_sample(conv_in, state_conv[l], conv_w[l], conv_b[l], conv_ln_g[l],
                                            conv_ln_b[l])
        ret_out, ret_state = _ret_sample(ret_in, state_ret[l], cos_s, sin_s, decay_rows)
        moba_out = _moba_sample(mq, mk, mv, mgate, k_cache_t, v_cache_t, l * n_phys, page_table, rel_bias,
                                bias_rows, page_size=page_size)
        hs = _outproj(hs, conv_out, ret_out, moba_out, w_out_b[l], final_g, tm=db, final_norm=last)
        outs["ks"].append(mk.reshape(db, 1, moba_heads, HEAD_DIM))
        outs["vs"].append(mv.reshape(db, 1, moba_heads, HEAD_DIM))
        outs["rs"].append(ret_state)
        outs["cs"].append(conv_state)

    st = {name: jnp.stack(vals) for name, vals in outs.items()}
    rows_major = lambda t: jnp.transpose(t.reshape(depth, batch, moba_heads, HEAD_DIM, seq), (0, 1, 4, 2, 3))
    return (hp.reshape(batch, seq, d), hs.reshape(db, 1, d), rows_major(kt_buf), rows_major(vt_buf),
            st["ks"], st["vs"], st["rp"], st["rs"], st["cp"], st["cs"])
```

```python
import functools
import math

import jax
import jax.numpy as jnp
from jax import lax
from jax.experimental import pallas as pl
from jax.experimental.pallas import tpu as pltpu

F32 = jnp.float32
BF16 = jnp.bfloat16
MIX_DTYPE = BF16

HEAD_DIM = 64
CONV_WIDTH = 31
MOBA_BLOCK = 256
TOP_K = 3
N_BUCKETS = 32
MAX_DISTANCE = 128
RET_CHUNK = 128
ROPE_BASE = 10000.0
EPS = 1e-6

LANES = 128
SUBLANES = 8
PAIR = LANES // HEAD_DIM
MASKED = -1e30
LOG2E = math.log2(math.e)
VMEM_LIMIT = 56 * 1024 * 1024
_CONTRACT_LAST = (((1,), (1,)), ((), ()))
_CONTRACT_FIRST = (((0,), (0,)), ((), ()))


def _silu(x):
    return x * jax.nn.sigmoid(x)


def _params(*sem):
    return pltpu.CompilerParams(dimension_semantics=sem, vmem_limit_bytes=VMEM_LIMIT)


def _inproj_kernel(x_ref, g_ref, w_ref, *out_refs, splits):
    x = x_ref[...]
    h = x * lax.rsqrt(jnp.mean(x * x, axis=-1, keepdims=True) + EPS) * g_ref[...]
    hb = h.astype(BF16)
    for (c0, c1), o_ref in zip(splits, out_refs):
        o_ref[...] = jnp.dot(hb, w_ref[:, c0:c1], preferred_element_type=F32)


def _inproj(x2d, g, w_bf16):
    m, d = x2d.shape
    conv_ch, ret_w, moba_w = d // 4, (d // 256) * HEAD_DIM, (d // 128) * HEAD_DIM
    widths = (3 * conv_ch, 4 * ret_w, moba_w, moba_w, moba_w, moba_w)
    edges = [0]
    for wd in widths:
        edges.append(edges[-1] + wd)
    splits = tuple(zip(edges[:-1], edges[1:]))
    assert edges[-1] == w_bf16.shape[1]
    return pl.pallas_call(
        functools.partial(_inproj_kernel, splits=splits),
        grid=(1,),
        in_specs=[pl.BlockSpec((m, d), lambda i: (0, 0)),
                  pl.BlockSpec((1, d), lambda i: (0, 0)),
                  pl.BlockSpec(w_bf16.shape, lambda i: (0, 0))],
        out_specs=[pl.BlockSpec((m, wd), lambda i: (0, 0)) for wd in widths],
        out_shape=[jax.ShapeDtypeStruct((m, wd), F32) for wd in widths],
        compiler_params=_params("arbitrary"),
        name="inproj_sample",
    )(x2d, g.reshape(1, d), w_bf16)


def _inproj_prompt_kernel(x_ref, g_ref, w_ref, wkv_ref, *refs, splits, n_mean, aliased):
    if aliased:
        refs = refs[2:]
    conv_ref, ret_ref, q_ref, gate_ref, kt_ref, vt_ref, km_ref = refs
    x = x_ref[...]
    h = x * lax.rsqrt(jnp.mean(x * x, axis=-1, keepdims=True) + EPS) * g_ref[...]
    hb = h.astype(BF16)
    for (c0, c1), o_ref in zip(splits, (conv_ref, ret_ref, q_ref, gate_ref)):
        o_ref[...] = jnp.dot(hb, w_ref[:, c0:c1], preferred_element_type=F32)
    mw = kt_ref.shape[2]
    kt = lax.dot_general(wkv_ref[0:mw, :], hb, _CONTRACT_LAST, preferred_element_type=F32)
    kt_ref[0, 0] = kt
    vt_ref[0, 0] = lax.dot_general(wkv_ref[mw:2 * mw, :], hb, _CONTRACT_LAST, preferred_element_type=F32)
    for i in range(n_mean):
        blk = kt[:, i * MOBA_BLOCK:(i + 1) * MOBA_BLOCK]
        km_ref[0, :, i:i + 1] = jnp.sum(blk, axis=-1, keepdims=True) * (1.0 / MOBA_BLOCK)


def _inproj_prompt(x2d, g, w_rows, w_kv_t, kt_buf, vt_buf, *, layer, depth, batch, seq, tm):
    m, d = x2d.shape
    conv_ch, ret_w, mw = d // 4, (d // 256) * HEAD_DIM, (d // 128) * HEAD_DIM
    widths = (3 * conv_ch, 4 * ret_w, mw, mw)
    edges = [0]
    for wd in widths:
        edges.append(edges[-1] + wd)
    splits = tuple(zip(edges[:-1], edges[1:]))
    assert edges[-1] == w_rows.shape[1] and seq % tm == 0 and tm % MOBA_BLOCK == 0
    n_mean = tm // MOBA_BLOCK
    assert n_mean <= 8
    per_seq = seq // tm
    aliased = kt_buf is not None
    row = lambda wd: pl.BlockSpec((tm, wd), lambda i: (i, 0))
    const = lambda a: pl.BlockSpec(a.shape, lambda i: (0,) * a.ndim)
    kv_spec = pl.BlockSpec((1, 1, mw, tm), lambda i: (layer, i // per_seq, 0, i % per_seq))
    kv_shape = jax.ShapeDtypeStruct((depth, batch, mw, seq), F32)
    in_specs = [row(d), pl.BlockSpec((1, d), lambda i: (0, 0)), const(w_rows), const(w_kv_t)]
    args = [x2d, g.reshape(1, d), w_rows, w_kv_t]
    aliases = {}
    if aliased:
        in_specs += [pl.BlockSpec(memory_space=pl.ANY)] * 2
        args += [kt_buf, vt_buf]
        aliases = {4: 4, 5: 5}
    *outs, km_t = pl.pallas_call(
        functools.partial(_inproj_prompt_kernel, splits=splits, n_mean=n_mean, aliased=aliased),
        grid=(m // tm,),
        in_specs=in_specs,
        out_specs=[row(wd) for wd in widths] + [kv_spec, kv_spec,
                                                pl.BlockSpec((1, mw, n_mean), lambda i: (i, 0, 0))],
        out_shape=[jax.ShapeDtypeStruct((m, wd), F32) for wd in widths]
                  + [kv_shape, kv_shape, jax.ShapeDtypeStruct((m // tm, mw, n_mean), F32)],
        input_output_aliases=aliases,
        compiler_params=_params("parallel"),
        name="inproj_prompt",
    )(*args)
    kmean = jnp.transpose(km_t.reshape(batch, per_seq, mw, n_mean), (0, 1, 3, 2)).reshape(batch, -1, mw)
    return (*outs, kmean)


def _outproj_kernel(x_ref, c_ref, r_ref, m_ref, w_ref, fg_ref, y_ref, *, conv_ch, ret_w, final_norm):
    y = x_ref[...]
    y += jnp.dot(c_ref[...].astype(BF16), w_ref[0:conv_ch, :], preferred_element_type=F32)
    y += jnp.dot(r_ref[...].astype(BF16), w_ref[conv_ch:conv_ch + ret_w, :], preferred_element_type=F32)
    y += jnp.dot(m_ref[...].astype(BF16), w_ref[conv_ch + ret_w:, :], preferred_element_type=F32)
    if final_norm:
        y = y * lax.rsqrt(jnp.mean(y * y, axis=-1, keepdims=True) + EPS) * fg_ref[...]
    y_ref[...] = y


def _outproj(x2d, conv_out, ret_out, moba_out, w_bf16, final_g, *, tm, final_norm):
    m, d = x2d.shape
    conv_ch, ret_w, moba_w = conv_out.shape[1], ret_out.shape[1], moba_out.shape[1]
    row = lambda wd: pl.BlockSpec((tm, wd), lambda i: (i, 0))
    return pl.pallas_call(
        functools.partial(_outproj_kernel, conv_ch=conv_ch, ret_w=ret_w, final_norm=final_norm),
        grid=(m // tm,),
        in_specs=[row(d), row(conv_ch), row(ret_w), row(moba_w),
                  pl.BlockSpec(w_bf16.shape, lambda i: (0, 0)),
                  pl.BlockSpec((1, d), lambda i: (0, 0))],
        out_specs=row(d),
        out_shape=jax.ShapeDtypeStruct((m, d), F32),
        compiler_params=_params("parallel"),
        name="outproj",
    )(x2d, conv_out, ret_out, moba_out, w_bf16, final_g.reshape(1, d))


def _layernorm_silu(y, g, b):
    mu = jnp.mean(y, axis=-1, keepdims=True)
    yc = y - mu
    yn = yc * lax.rsqrt(jnp.mean(yc * yc, axis=-1, keepdims=True) + EPS)
    return _silu(yn * g + b)


CONV_PAD = 32
CONV_ROWS = 64


def _conv_kernel(in_ref, prev_ref, w_ref, b_ref, g_ref, beta_ref, o_ref, st_ref, buf, shifted, *, ts, ch):
    t = pl.program_id(1)
    keep = CONV_WIDTH - 1
    lo = CONV_PAD - keep

    @pl.when(t == 0)
    def _():
        buf[lo:CONV_PAD, :] = prev_ref[0]

    @pl.when(t > 0)
    def _():
        buf[lo:CONV_PAD, :] = buf[ts + lo:ts + CONV_PAD, :]

    buf[CONV_PAD:CONV_PAD + ts, :] = in_ref[:, 0:ch] * jax.nn.sigmoid(in_ref[:, ch:2 * ch])
    used = ts + CONV_PAD - SUBLANES
    for k in range(1, SUBLANES):
        shifted[k - 1, 0:used, :] = buf[k:k + used, :]
    for r0 in range(0, ts, CONV_ROWS):
        acc = jnp.zeros((CONV_ROWS, ch), F32)
        for j in range(CONV_WIDTH):
            k = (lo + j) % SUBLANES
            base = r0 + lo + j - k
            rows = buf[base:base + CONV_ROWS, :] if k == 0 else shifted[k - 1, base:base + CONV_ROWS, :]
            acc += rows * w_ref[j:j + 1, :]
        y = _layernorm_silu(acc + b_ref[...], g_ref[...], beta_ref[...])
        gated = y * _silu(in_ref[r0:r0 + CONV_ROWS, 2 * ch:3 * ch])
        o_ref[r0:r0 + CONV_ROWS, :] = gated.astype(o_ref.dtype)

    @pl.when(t == pl.num_programs(1) - 1)
    def _():
        st_ref[0] = buf[ts + lo:ts + CONV_PAD, :]


def _conv_prompt(conv_in, prev, w, b, g, beta, *, batch, seq, ts):
    ch = conv_in.shape[1] // 3
    nts = seq // ts
    assert seq % ts == 0 and ts % CONV_ROWS == 0 and ts >= CONV_PAD
    vec = lambda: pl.BlockSpec((1, ch), lambda bi, t: (0, 0))
    return pl.pallas_call(
        functools.partial(_conv_kernel, ts=ts, ch=ch),
        grid=(batch, nts),
        in_specs=[pl.BlockSpec((ts, 3 * ch), lambda bi, t: (bi * nts + t, 0)),
                  pl.BlockSpec((1, CONV_WIDTH - 1, ch), lambda bi, t: (bi, 0, 0)),
                  pl.BlockSpec((CONV_WIDTH, ch), lambda bi, t: (0, 0)),
                  vec(), vec(), vec()],
        out_specs=[pl.BlockSpec((ts, ch), lambda bi, t: (bi * nts + t, 0)),
                   pl.BlockSpec((1, CONV_WIDTH - 1, ch), lambda bi, t: (bi, 0, 0))],
        out_shape=[jax.ShapeDtypeStruct((batch * seq, ch), MIX_DTYPE),
                   jax.ShapeDtypeStruct((batch, CONV_WIDTH - 1, ch), F32)],
        scratch_shapes=[pltpu.VMEM((ts + CONV_PAD, ch), F32),
                        pltpu.VMEM((SUBLANES - 1, ts + CONV_PAD, ch), F32)],
        compiler_params=_params("parallel", "arbitrary"),
        name="conv_prompt",
    )(conv_in, prev, w, b.reshape(1, ch), g.reshape(1, ch), beta.reshape(1, ch))


SAMPLE_SEQS = 8


def _seqs_per_step(db):
    return math.gcd(db, SAMPLE_SEQS)


def _conv_step_kernel(in_ref, prev_ref, w_ref, b_ref, g_ref, beta_ref, o_ref, st_ref, *, ch):
    keep = CONV_WIDTH - 1
    for s in range(in_ref.shape[0]):
        row = in_ref[s]
        u = row[:, 0:ch] * jax.nn.sigmoid(row[:, ch:2 * ch])
        prev = prev_ref[s]
        acc = jnp.sum(prev * w_ref[0:keep, :], axis=0, keepdims=True) + u * w_ref[keep:keep + 1, :]
        y = _layernorm_silu(acc + b_ref[...], g_ref[...], beta_ref[...])
        o_ref[s] = y * _silu(row[:, 2 * ch:3 * ch])
        st_ref[s, 0:keep - 1, :] = prev_ref[s, 1:keep, :]
        st_ref[s, keep - 1:keep, :] = u


def _conv_sample(conv_in, prev, w, b, g, beta):
    db, ch3 = conv_in.shape
    ch = ch3 // 3
    keep = CONV_WIDTH - 1
    n = _seqs_per_step(db)
    vec = lambda: pl.BlockSpec((1, ch), lambda bi: (0, 0))
    out, st = pl.pallas_call(
        functools.partial(_conv_step_kernel, ch=ch),
        grid=(db // n,),
        in_specs=[pl.BlockSpec((n, 1, ch3), lambda bi: (bi, 0, 0)),
                  pl.BlockSpec((n, keep, ch), lambda bi: (bi, 0, 0)),
                  pl.BlockSpec((CONV_WIDTH, ch), lambda bi: (0, 0)),
                  vec(), vec(), vec()],
        out_specs=[pl.BlockSpec((n, 1, ch), lambda bi: (bi, 0, 0)),
                   pl.BlockSpec((n, keep, ch), lambda bi: (bi, 0, 0))],
        out_shape=[jax.ShapeDtypeStruct((db, 1, ch), F32),
                   jax.ShapeDtypeStruct((db, keep, ch), F32)],
        compiler_params=_params("parallel"),
        name="conv_sample",
    )(conv_in.reshape(db, 1, ch3), prev, w, b.reshape(1, ch), g.reshape(1, ch), beta.reshape(1, ch))
    return out.reshape(db, ch), st


def _rope_rows(x, cos, sin_signed):
    w = x.shape[-1]
    half = HEAD_DIM // 2
    lane = lax.broadcasted_iota(jnp.int32, (1, w), 1)
    partner = jnp.where((lane % HEAD_DIM) < half, pltpu.roll(x, w - half, 1), pltpu.roll(x, half, 1))
    return x * cos + partner * sin_signed


def _half_sums(x, first_half):
    a = jnp.sum(jnp.where(first_half, x, 0.0), axis=-1, keepdims=True)
    b = jnp.sum(jnp.where(first_half, 0.0, x), axis=-1, keepdims=True)
    return jnp.where(first_half, a, b)


RET_SEQS = 2
RET_STEP_CHUNKS = 4


def _ret_kernel(in_ref, r0_ref, cos_ref, sin_ref, dmat_ref, cross_ref, wk_ref, decay_ref,
                o_ref, rout_ref, r_scr, *, rw):
    in_refs = [in_ref.at[b] for b in range(RET_SEQS)]
    o_refs = [o_ref.at[b] for b in range(RET_SEQS)]
    c = pl.program_id(1)
    npair = rw // LANES

    @pl.when(c == 0)
    def _():
        r_scr[...] = r0_ref[...]

    lane = lax.broadcasted_iota(jnp.int32, (1, LANES), 1)
    first = lane < HEAD_DIM
    row = lax.broadcasted_iota(jnp.int32, (LANES, 1), 0)
    blockdiag = (row < HEAD_DIM) == first
    problems = [(b, p) for b in range(RET_SEQS) for p in range(npair)]
    lanes_of = lambda p: slice(p * LANES, (p + 1) * LANES)
    for ci in range(in_ref.shape[1] // RET_CHUNK):
        rows = slice(ci * RET_CHUNK, (ci + 1) * RET_CHUNK)
        cos, sin = cos_ref[rows, :], sin_ref[rows, :]
        q = [_rope_rows(r[rows, 0:rw], cos, sin) for r in in_refs]
        k = [_rope_rows(r[rows, rw:2 * rw], cos, sin) * (HEAD_DIM ** -0.5) for r in in_refs]
        qb = {(b, p): q[b][:, lanes_of(p)] for b, p in problems}
        kp = {(b, p): k[b][:, lanes_of(p)] for b, p in problems}
        vb = {(b, p): in_refs[b][rows, 2 * rw + p * LANES:2 * rw + (p + 1) * LANES].astype(BF16)
              for b, p in problems}
        kb = {pr: kp[pr].astype(BF16) for pr in problems}
        carry = {(b, p): jnp.dot(qb[b, p].astype(BF16), r_scr[b, p].astype(BF16), preferred_element_type=F32)
                 for b, p in problems}
        scores = {}
        for b, p in problems:
            for hh in range(PAIR):
                own = first if hh == 0 else jnp.logical_not(first)
                qh = jnp.where(own, qb[b, p], 0.0).astype(BF16)
                scores[b, p, hh] = lax.dot_general(qh, kb[b, p], _CONTRACT_LAST, preferred_element_type=F32)
        kv = {(b, p): lax.dot_general((kp[b, p] * wk_ref[p]).astype(BF16), vb[b, p], _CONTRACT_FIRST,
                                      preferred_element_type=F32) for b, p in problems}
        inner = {(b, p, hh): jnp.dot((scores[b, p, hh] * dmat_ref[PAIR * p + hh]).astype(BF16), vb[b, p],
                                     preferred_element_type=F32)
                 for b, p in problems for hh in range(PAIR)}
        for b, p in problems:
            o = carry[b, p] * cross_ref[p] + jnp.where(first, inner[b, p, 0], inner[b, p, 1])
            r_scr[b, p] = r_scr[b, p] * decay_ref[p] + jnp.where(blockdiag, kv[b, p], 0.0)
            ms = _half_sums(o * o, first) * (1.0 / HEAD_DIM)
            gate = in_refs[b][rows, 3 * rw + p * LANES:3 * rw + (p + 1) * LANES]
            o_refs[b][rows, lanes_of(p)] = (o * lax.rsqrt(ms + EPS) * _silu(gate)).astype(o_ref.dtype)

    @pl.when(c == pl.num_programs(1) - 1)
    def _():
        rout_ref[...] = r_scr[...]


def _ret_tables(heads):
    c = RET_CHUNK
    lg = jnp.log(1.0 - 2.0 ** (-5.0 - jnp.arange(heads, dtype=F32)))
    t = jnp.arange(c, dtype=F32)
    diff = t[:, None] - t[None, :]
    dmat = jnp.where(diff >= 0, jnp.exp(lg[:, None, None] * jnp.maximum(diff, 0.0)), 0.0)
    cross = jnp.exp(lg[None, :] * (t[:, None] + 1.0))
    wk = jnp.exp(lg[:, None] * (c - 1.0 - t[None, :]))
    chunk_decay = jnp.exp(lg * c)
    per_lane = lambda a: jnp.repeat(a, HEAD_DIM, axis=-1)
    to_pairs = lambda a: jnp.swapaxes(a.reshape(c, heads // PAIR, LANES), 0, 1)
    cross_p = to_pairs(per_lane(cross))
    wk_p = to_pairs(per_lane(wk.T))
    decay_p = to_pairs(per_lane(jnp.broadcast_to(chunk_decay[None, :], (c, heads))))
    return dmat, cross_p, wk_p, decay_p


def _unpair_states(pairs):
    a = pairs[:, :, :HEAD_DIM, :HEAD_DIM]
    b = pairs[:, :, HEAD_DIM:, HEAD_DIM:]
    return jnp.stack([a, b], axis=2).reshape(pairs.shape[0], -1, HEAD_DIM, HEAD_DIM)


def _ret_prompt(ret_in, state0_pairs, cos_t, sin_t, tables, *, batch, seq):
    rw = ret_in.shape[1] // 4
    npair = rw // LANES
    c = RET_CHUNK * RET_STEP_CHUNKS
    nc = seq // c
    assert seq % c == 0 and batch % RET_SEQS == 0
    dmat, cross_p, wk_p, decay_p = tables
    const = lambda a: pl.BlockSpec(a.shape, lambda bi, ci: (0,) * a.ndim)
    state_spec = pl.BlockSpec((RET_SEQS, npair, LANES, LANES), lambda bi, ci: (bi, 0, 0, 0))
    out, rout = pl.pallas_call(
        functools.partial(_ret_kernel, rw=rw),
        grid=(batch // RET_SEQS, nc),
        in_specs=[pl.BlockSpec((RET_SEQS, c, 4 * rw), lambda bi, ci: (bi, ci, 0)),
                  state_spec,
                  pl.BlockSpec((c, rw), lambda bi, ci: (ci, 0)),
                  pl.BlockSpec((c, rw), lambda bi, ci: (ci, 0)),
                  const(dmat), const(cross_p), const(wk_p), const(decay_p)],
        out_specs=[pl.BlockSpec((RET_SEQS, c, rw), lambda bi, ci: (bi, ci, 0)), state_spec],
        out_shape=[jax.ShapeDtypeStruct((batch, seq, rw), MIX_DTYPE),
                   jax.ShapeDtypeStruct((batch, npair, LANES, LANES), F32)],
        scratch_shapes=[pltpu.VMEM((RET_SEQS, npair, LANES, LANES), F32)],
        compiler_params=_params("parallel", "arbitrary"),
        name="ret_prompt",
    )(ret_in.reshape(batch, seq, 4 * rw), state0_pairs, cos_t, sin_t, dmat, cross_p, wk_p, decay_p)
    return out.reshape(batch * seq, rw), rout


def _ret_step_kernel(in_ref, r_ref, cos_ref, sin_ref, decay_ref, o_ref, rout_ref, *, rw):
    heads = rw // HEAD_DIM
    eye = (lax.broadcasted_iota(jnp.int32, (HEAD_DIM, HEAD_DIM), 0)
           == lax.broadcasted_iota(jnp.int32, (HEAD_DIM, HEAD_DIM), 1))
    col = lambda r: jnp.sum(jnp.where(eye, r, 0.0), axis=-1, keepdims=True)
    for s in range(in_ref.shape[0]):
        row = in_ref[s]
        q = _rope_rows(row[:, 0:rw], cos_ref[...], sin_ref[...])
        k = _rope_rows(row[:, rw:2 * rw], cos_ref[...], sin_ref[...]) * (HEAD_DIM ** -0.5)
        v = row[:, 2 * rw:3 * rw]
        gate = row[:, 3 * rw:4 * rw]
        outs = []
        for h in range(heads):
            sl = slice(h * HEAD_DIM, (h + 1) * HEAD_DIM)
            qh, kh, vh = q[:, sl], k[:, sl], v[:, sl]
            decay = decay_ref[h:h + 1, :]
            state = r_ref[s, h]
            o = jnp.sum(qh * kh, axis=-1, keepdims=True) * vh
            o = o + jnp.sum(col(qh) * state, axis=0, keepdims=True) * decay
            rout_ref[s, h] = state * decay + col(kh) * vh
            o = o * lax.rsqrt(jnp.mean(o * o, axis=-1, keepdims=True) + EPS)
            outs.append(o * _silu(gate[:, sl]))
        o_ref[s] = jnp.concatenate(outs, axis=-1)


def _ret_sample(ret_in, state, cos_row, sin_row, decay_rows):
    db, rw4 = ret_in.shape
    rw = rw4 // 4
    heads = rw // HEAD_DIM
    n = _seqs_per_step(db)
    out, rout = pl.pallas_call(
        functools.partial(_ret_step_kernel, rw=rw),
        grid=(db // n,),
        in_specs=[pl.BlockSpec((n, 1, rw4), lambda bi: (bi, 0, 0)),
                  pl.BlockSpec((n, heads, HEAD_DIM, HEAD_DIM), lambda bi: (bi, 0, 0, 0)),
                  pl.BlockSpec((1, rw), lambda bi: (0, 0)),
                  pl.BlockSpec((1, rw), lambda bi: (0, 0)),
                  pl.BlockSpec((heads, HEAD_DIM), lambda bi: (0, 0))],
        out_specs=[pl.BlockSpec((n, 1, rw), lambda bi: (bi, 0, 0)),
                   pl.BlockSpec((n, heads, HEAD_DIM, HEAD_DIM), lambda bi: (bi, 0, 0, 0))],
        out_shape=[jax.ShapeDtypeStruct((db, 1, rw), F32),
                   jax.ShapeDtypeStruct(state.shape, F32)],
        compiler_params=_params("parallel"),
        name="ret_sample",
    )(ret_in.reshape(db, 1, rw4), state, cos_row, sin_row, decay_rows)
    return out.reshape(db, rw), rout


def _rope_tables(pos, heads):
    half = HEAD_DIM // 2
    freqs = ROPE_BASE ** (-jnp.arange(half, dtype=F32) / half)
    ang = pos.astype(F32)[:, None] * freqs[None, :]
    cos, sin = jnp.cos(ang), jnp.sin(ang)
    cos_t = jnp.tile(jnp.concatenate([cos, cos], axis=-1), (1, heads))
    sin_t = jnp.tile(jnp.concatenate([-sin, sin], axis=-1), (1, heads))
    return cos_t, sin_t


def _rel_bucket(d):
    n = jnp.maximum(d, 0)
    max_exact = N_BUCKETS // 2
    large = max_exact + (jnp.log(jnp.maximum(n, 1).astype(F32) / max_exact)
                         / math.log(MAX_DISTANCE / max_exact) * (N_BUCKETS - max_exact)).astype(jnp.int32)
    large = jnp.minimum(large, N_BUCKETS - 1)
    return jnp.where(n < max_exact, n, large)


def _block_rank_penalty(scores, n_past):
    blk = lax.broadcasted_iota(jnp.int32, scores.shape, 0)
    past = blk < n_past
    if n_past <= TOP_K:
        return jnp.where(past, 0.0, MASKED)
    rank = jnp.zeros(scores.shape, F32)
    for m in range(n_past):
        other = scores[m:m + 1, :]
        beats = (other > scores) | ((other == scores) & (m < blk))
        rank += jnp.where(beats, 1.0, 0.0)
    return jnp.where(past & (rank < TOP_K), 0.0, MASKED)


def _moba_kernel(q_ref, kt_ref, vt_ref, km_ref, gate_ref, bias_ref, o_ref,
                 kext_scr, vext_scr, m_scr, acc_scr, *, nb):
    tq = MOBA_BLOCK
    nq = kt_ref.shape[-1] // tq
    lane = lax.broadcasted_iota(jnp.int32, (1, LANES), 1)
    row = lax.broadcasted_iota(jnp.int32, (LANES, 1), 0)
    km = km_ref[0]

    def build_block(c):
        cols = slice(c * tq, (c + 1) * tq)
        kt = kt_ref[0, 0, :, cols]
        vt = vt_ref[0, 0, :, cols]
        for hh in range(PAIR):
            own_rows = (row < HEAD_DIM) if hh == 0 else (row >= HEAD_DIM)
            flag_row = c + (HEAD_DIM if hh == 0 else 0)
            kext = jnp.where(own_rows, kt, jnp.where(row == flag_row, 1.0, 0.0))
            kext_scr[hh, cols, :] = jnp.transpose(kext).astype(BF16)
            vext_scr[hh, :, cols] = jnp.where(own_rows, vt, 1.0).astype(BF16)

    def extended_queries(t):
        q2 = q_ref[t * tq:(t + 1) * tq, :]
        q_t = jnp.transpose(q2) * (HEAD_DIM ** -0.5 * LOG2E)
        blk = lax.broadcasted_iota(jnp.int32, (nb, tq), 0)
        pad = jnp.zeros((HEAD_DIM, tq), F32)
        tail = jnp.zeros((HEAD_DIM - nb, tq), F32)
        out = []
        for hh in range(PAIR):
            own = (lane < HEAD_DIM) if hh == 0 else (lane >= HEAD_DIM)
            own_rows = (row < HEAD_DIM) if hh == 0 else (row >= HEAD_DIM)
            if t > TOP_K:
                scores = lax.dot_general(jnp.where(own, km, 0.0), q2, _CONTRACT_LAST,
                                         precision=lax.Precision.HIGHEST, preferred_element_type=F32)
            else:
                scores = jnp.zeros((nb, tq), F32)
            pen = jnp.where(blk == t, 0.0, _block_rank_penalty(scores, t))
            pen_rows = jnp.concatenate([pad, pen, tail] if hh == 0 else [pen, tail, pad], axis=0)
            out.append(jnp.where(own_rows, q_t, pen_rows).astype(BF16))
        return out

    def steps_of(t):
        pairs = [(c, 2) for c in range(t - 1, -1, -2)]
        return pairs + ([(0, 1)] if t % 2 == 0 else [])

    schedule = [(t, c, n) for t in range(nq) for c, n in steps_of(t)]
    q_ext = {}

    def logits(t, c, n):
        if t not in q_ext:
            build_block(t)
            q_ext.clear()
            q_ext[t] = extended_queries(t)
        out = []
        for hh in range(PAIR):
            s = jnp.dot(kext_scr[hh, c * tq:(c + n) * tq, :], q_ext[t][hh], preferred_element_type=F32)
            blocks = [s[i * tq:(i + 1) * tq, :] + bias_ref[hh, min(t - c - i, 2)] for i in range(n)]
            top = functools.reduce(jnp.maximum, [jnp.max(blk, axis=0, keepdims=True) for blk in blocks])
            out.append((blocks, top))
        return out

    def softmax_pv(c, n, s_pair, first):
        for hh in range(PAIR):
            blocks, top = s_pair[hh]
            m_new = top if first else jnp.maximum(m_scr[hh], top)
            probs = jnp.concatenate([jnp.exp2((blk - m_new).astype(BF16)) for blk in blocks], axis=0)
            pv = jnp.dot(vext_scr[hh, :, c * tq:(c + n) * tq], probs, preferred_element_type=F32)
            acc_scr[hh] = pv if first else jnp.exp2(m_scr[hh] - m_new) * acc_scr[hh] + pv
            m_scr[hh] = m_new

    s_cur = logits(*schedule[0])
    for idx, (t, c, n) in enumerate(schedule):
        s_next = logits(*schedule[idx + 1]) if idx + 1 < len(schedule) else None
        softmax_pv(c, n, s_cur, first=(c + n == t + 1))
        if c == 0:
            numer = jnp.where(row < HEAD_DIM, acc_scr[0], acc_scr[1])
            denom = jnp.where(row < HEAD_DIM, pltpu.roll(acc_scr[0], HEAD_DIM, 0),
                              pltpu.roll(acc_scr[1], HEAD_DIM, 0))
            rows = slice(t * tq, (t + 1) * tq)
            o_ref[rows, :] = (jnp.transpose(numer / denom) * _silu(gate_ref[rows, :])).astype(o_ref.dtype)
        s_cur = s_next


def _toeplitz(vec, rows, cols):
    h, n = vec.shape
    assert cols <= n - 1 and rows <= n
    flat = jnp.tile(vec, (1, rows))[:, :rows * (n - 1)]
    return flat.reshape(h, rows, n - 1)[:, :, :cols]


def _moba_bias_tiles(rel_bias):
    blk = MOBA_BLOCK
    bias_t = rel_bias.astype(F32).T
    tiles = []
    n = 2 * blk - 1
    k = jnp.arange(n)
    col_minus_row = jnp.where(k < blk, k, k - n)
    for dist in range(3):
        d = dist * blk + col_minus_row
        g = bias_t[:, _rel_bucket(d)]
        if dist == 0:
            g = jnp.where((d >= 0)[None], g, -jnp.inf)
        tiles.append(_toeplitz(g, blk, blk))
    return jnp.stack(tiles, axis=1) * LOG2E


def _moba_prompt(mq, kt_buf, vt_buf, kmean, mgate, bias_tiles, *, layer, batch, seq):
    mw = mq.shape[1]
    npair = mw // LANES
    nq = seq // MOBA_BLOCK
    nb = kmean.shape[1]
    assert seq % MOBA_BLOCK == 0 and nb % 8 == 0 and 2 <= nq <= nb <= HEAD_DIM
    assert MOBA_BLOCK + 1 >= MAX_DISTANCE
    rows = lambda: pl.BlockSpec((seq, LANES), lambda b, p: (b, p))
    cols = lambda: pl.BlockSpec((1, 1, LANES, seq), lambda b, p: (layer, b, p, 0))
    return pl.pallas_call(
        functools.partial(_moba_kernel, nb=nb),
        grid=(batch, npair),
        in_specs=[rows(), cols(), cols(),
                  pl.BlockSpec((1, nb, LANES), lambda b, p: (b, 0, p)),
                  rows(),
                  pl.BlockSpec((PAIR, 3, MOBA_BLOCK, MOBA_BLOCK), lambda b, p: (p, 0, 0, 0))],
        out_specs=rows(),
        out_shape=jax.ShapeDtypeStruct((batch * seq, mw), MIX_DTYPE),
        scratch_shapes=[pltpu.VMEM((PAIR, seq, LANES), BF16),
                        pltpu.VMEM((PAIR, LANES, seq), BF16),
                        pltpu.VMEM((PAIR, 1, MOBA_BLOCK), F32),
                        pltpu.VMEM((PAIR, LANES, MOBA_BLOCK), F32)],
        compiler_params=_params("parallel", "parallel"),
        name="moba_prompt",
    )(mq, kt_buf, vt_buf, kmean, mgate, bias_tiles)


PAGES_PER_STEP = 32


def _page_score_kernel(pt_ref, q_ref, *refs, pages_per_block, heads):
    n_step_pages = len(refs) - 1
    page_refs, s_ref = refs[:n_step_pages], refs[n_step_pages]
    g = pl.program_id(1)

    @pl.when(g == 0)
    def _():
        s_ref[...] = jnp.zeros(s_ref.shape, F32)

    lane = lax.broadcasted_iota(jnp.int32, (1, LANES), 1)
    head_row = lax.broadcasted_iota(jnp.int32, (heads, 1), 0)
    blocks_per_step = n_step_pages // pages_per_block
    for bi in range(blocks_per_step):
        blk = g * blocks_per_step + bi
        pages = page_refs[bi * pages_per_block:(bi + 1) * pages_per_block]
        upd = jnp.zeros((heads, LANES), F32)
        for h in range(heads):
            rows = slice(h * HEAD_DIM, (h + 1) * HEAD_DIM)
            ksum = pages[0][0, rows, :]
            for page in pages[1:]:
                ksum = ksum + page[0, rows, :]
            part = jnp.sum(ksum * q_ref[0, rows, :], axis=0, keepdims=True)
            score = jnp.sum(part, axis=-1, keepdims=True) * (1.0 / MOBA_BLOCK)
            upd = jnp.where((head_row == h) & (lane == blk), score, upd)
        s_ref[0] += upd


def _page_scores(pt_flat, q_rep, cache_pages, *, n_pages, page_size, page0):
    db, width, _ = q_rep.shape
    heads = width // HEAD_DIM
    pages_per_block = MOBA_BLOCK // page_size
    step_pages = math.gcd(n_pages, PAGES_PER_STEP)
    assert MOBA_BLOCK % page_size == 0 and step_pages % pages_per_block == 0
    assert n_pages // pages_per_block <= LANES and page_size == LANES
    steps = n_pages // step_pages

    def page_spec(i):
        return pl.BlockSpec((1, width, page_size),
                            lambda b, g, pt: (page0 + pt[b * n_pages + g * step_pages + i], 0, 0))

    grid_spec = pltpu.PrefetchScalarGridSpec(
        num_scalar_prefetch=1,
        grid=(db, steps),
        in_specs=[pl.BlockSpec((1, width, LANES), lambda b, g, pt: (b, 0, 0))]
                 + [page_spec(i) for i in range(step_pages)],
        out_specs=pl.BlockSpec((1, heads, LANES), lambda b, g, pt: (b, 0, 0)),
    )
    return pl.pallas_call(
        functools.partial(_page_score_kernel, pages_per_block=pages_per_block, heads=heads),
        grid_spec=grid_spec,
        out_shape=jax.ShapeDtypeStruct((db, heads, LANES), F32),
        compiler_params=_params("parallel", "arbitrary"),
        name="page_scores",
    )(pt_flat, q_rep, *([cache_pages] * step_pages))


def _topk_kernel(s_ref, sel_ref, *, n_blocks):
    s = s_ref[...]
    lane = lax.broadcasted_iota(jnp.int32, s.shape, 1).astype(F32)
    s = jnp.where(lane < n_blocks, s, -jnp.inf)
    sel = jnp.zeros(s.shape, F32)
    for j in range(TOP_K):
        best = jnp.max(s, axis=-1, keepdims=True)
        idx = jnp.min(jnp.where(s == best, lane, float(LANES)), axis=-1, keepdims=True)
        sel = jnp.where(lane == j, idx, sel)
        s = jnp.where(lane == idx, -jnp.inf, s)
    sel_ref[...] = sel.astype(jnp.int32)


def _topk_blocks(scores, *, n_blocks):
    db, heads, _ = scores.shape
    assert n_blocks >= TOP_K
    spec = pl.BlockSpec((db * heads, LANES), lambda i: (0, 0))
    return pl.pallas_call(
        functools.partial(_topk_kernel, n_blocks=n_blocks),
        grid=(1,),
        in_specs=[spec],
        out_specs=spec,
        out_shape=jax.ShapeDtypeStruct((db * heads, LANES), jnp.int32),
        compiler_params=_params("arbitrary"),
        name="topk_blocks",
    )(scores.reshape(db * heads, LANES)).reshape(db, heads, LANES)


ATTEND_HEADS = 4


def _page_attend_kernel(pt_ref, seqp_ref, q_ref, kn_ref, vn_ref, gate_ref, bown_ref, bias_ref, *refs,
                        heads, n_sel_pages):
    n_tiles = ATTEND_HEADS * n_sel_pages
    k_refs, v_refs, o_ref = refs[:n_tiles], refs[n_tiles:2 * n_tiles], refs[2 * n_tiles]
    b = pl.program_id(0)
    h0 = pl.program_id(1) * ATTEND_HEADS
    diag = (lax.broadcasted_iota(jnp.int32, (HEAD_DIM, LANES), 0)
            == lax.broadcasted_iota(jnp.int32, (HEAD_DIM, LANES), 1))
    for i in range(ATTEND_HEADS):
        h = h0 + i
        dims = slice(i * HEAD_DIM, (i + 1) * HEAD_DIM)
        q = q_ref[0, dims, :] * (HEAD_DIM ** -0.5)
        logits = []
        for j in range(n_sel_pages):
            seq_page = seqp_ref[(b * heads + h) * n_sel_pages + j]
            bias = bias_ref[h, pl.ds(seq_page, 1), :]
            logits.append(jnp.sum(k_refs[i * n_sel_pages + j][0] * q, axis=0, keepdims=True) + bias)
        s_own = jnp.sum(kn_ref[0, dims, :] * q, axis=0, keepdims=True) + bown_ref[pl.ds(h, 1), :]
        top = functools.reduce(jnp.maximum, logits)
        m = jnp.maximum(jnp.max(top, axis=-1, keepdims=True), s_own)
        p_own = jnp.exp(s_own - m)
        psum = jnp.zeros_like(m)
        acc = jnp.zeros((HEAD_DIM, LANES), F32)
        for j in range(n_sel_pages):
            p = jnp.exp(logits[j] - m)
            psum += p
            acc += v_refs[i * n_sel_pages + j][0] * p
        denom = jnp.sum(psum, axis=-1, keepdims=True) + p_own
        out_rep = (jnp.sum(acc, axis=-1, keepdims=True) + p_own * vn_ref[0, dims, :]) / denom
        out_row = jnp.sum(jnp.where(diag, out_rep, 0.0), axis=0, keepdims=True)[:, :HEAD_DIM]
        o_ref[0, pl.ds(h, 1), :] = out_row * _silu(gate_ref[0, pl.ds(h, 1), :])


def _page_attend(pt_flat, seq_pages, q_rep, kn_rep, vn_rep, gate8, bown, bias_rows, k_tiles, v_tiles,
                 *, n_sel_pages, n_pages, page0):
    db, heads, _ = gate8.shape
    page_size = k_tiles.shape[-1]
    assert heads % ATTEND_HEADS == 0
    head_rep = lambda: pl.BlockSpec((1, ATTEND_HEADS * HEAD_DIM, LANES), lambda b, g, pt, sp: (b, g, 0))
    whole = lambda a: pl.BlockSpec(a.shape, lambda b, g, pt, sp: (0,) * a.ndim)

    def tile_spec(i, j):
        def index(b, g, pt, sp):
            h = g * ATTEND_HEADS + i
            page = page0 + pt[b * n_pages + sp[(b * heads + h) * n_sel_pages + j]]
            return (page * heads + h, 0, 0)
        return pl.BlockSpec((1, HEAD_DIM, page_size), index)

    tile_specs = [tile_spec(i, j) for i in range(ATTEND_HEADS) for j in range(n_sel_pages)]
    grid_spec = pltpu.PrefetchScalarGridSpec(
        num_scalar_prefetch=2,
        grid=(db, heads // ATTEND_HEADS),
        in_specs=[head_rep(), head_rep(), head_rep(),
                  pl.BlockSpec((1, heads, HEAD_DIM), lambda b, g, pt, sp: (b, 0, 0)),
                  whole(bown), whole(bias_rows)] + tile_specs * 2,
        out_specs=pl.BlockSpec((1, heads, HEAD_DIM), lambda b, g, pt, sp: (b, 0, 0)),
    )
    return pl.pallas_call(
        functools.partial(_page_attend_kernel, heads=heads, n_sel_pages=n_sel_pages),
        grid_spec=grid_spec,
        out_shape=jax.ShapeDtypeStruct((db, heads, HEAD_DIM), F32),
        compiler_params=_params("parallel", "arbitrary"),
        name="page_attend",
    )(pt_flat, seq_pages, q_rep, kn_rep, vn_rep, gate8, bown, bias_rows,
      *([k_tiles] * len(tile_specs)), *([v_tiles] * len(tile_specs)))


def _sample_bias_rows(rel_bias, *, past_len, page_size):
    kpos = jnp.arange(past_len)
    onehot = (_rel_bucket(past_len - kpos)[:, None] == jnp.arange(N_BUCKETS)[None, :]).astype(F32)
    bias = jnp.dot(onehot, rel_bias.astype(F32), precision=lax.Precision.HIGHEST)
    return bias.T.reshape(rel_bias.shape[1], past_len // page_size, page_size)


def _lane_replicated(x):
    return jnp.broadcast_to(x[:, :, None], x.shape + (LANES,))


def _moba_sample(mq, mk, mv, mgate, k_cache_t, v_cache_t, page0, page_table, rel_bias, bias_rows, *, page_size):
    db, mw = mq.shape
    heads = mw // HEAD_DIM
    n_pages = page_table.shape[1]
    past_len = n_pages * page_size
    pages_per_block = MOBA_BLOCK // page_size
    n_blocks = past_len // MOBA_BLOCK
    assert past_len % MOBA_BLOCK == 0
    n_rows = k_cache_t.shape[0]
    pt_flat = page_table.reshape(-1)
    q_rep = _lane_replicated(mq)
    scores = _page_scores(pt_flat, q_rep, k_cache_t.reshape(n_rows, mw, page_size), n_pages=n_pages,
                          page_size=page_size, page0=page0)
    sel = _topk_blocks(scores, n_blocks=n_blocks)[:, :, :TOP_K]
    seq_pages = (sel[..., None] * pages_per_block + jnp.arange(pages_per_block, dtype=jnp.int32)).reshape(-1)
    bown = jnp.broadcast_to(rel_bias.astype(F32)[0][:, None], (heads, LANES))
    out = _page_attend(pt_flat, seq_pages, q_rep, _lane_replicated(mk), _lane_replicated(mv),
                       mgate.reshape(db, heads, HEAD_DIM), bown, bias_rows,
                       k_cache_t.reshape(n_rows * heads, HEAD_DIM, page_size),
                       v_cache_t.reshape(n_rows * heads, HEAD_DIM, page_size),
                       n_sel_pages=TOP_K * pages_per_block, n_pages=n_pages, page0=page0)
    return out.reshape(db, mw)


PROMPT_TM = 512
OUTPROJ_TM = 1024
CONV_TS = 512


def kernel(x_prompt, x_sample, cache_k, cache_v, page_table, state_ret, state_conv, norm_g, w_in, conv_w,
           conv_b, conv_ln_g, conv_ln_b, w_out, rel_bias, final_g):
    batch, seq, d = x_prompt.shape
    db, dec_seq, _ = x_sample.shape
    assert dec_seq == 1, "the sample path handles one new token per sequence"
    depth = w_in.shape[0]
    ret_heads = d // 256
    moba_heads = d // 128
    conv_ch = d // 4
    n_pages = page_table.shape[1]
    page_size = cache_k.shape[2]
    past_len = n_pages * page_size
    n_phys = cache_k.shape[1]
    tokens_minor = lambda c: jnp.transpose(c, (0, 1, 3, 4, 2)).reshape(depth * n_phys, moba_heads, HEAD_DIM,
                                                                         page_size)
    k_cache_t, v_cache_t = tokens_minor(cache_k), tokens_minor(cache_v)

    mw = moba_heads * HEAD_DIM
    k0 = 3 * conv_ch + 4 * ret_heads * HEAD_DIM + mw
    w_rows = jnp.concatenate([w_in[:, :, :k0], w_in[:, :, k0 + 2 * mw:]], axis=-1).astype(BF16)
    w_kv_t = jnp.swapaxes(w_in[:, :, k0:k0 + 2 * mw], 1, 2).astype(BF16)
    w_in_b = w_in.astype(BF16)
    w_out_b = w_out.astype(BF16)
    cos_p, sin_p = _rope_tables(jnp.arange(seq, dtype=jnp.int32), ret_heads)
    cos_s, sin_s = _rope_tables(past_len + jnp.arange(1, dtype=jnp.int32), ret_heads)
    ret_tables = _ret_tables(ret_heads)
    lg = jnp.log(1.0 - 2.0 ** (-5.0 - jnp.arange(ret_heads, dtype=F32)))
    decay_rows = jnp.broadcast_to(jnp.exp(lg)[:, None], (ret_heads, HEAD_DIM))
    bias_tiles = _moba_bias_tiles(rel_bias)
    bias_rows = _sample_bias_rows(rel_bias, past_len=past_len, page_size=page_size)
    conv0 = jnp.zeros((batch, CONV_WIDTH - 1, conv_ch), F32)
    ret0 = jnp.zeros((batch, ret_heads // PAIR, LANES, LANES), F32)
    nb = seq // MOBA_BLOCK
    nb_pad = -(-nb // 8) * 8

    hp = x_prompt.reshape(batch * seq, d)
    hs = x_sample.reshape(db, d)
    outs = {name: [] for name in ("ks", "vs", "rp", "rs", "cp", "cs")}
    kt_buf = vt_buf = None
    for l in range(depth):
        last = l == depth - 1
        conv_in, ret_in, mq, mgate, kt_buf, vt_buf, kmean = _inproj_prompt(
            hp, norm_g[l], w_rows[l], w_kv_t[l], kt_buf, vt_buf, layer=l, depth=depth, batch=batch,
            seq=seq, tm=PROMPT_TM)
        conv_out, conv_state = _conv_prompt(conv_in, conv0, conv_w[l], conv_b[l], conv_ln_g[l], conv_ln_b[l],
                                            batch=batch, seq=seq, ts=CONV_TS)
        ret_out, ret_state = _ret_prompt(ret_in, ret0, cos_p, sin_p, ret_tables, batch=batch, seq=seq)
        kmean = jnp.pad(kmean, ((0, 0), (0, nb_pad - nb), (0, 0)))
        moba_out = _moba_prompt(mq, kt_buf, vt_buf, kmean, mgate, bias_tiles, layer=l, batch=batch, seq=seq)
        hp = _outproj(hp, conv_out, ret_out, moba_out, w_out_b[l], final_g, tm=math.gcd(OUTPROJ_TM, batch * seq),
                      final_norm=last)
        outs["rp"].append(_unpair_states(ret_state))
        outs["cp"].append(conv_state)
        conv_in, ret_in, mq, mk, mv, mgate = _inproj(hs, norm_g[l], w_in_b[l])
        conv_out, conv_state = _conv_sample(conv_in, state_conv[l], conv_w[l], conv_b[l], conv_ln_g[l],
                                            conv_ln_b[l])
        ret_out, ret_state = _ret_sample(ret_in, state_ret[l], cos_s, sin_s, decay_rows)
        moba_out = _moba_sample(mq, mk, mv, mgate, k_cache_t, v_cache_t, l * n_phys, page_table, rel_bias,
                                bias_rows, page_size=page_size)
        hs = _outproj(hs, conv_out, ret_out, moba_out, w_out_b[l], final_g, tm=db, final_norm=last)
        outs["ks"].append(mk.reshape(db, 1, moba_heads, HEAD_DIM))
        outs["vs"].append(mv.reshape(db, 1, moba_heads, HEAD_DIM))
        outs["rs"].append(ret_state)
        outs["cs"].append(conv_state)

    st = {name: jnp.stack(vals) for name, vals in outs.items()}
    rows_major = lambda t: jnp.transpose(t.reshape(depth, batch, moba_heads, HEAD_DIM, seq), (0, 1, 4, 2, 3))
    return (hp.reshape(batch, seq, d), hs.reshape(db, 1, d), rows_major(kt_buf), rows_major(vt_buf),
            st["ks"], st["vs"], st["rp"], st["rs"], st["cp"], st["cs"])
```

```python
import functools
import math

import jax
import jax.numpy as jnp
from jax import lax
from jax.experimental import pallas as pl
from jax.experimental.pallas import tpu as pltpu

F32 = jnp.float32
BF16 = jnp.bfloat16
MIX_DTYPE = BF16

HEAD_DIM = 64
CONV_WIDTH = 31
MOBA_BLOCK = 256
TOP_K = 3
N_BUCKETS = 32
MAX_DISTANCE = 128
RET_CHUNK = 128
ROPE_BASE = 10000.0
EPS = 1e-6

LANES = 128
SUBLANES = 8
PAIR = LANES // HEAD_DIM
MASKED = -1e30
LOG2E = math.log2(math.e)
VMEM_LIMIT = 56 * 1024 * 1024
_CONTRACT_LAST = (((1,), (1,)), ((), ()))
_CONTRACT_FIRST = (((0,), (0,)), ((), ()))


def _silu(x):
    return x * jax.nn.sigmoid(x)


def _params(*sem):
    return pltpu.CompilerParams(dimension_semantics=sem, vmem_limit_bytes=VMEM_LIMIT)


def _inproj_kernel(x_ref, g_ref, w_ref, *out_refs, splits):
    x = x_ref[...]
    h = x * lax.rsqrt(jnp.mean(x * x, axis=-1, keepdims=True) + EPS) * g_ref[...]
    hb = h.astype(BF16)
    for (c0, c1), o_ref in zip(splits, out_refs):
        o_ref[...] = jnp.dot(hb, w_ref[:, c0:c1], preferred_element_type=F32)


def _inproj(x2d, g, w_bf16):
    m, d = x2d.shape
    conv_ch, ret_w, moba_w = d // 4, (d // 256) * HEAD_DIM, (d // 128) * HEAD_DIM
    widths = (3 * conv_ch, 4 * ret_w, moba_w, moba_w, moba_w, moba_w)
    edges = [0]
    for wd in widths:
        edges.append(edges[-1] + wd)
    splits = tuple(zip(edges[:-1], edges[1:]))
    assert edges[-1] == w_bf16.shape[1]
    return pl.pallas_call(
        functools.partial(_inproj_kernel, splits=splits),
        grid=(1,),
        in_specs=[pl.BlockSpec((m, d), lambda i: (0, 0)),
                  pl.BlockSpec((1, d), lambda i: (0, 0)),
                  pl.BlockSpec(w_bf16.shape, lambda i: (0, 0))],
        out_specs=[pl.BlockSpec((m, wd), lambda i: (0, 0)) for wd in widths],
        out_shape=[jax.ShapeDtypeStruct((m, wd), F32) for wd in widths],
        compiler_params=_params("arbitrary"),
        name="inproj_sample",
    )(x2d, g.reshape(1, d), w_bf16)


def _inproj_prompt_kernel(x_ref, g_ref, w_ref, wkv_ref, *refs, splits, n_mean, aliased):
    if aliased:
        refs = refs[2:]
    conv_ref, ret_ref, q_ref, gate_ref, kt_ref, vt_ref, km_ref = refs
    x = x_ref[...]
    h = x * lax.rsqrt(jnp.mean(x * x, axis=-1, keepdims=True) + EPS) * g_ref[...]
    hb = h.astype(BF16)
    for (c0, c1), o_ref in zip(splits, (conv_ref, ret_ref, q_ref, gate_ref)):
        o_ref[...] = jnp.dot(hb, w_ref[:, c0:c1], preferred_element_type=F32)
    mw = kt_ref.shape[2]
    kt = lax.dot_general(wkv_ref[0:mw, :], hb, _CONTRACT_LAST, preferred_element_type=F32)
    kt_ref[0, 0] = kt
    vt_ref[0, 0] = lax.dot_general(wkv_ref[mw:2 * mw, :], hb, _CONTRACT_LAST, preferred_element_type=F32)
    for i in range(n_mean):
        blk = kt[:, i * MOBA_BLOCK:(i + 1) * MOBA_BLOCK]
        km_ref[0, :, i:i + 1] = jnp.sum(blk, axis=-1, keepdims=True) * (1.0 / MOBA_BLOCK)


def _inproj_prompt(x2d, g, w_rows, w_kv_t, kt_buf, vt_buf, *, layer, depth, batch, seq, tm):
    m, d = x2d.shape
    conv_ch, ret_w, mw = d // 4, (d // 256) * HEAD_DIM, (d // 128) * HEAD_DIM
    widths = (3 * conv_ch, 4 * ret_w, mw, mw)
    edges = [0]
    for wd in widths:
        edges.append(edges[-1] + wd)
    splits = tuple(zip(edges[:-1], edges[1:]))
    assert edges[-1] == w_rows.shape[1] and seq % tm == 0 and tm % MOBA_BLOCK == 0
    n_mean = tm // MOBA_BLOCK
    assert n_mean <= 8
    per_seq = seq // tm
    aliased = kt_buf is not None
    row = lambda wd: pl.BlockSpec((tm, wd), lambda i: (i, 0))
    const = lambda a: pl.BlockSpec(a.shape, lambda i: (0,) * a.ndim)
    kv_spec = pl.BlockSpec((1, 1, mw, tm), lambda i: (layer, i // per_seq, 0, i % per_seq))
    kv_shape = jax.ShapeDtypeStruct((depth, batch, mw, seq), F32)
    in_specs = [row(d), pl.BlockSpec((1, d), lambda i: (0, 0)), const(w_rows), const(w_kv_t)]
    args = [x2d, g.reshape(1, d), w_rows, w_kv_t]
    aliases = {}
    if aliased:
        in_specs += [pl.BlockSpec(memory_space=pl.ANY)] * 2
        args += [kt_buf, vt_buf]
        aliases = {4: 4, 5: 5}
    *outs, km_t = pl.pallas_call(
        functools.partial(_inproj_prompt_kernel, splits=splits, n_mean=n_mean, aliased=aliased),
        grid=(m // tm,),
        in_specs=in_specs,
        out_specs=[row(wd) for wd in widths] + [kv_spec, kv_spec,
                                                pl.BlockSpec((1, mw, n_mean), lambda i: (i, 0, 0))],
        out_shape=[jax.ShapeDtypeStruct((m, wd), F32) for wd in widths]
                  + [kv_shape, kv_shape, jax.ShapeDtypeStruct((m // tm, mw, n_mean), F32)],
        input_output_aliases=aliases,
        compiler_params=_params("parallel"),
        name="inproj_prompt",
    )(*args)
    kmean = jnp.transpose(km_t.reshape(batch, per_seq, mw, n_mean), (0, 1, 3, 2)).reshape(batch, -1, mw)
    return (*outs, kmean)


def _outproj_kernel(x_ref, c_ref, r_ref, m_ref, w_ref, fg_ref, y_ref, *, conv_ch, ret_w, final_norm):
    y = x_ref[...]
    y += jnp.dot(c_ref[...].astype(BF16), w_ref[0:conv_ch, :], preferred_element_type=F32)
    y += jnp.dot(r_ref[...].astype(BF16), w_ref[conv_ch:conv_ch + ret_w, :], preferred_element_type=F32)
    y += jnp.dot(m_ref[...].astype(BF16), w_ref[conv_ch + ret_w:, :], preferred_element_type=F32)
    if final_norm:
        y = y * lax.rsqrt(jnp.mean(y * y, axis=-1, keepdims=True) + EPS) * fg_ref[...]
    y_ref[...] = y


def _outproj(x2d, conv_out, ret_out, moba_out, w_bf16, final_g, *, tm, final_norm):
    m, d = x2d.shape
    conv_ch, ret_w, moba_w = conv_out.shape[1], ret_out.shape[1], moba_out.shape[1]
    row = lambda wd: pl.BlockSpec((tm, wd), lambda i: (i, 0))
    return pl.pallas_call(
        functools.partial(_outproj_kernel, conv_ch=conv_ch, ret_w=ret_w, final_norm=final_norm),
        grid=(m // tm,),
        in_specs=[row(d), row(conv_ch), row(ret_w), row(moba_w),
                  pl.BlockSpec(w_bf16.shape, lambda i: (0, 0)),
                  pl.BlockSpec((1, d), lambda i: (0, 0))],
        out_specs=row(d),
        out_shape=jax.ShapeDtypeStruct((m, d), F32),
        compiler_params=_params("parallel"),
        name="outproj",
    )(x2d, conv_out, ret_out, moba_out, w_bf16, final_g.reshape(1, d))


def _layernorm_silu(y, g, b):
    mu = jnp.mean(y, axis=-1, keepdims=True)
    yc = y - mu
    yn = yc * lax.rsqrt(jnp.mean(yc * yc, axis=-1, keepdims=True) + EPS)
    return _silu(yn * g + b)


CONV_PAD = 32
CONV_ROWS = 64


def _conv_kernel(in_ref, prev_ref, w_ref, b_ref, g_ref, beta_ref, o_ref, st_ref, buf, shifted, *, ts, ch):
    t = pl.program_id(1)
    keep = CONV_WIDTH - 1
    lo = CONV_PAD - keep

    @pl.when(t == 0)
    def _():
        buf[lo:CONV_PAD, :] = prev_ref[0]

    @pl.when(t > 0)
    def _():
        buf[lo:CONV_PAD, :] = buf[ts + lo:ts + CONV_PAD, :]

    buf[CONV_PAD:CONV_PAD + ts, :] = in_ref[:, 0:ch] * jax.nn.sigmoid(in_ref[:, ch:2 * ch])
    used = ts + CONV_PAD - SUBLANES
    for k in range(1, SUBLANES):
        shifted[k - 1, 0:used, :] = buf[k:k + used, :]
    for r0 in range(0, ts, CONV_ROWS):
        acc = jnp.zeros((CONV_ROWS, ch), F32)
        for j in range(CONV_WIDTH):
            k = (lo + j) % SUBLANES
            base = r0 + lo + j - k
            rows = buf[base:base + CONV_ROWS, :] if k == 0 else shifted[k - 1, base:base + CONV_ROWS, :]
            acc += rows * w_ref[j:j + 1, :]
        y = _layernorm_silu(acc + b_ref[...], g_ref[...], beta_ref[...])
        gated = y * _silu(in_ref[r0:r0 + CONV_ROWS, 2 * ch:3 * ch])
        o_ref[r0:r0 + CONV_ROWS, :] = gated.astype(o_ref.dtype)

    @pl.when(t == pl.num_programs(1) - 1)
    def _():
        st_ref[0] = buf[ts + lo:ts + CONV_PAD, :]


def _conv_prompt(conv_in, prev, w, b, g, beta, *, batch, seq, ts):
    ch = conv_in.shape[1] // 3
    nts = seq // ts
    assert seq % ts == 0 and ts % CONV_ROWS == 0 and ts >= CONV_PAD
    vec = lambda: pl.BlockSpec((1, ch), lambda bi, t: (0, 0))
    return pl.pallas_call(
        functools.partial(_conv_kernel, ts=ts, ch=ch),
        grid=(batch, nts),
        in_specs=[pl.BlockSpec((ts, 3 * ch), lambda bi, t: (bi * nts + t, 0)),
                  pl.BlockSpec((1, CONV_WIDTH - 1, ch), lambda bi, t: (bi, 0, 0)),
                  pl.BlockSpec((CONV_WIDTH, ch), lambda bi, t: (0, 0)),
                  vec(), vec(), vec()],
        out_specs=[pl.BlockSpec((ts, ch), lambda bi, t: (bi * nts + t, 0)),
                   pl.BlockSpec((1, CONV_WIDTH - 1, ch), lambda bi, t: (bi, 0, 0))],
        out_shape=[jax.ShapeDtypeStruct((batch * seq, ch), MIX_DTYPE),
                   jax.ShapeDtypeStruct((batch, CONV_WIDTH - 1, ch), F32)],
        scratch_shapes=[pltpu.VMEM((ts + CONV_PAD, ch), F32),
                        pltpu.VMEM((SUBLANES - 1, ts + CONV_PAD, ch), F32)],
        compiler_params=_params("parallel", "arbitrary"),
        name="conv_prompt",
    )(conv_in, prev, w, b.reshape(1, ch), g.reshape(1, ch), beta.reshape(1, ch))


SAMPLE_SEQS = 8


def _seqs_per_step(db):
    return math.gcd(db, SAMPLE_SEQS)


def _conv_step_kernel(in_ref, prev_ref, w_ref, b_ref, g_ref, beta_ref, o_ref, st_ref, *, ch):
    keep = CONV_WIDTH - 1
    for s in range(in_ref.shape[0]):
        row = in_ref[s]
        u = row[:, 0:ch] * jax.nn.sigmoid(row[:, ch:2 * ch])
        prev = prev_ref[s]
        acc = jnp.sum(prev * w_ref[0:keep, :], axis=0, keepdims=True) + u * w_ref[keep:keep + 1, :]
        y = _layernorm_silu(acc + b_ref[...], g_ref[...], beta_ref[...])
        o_ref[s] = y * _silu(row[:, 2 * ch:3 * ch])
        st_ref[s, 0:keep - 1, :] = prev_ref[s, 1:keep, :]
        st_ref[s, keep - 1:keep, :] = u


def _conv_sample(conv_in, prev, w, b, g, beta):
    db, ch3 = conv_in.shape
    ch = ch3 // 3
    keep = CONV_WIDTH - 1
    n = _seqs_per_step(db)
    vec = lambda: pl.BlockSpec((1, ch), lambda bi: (0, 0))
    out, st = pl.pallas_call(
        functools.partial(_conv_step_kernel, ch=ch),
        grid=(db // n,),
        in_specs=[pl.BlockSpec((n, 1, ch3), lambda bi: (bi, 0, 0)),
                  pl.BlockSpec((n, keep, ch), lambda bi: (bi, 0, 0)),
                  pl.BlockSpec((CONV_WIDTH, ch), lambda bi: (0, 0)),
                  vec(), vec(), vec()],
        out_specs=[pl.BlockSpec((n, 1, ch), lambda bi: (bi, 0, 0)),
                   pl.BlockSpec((n, keep, ch), lambda bi: (bi, 0, 0))],
        out_shape=[jax.ShapeDtypeStruct((db, 1, ch), F32),
                   jax.ShapeDtypeStruct((db, keep, ch), F32)],
        compiler_params=_params("parallel"),
        name="conv_sample",
    )(conv_in.reshape(db, 1, ch3), prev, w, b.reshape(1, ch), g.reshape(1, ch), beta.reshape(1, ch))
    return out.reshape(db, ch), st


def _rope_rows(x, cos, sin_signed):
    w = x.shape[-1]
    half = HEAD_DIM // 2
    lane = lax.broadcasted_iota(jnp.int32, (1, w), 1)
    partner = jnp.where((lane % HEAD_DIM) < half, pltpu.roll(x, w - half, 1), pltpu.roll(x, half, 1))
    return x * cos + partner * sin_signed


def _half_sums(x, first_half):
    a = jnp.sum(jnp.where(first_half, x, 0.0), axis=-1, keepdims=True)
    b = jnp.sum(jnp.where(first_half, 0.0, x), axis=-1, keepdims=True)
    return jnp.where(first_half, a, b)


RET_SEQS = 2
RET_STEP_CHUNKS = 4


def _ret_kernel(in_ref, r0_ref, cos_ref, sin_ref, dmat_ref, cross_ref, wk_ref, decay_ref,
                o_ref, rout_ref, r_scr, *, rw):
    in_refs = [in_ref.at[b] for b in range(RET_SEQS)]
    o_refs = [o_ref.at[b] for b in range(RET_SEQS)]
    c = pl.program_id(1)
    npair = rw // LANES

    @pl.when(c == 0)
    def _():
        r_scr[...] = r0_ref[...]

    lane = lax.broadcasted_iota(jnp.int32, (1, LANES), 1)
    first = lane < HEAD_DIM
    row = lax.broadcasted_iota(jnp.int32, (LANES, 1), 0)
    blockdiag = (row < HEAD_DIM) == first
    problems = [(b, p) for b in range(RET_SEQS) for p in range(npair)]
    lanes_of = lambda p: slice(p * LANES, (p + 1) * LANES)
    for ci in range(in_ref.shape[1] // RET_CHUNK):
        rows = slice(ci * RET_CHUNK, (ci + 1) * RET_CHUNK)
        cos, sin = cos_ref[rows, :], sin_ref[rows, :]
        q = [_rope_rows(r[rows, 0:rw], cos, sin) for r in in_refs]
        k = [_rope_rows(r[rows, rw:2 * rw], cos, sin) * (HEAD_DIM ** -0.5) for r in in_refs]
        qb = {(b, p): q[b][:, lanes_of(p)] for b, p in problems}
        kp = {(b, p): k[b][:, lanes_of(p)] for b, p in problems}
        vb = {(b, p): in_refs[b][rows, 2 * rw + p * LANES:2 * rw + (p + 1) * LANES].astype(BF16)
              for b, p in problems}
        kb = {pr: kp[pr].astype(BF16) for pr in problems}
        carry = {(b, p): jnp.dot(qb[b, p].astype(BF16), r_scr[b, p].astype(BF16), preferred_element_type=F32)
                 for b, p in problems}
        scores = {}
        for b, p in problems:
            for hh in range(PAIR):
                own = first if hh == 0 else jnp.logical_not(first)
                qh = jnp.where(own, qb[b, p], 0.0).astype(BF16)
                scores[b, p, hh] = lax.dot_general(qh, kb[b, p], _CONTRACT_LAST, preferred_element_type=F32)
        kv = {(b, p): lax.dot_general((kp[b, p] * wk_ref[p]).astype(BF16), vb[b, p], _CONTRACT_FIRST,
                                      preferred_element_type=F32) for b, p in problems}
        inner = {(b, p, hh): jnp.dot((scores[b, p, hh] * dmat_ref[PAIR * p + hh]).astype(BF16), vb[b, p],
                                     preferred_element_type=F32)
                 for b, p in problems for hh in range(PAIR)}
        for b, p in problems:
            o = carry[b, p] * cross_ref[p] + jnp.where(first, inner[b, p, 0], inner[b, p, 1])
            r_scr[b, p] = r_scr[b, p] * decay_ref[p] + jnp.where(blockdiag, kv[b, p], 0.0)
            ms = _half_sums(o * o, first) * (1.0 / HEAD_DIM)
            gate = in_refs[b][rows, 3 * rw + p * LANES:3 * rw + (p + 1) * LANES]
            o_refs[b][rows, lanes_of(p)] = (o * lax.rsqrt(ms + EPS) * _silu(gate)).astype(o_ref.dtype)

    @pl.when(c == pl.num_programs(1) - 1)
    def _():
        rout_ref[...] = r_scr[...]


def _ret_tables(heads):
    c = RET_CHUNK
    lg = jnp.log(1.0 - 2.0 ** (-5.0 - jnp.arange(heads, dtype=F32)))
    t = jnp.arange(c, dtype=F32)
    diff = t[:, None] - t[None, :]
    dmat = jnp.where(diff >= 0, jnp.exp(lg[:, None, None] * jnp.maximum(diff, 0.0)), 0.0)
    cross = jnp.exp(lg[None, :] * (t[:, None] + 1.0))
    wk = jnp.exp(lg[:, None] * (c - 1.0 - t[None, :]))
    chunk_decay = jnp.exp(lg * c)
    per_lane = lambda a: jnp.repeat(a, HEAD_DIM, axis=-1)
    to_pairs = lambda a: jnp.swapaxes(a.reshape(c, heads // PAIR, LANES), 0, 1)
    cross_p = to_pairs(per_lane(cross))
    wk_p = to_pairs(per_lane(wk.T))
    decay_p = to_pairs(per_lane(jnp.broadcast_to(chunk_decay[None, :], (c, heads))))
    return dmat, cross_p, wk_p, decay_p


def _unpair_states(pairs):
    a = pairs[:, :, :HEAD_DIM, :HEAD_DIM]
    b = pairs[:, :, HEAD_DIM:, HEAD_DIM:]
    return jnp.stack([a, b], axis=2).reshape(pairs.shape[0], -1, HEAD_DIM, HEAD_DIM)


def _ret_prompt(ret_in, state0_pairs, cos_t, sin_t, tables, *, batch, seq):
    rw = ret_in.shape[1] // 4
    npair = rw // LANES
    c = RET_CHUNK * RET_STEP_CHUNKS
    nc = seq // c
    assert seq % c == 0 and batch % RET_SEQS == 0
    dmat, cross_p, wk_p, decay_p = tables
    const = lambda a: pl.BlockSpec(a.shape, lambda bi, ci: (0,) * a.ndim)
    state_spec = pl.BlockSpec((RET_SEQS, npair, LANES, LANES), lambda bi, ci: (bi, 0, 0, 0))
    out, rout = pl.pallas_call(
        functools.partial(_ret_kernel, rw=rw),
        grid=(batch // RET_SEQS, nc),
        in_specs=[pl.BlockSpec((RET_SEQS, c, 4 * rw), lambda bi, ci: (bi, ci, 0)),
                  state_spec,
                  pl.BlockSpec((c, rw), lambda bi, ci: (ci, 0)),
                  pl.BlockSpec((c, rw), lambda bi, ci: (ci, 0)),
                  const(dmat), const(cross_p), const(wk_p), const(decay_p)],
        out_specs=[pl.BlockSpec((RET_SEQS, c, rw), lambda bi, ci: (bi, ci, 0)), state_spec],
        out_shape=[jax.ShapeDtypeStruct((batch, seq, rw), MIX_DTYPE),
                   jax.ShapeDtypeStruct((batch, npair, LANES, LANES), F32)],
        scratch_shapes=[pltpu.VMEM((RET_SEQS, npair, LANES, LANES), F32)],
        compiler_params=_params("parallel", "arbitrary"),
        name="ret_prompt",
    )(ret_in.reshape(batch, seq, 4 * rw), state0_pairs, cos_t, sin_t, dmat, cross_p, wk_p, decay_p)
    return out.reshape(batch * seq, rw), rout


def _ret_step_kernel(in_ref, r_ref, cos_ref, sin_ref, decay_ref, o_ref, rout_ref, *, rw):
    heads = rw // HEAD_DIM
    eye = (lax.broadcasted_iota(jnp.int32, (HEAD_DIM, HEAD_DIM), 0)
           == lax.broadcasted_iota(jnp.int32, (HEAD_DIM, HEAD_DIM), 1))
    col = lambda r: jnp.sum(jnp.where(eye, r, 0.0), axis=-1, keepdims=True)
    for s in range(in_ref.shape[0]):
        row = in_ref[s]
        q = _rope_rows(row[:, 0:rw], cos_ref[...], sin_ref[...])
        k = _rope_rows(row[:, rw:2 * rw], cos_ref[...], sin_ref[...]) * (HEAD_DIM ** -0.5)
        v = row[:, 2 * rw:3 * rw]
        gate = row[:, 3 * rw:4 * rw]
        outs = []
        for h in range(heads):
            sl = slice(h * HEAD_DIM, (h + 1) * HEAD_DIM)
            qh, kh, vh = q[:, sl], k[:, sl], v[:, sl]
            decay = decay_ref[h:h + 1, :]
            state = r_ref[s, h]
            o = jnp.sum(qh * kh, axis=-1, keepdims=True) * vh
            o = o + jnp.sum(col(qh) * state, axis=0, keepdims=True) * decay
            rout_ref[s, h] = state * decay + col(kh) * vh
            o = o * lax.rsqrt(jnp.mean(o * o, axis=-1, keepdims=True) + EPS)
            outs.append(o * _silu(gate[:, sl]))
        o_ref[s] = jnp.concatenate(outs, axis=-1)


def _ret_sample(ret_in, state, cos_row, sin_row, decay_rows):
    db, rw4 = ret_in.shape
    rw = rw4 // 4
    heads = rw // HEAD_DIM
    n = _seqs_per_step(db)
    out, rout = pl.pallas_call(
        functools.partial(_ret_step_kernel, rw=rw),
        grid=(db // n,),
        in_specs=[pl.BlockSpec((n, 1, rw4), lambda bi: (bi, 0, 0)),
                  pl.BlockSpec((n, heads, HEAD_DIM, HEAD_DIM), lambda bi: (bi, 0, 0, 0)),
                  pl.BlockSpec((1, rw), lambda bi: (0, 0)),
                  pl.BlockSpec((1, rw), lambda bi: (0, 0)),
                  pl.BlockSpec((heads, HEAD_DIM), lambda bi: (0, 0))],
        out_specs=[pl.BlockSpec((n, 1, rw), lambda bi: (bi, 0, 0)),
                   pl.BlockSpec((n, heads, HEAD_DIM, HEAD_DIM), lambda bi: (bi, 0, 0, 0))],
        out_shape=[jax.ShapeDtypeStruct((db, 1, rw), F32),
                   jax.ShapeDtypeStruct(state.shape, F32)],
        compiler_params=_params("parallel"),
        name="ret_sample",
    )(ret_in.reshape(db, 1, rw4), state, cos_row, sin_row, decay_rows)
    return out.reshape(db, rw), rout


def _rope_tables(pos, heads):
    half = HEAD_DIM // 2
    freqs = ROPE_BASE ** (-jnp.arange(half, dtype=F32) / half)
    ang = pos.astype(F32)[:, None] * freqs[None, :]
    cos, sin = jnp.cos(ang), jnp.sin(ang)
    cos_t = jnp.tile(jnp.concatenate([cos, cos], axis=-1), (1, heads))
    sin_t = jnp.tile(jnp.concatenate([-sin, sin], axis=-1), (1, heads))
    return cos_t, sin_t


def _rel_bucket(d):
    n = jnp.maximum(d, 0)
    max_exact = N_BUCKETS // 2
    large = max_exact + (jnp.log(jnp.maximum(n, 1).astype(F32) / max_exact)
                         / math.log(MAX_DISTANCE / max_exact) * (N_BUCKETS - max_exact)).astype(jnp.int32)
    large = jnp.minimum(large, N_BUCKETS - 1)
    return jnp.where(n < max_exact, n, large)


def _block_rank_penalty(scores, n_past):
    blk = lax.broadcasted_iota(jnp.int32, scores.shape, 0)
    past = blk < n_past
    if n_past <= TOP_K:
        return jnp.where(past, 0.0, MASKED)
    rank = jnp.zeros(scores.shape, F32)
    for m in range(n_past):
        other = scores[m:m + 1, :]
        beats = (other > scores) | ((other == scores) & (m < blk))
        rank += jnp.where(beats, 1.0, 0.0)
    return jnp.where(past & (rank < TOP_K), 0.0, MASKED)


def _moba_kernel(pt_ref, q_ref, kt_ref, vt_ref, km_ref, gate_ref, bias_ref, *refs, nb, n_step_pages,
                 pages_per_block):
    page_refs = refs[:n_step_pages]
    o_ref, ksum_ref, kext_scr, vext_scr, m_scr, acc_scr = refs[n_step_pages:]
    tq = MOBA_BLOCK
    nq = kt_ref.shape[-1] // tq
    lane = lax.broadcasted_iota(jnp.int32, (1, LANES), 1)
    row = lax.broadcasted_iota(jnp.int32, (LANES, 1), 0)
    km = km_ref[0]

    def build_block(c):
        cols = slice(c * tq, (c + 1) * tq)
        kt = kt_ref[0, 0, :, cols]
        vt = vt_ref[0, 0, :, cols]
        for hh in range(PAIR):
            own_rows = (row < HEAD_DIM) if hh == 0 else (row >= HEAD_DIM)
            flag_row = c + (HEAD_DIM if hh == 0 else 0)
            kext = jnp.where(own_rows, kt, jnp.where(row == flag_row, 1.0, 0.0))
            kext_scr[hh, cols, :] = jnp.transpose(kext).astype(BF16)
            vext_scr[hh, :, cols] = jnp.where(own_rows, vt, 1.0).astype(BF16)

    def extended_queries(t):
        q2 = q_ref[t * tq:(t + 1) * tq, :]
        q_t = jnp.transpose(q2) * (HEAD_DIM ** -0.5 * LOG2E)
        blk = lax.broadcasted_iota(jnp.int32, (nb, tq), 0)
        pad = jnp.zeros((HEAD_DIM, tq), F32)
        tail = jnp.zeros((HEAD_DIM - nb, tq), F32)
        out = []
        for hh in range(PAIR):
            own = (lane < HEAD_DIM) if hh == 0 else (lane >= HEAD_DIM)
            own_rows = (row < HEAD_DIM) if hh == 0 else (row >= HEAD_DIM)
            if t > TOP_K:
                scores = lax.dot_general(jnp.where(own, km, 0.0), q2, _CONTRACT_LAST,
                                         precision=lax.Precision.HIGHEST, preferred_element_type=F32)
            else:
                scores = jnp.zeros((nb, tq), F32)
            pen = jnp.where(blk == t, 0.0, _block_rank_penalty(scores, t))
            pen_rows = jnp.concatenate([pad, pen, tail] if hh == 0 else [pen, tail, pad], axis=0)
            out.append(jnp.where(own_rows, q_t, pen_rows).astype(BF16))
        return out

    def steps_of(t):
        pairs = [(c, 2) for c in range(t - 1, -1, -2)]
        return pairs + ([(0, 1)] if t % 2 == 0 else [])

    schedule = [(t, c, n) for t in range(nq) for c, n in steps_of(t)]
    q_ext = {}

    def logits(t, c, n):
        if t not in q_ext:
            build_block(t)
            q_ext.clear()
            q_ext[t] = extended_queries(t)
        out = []
        for hh in range(PAIR):
            s = jnp.dot(kext_scr[hh, c * tq:(c + n) * tq, :], q_ext[t][hh], preferred_element_type=F32)
            blocks = [s[i * tq:(i + 1) * tq, :] + bias_ref[hh, min(t - c - i, 2)] for i in range(n)]
            top = functools.reduce(jnp.maximum, [jnp.max(blk, axis=0, keepdims=True) for blk in blocks])
            out.append((blocks, top))
        return out

    def softmax_pv(c, n, s_pair, first):
        for hh in range(PAIR):
            blocks, top = s_pair[hh]
            m_new = top if first else jnp.maximum(m_scr[hh], top)
            probs = jnp.concatenate([jnp.exp2((blk - m_new).astype(BF16)) for blk in blocks], axis=0)
            pv = jnp.dot(vext_scr[hh, :, c * tq:(c + n) * tq], probs, preferred_element_type=F32)
            acc_scr[hh] = pv if first else jnp.exp2(m_scr[hh] - m_new) * acc_scr[hh] + pv
            m_scr[hh] = m_new

    def sum_page_block(j):
        pages = page_refs[j * pages_per_block:(j + 1) * pages_per_block]
        for r0 in range(0, ksum_ref.shape[2], HEAD_DIM):
            rows = slice(r0, r0 + HEAD_DIM)
            total = functools.reduce(jnp.add, [pg[0, rows, :] for pg in pages])
            ksum_ref[0, 0, rows, j:j + 1] = jnp.sum(total, axis=-1, keepdims=True)

    n_page_blocks = n_step_pages // pages_per_block
    ksum_ref[...] = jnp.zeros(ksum_ref.shape, F32)
    per_step = -(-n_page_blocks // len(schedule))

    s_cur = logits(*schedule[0])
    for idx, (t, c, n) in enumerate(schedule):
        s_next = logits(*schedule[idx + 1]) if idx + 1 < len(schedule) else None
        for j in range(idx * per_step, min((idx + 1) * per_step, n_page_blocks)):
            sum_page_block(j)
        softmax_pv(c, n, s_cur, first=(c + n == t + 1))
        if c == 0:
            numer = jnp.where(row < HEAD_DIM, acc_scr[0], acc_scr[1])
            denom = jnp.where(row < HEAD_DIM, pltpu.roll(acc_scr[0], HEAD_DIM, 0),
                              pltpu.roll(acc_scr[1], HEAD_DIM, 0))
            rows = slice(t * tq, (t + 1) * tq)
            o_ref[rows, :] = (jnp.transpose(numer / denom) * _silu(gate_ref[rows, :])).astype(o_ref.dtype)
        s_cur = s_next


def _toeplitz(vec, rows, cols):
    h, n = vec.shape
    assert cols <= n - 1 and rows <= n
    flat = jnp.tile(vec, (1, rows))[:, :rows * (n - 1)]
    return flat.reshape(h, rows, n - 1)[:, :, :cols]


def _moba_bias_tiles(rel_bias):
    blk = MOBA_BLOCK
    bias_t = rel_bias.astype(F32).T
    tiles = []
    n = 2 * blk - 1
    k = jnp.arange(n)
    col_minus_row = jnp.where(k < blk, k, k - n)
    for dist in range(3):
        d = dist * blk + col_minus_row
        g = bias_t[:, _rel_bucket(d)]
        if dist == 0:
            g = jnp.where((d >= 0)[None], g, -jnp.inf)
        tiles.append(_toeplitz(g, blk, blk))
    return jnp.stack(tiles, axis=1) * LOG2E


def _moba_prompt(mq, kt_buf, vt_buf, kmean, mgate, bias_tiles, pt_flat, cache_pages, *, layer, batch, seq,
                 page0, n_pages):
    mw = mq.shape[1]
    npair = mw // LANES
    nq = seq // MOBA_BLOCK
    nb = kmean.shape[1]
    assert seq % MOBA_BLOCK == 0 and nb % 8 == 0 and 2 <= nq <= nb <= HEAD_DIM
    assert MOBA_BLOCK + 1 >= MAX_DISTANCE
    width, page_size = cache_pages.shape[1:]
    pages_per_block = MOBA_BLOCK // page_size
    steps = batch * npair
    step_pages = pt_flat.shape[0] // steps
    assert pt_flat.shape[0] % steps == 0 and n_pages % step_pages == 0 and step_pages % pages_per_block == 0
    assert step_pages // pages_per_block <= LANES and page_size == LANES
    parts = n_pages // step_pages
    step_of = lambda b, p: b * npair + p
    rows = lambda: pl.BlockSpec((seq, LANES), lambda b, p, pt: (b, p))
    cols = lambda: pl.BlockSpec((1, 1, LANES, seq), lambda b, p, pt: (layer, b, p, 0))

    def page_spec(i):
        return pl.BlockSpec((1, width, page_size),
                            lambda b, p, pt: (page0 + pt[step_of(b, p) * step_pages + i], 0, 0))

    grid_spec = pltpu.PrefetchScalarGridSpec(
        num_scalar_prefetch=1,
        grid=(batch, npair),
        in_specs=[rows(), cols(), cols(),
                  pl.BlockSpec((1, nb, LANES), lambda b, p, pt: (b, 0, p)),
                  rows(),
                  pl.BlockSpec((PAIR, 3, MOBA_BLOCK, MOBA_BLOCK), lambda b, p, pt: (p, 0, 0, 0))]
                 + [page_spec(i) for i in range(step_pages)],
        out_specs=[rows(),
                   pl.BlockSpec((1, 1, width, LANES),
                                lambda b, p, pt: (step_of(b, p) // parts, step_of(b, p) % parts, 0, 0))],
        scratch_shapes=[pltpu.VMEM((PAIR, seq, LANES), BF16),
                        pltpu.VMEM((PAIR, LANES, seq), BF16),
                        pltpu.VMEM((PAIR, 1, MOBA_BLOCK), F32),
                        pltpu.VMEM((PAIR, LANES, MOBA_BLOCK), F32)],
    )
    return pl.pallas_call(
        functools.partial(_moba_kernel, nb=nb, n_step_pages=step_pages, pages_per_block=pages_per_block),
        grid_spec=grid_spec,
        out_shape=[jax.ShapeDtypeStruct((batch * seq, mw), MIX_DTYPE),
                   jax.ShapeDtypeStruct((pt_flat.shape[0] // n_pages, parts, width, LANES), F32)],
        compiler_params=_params("parallel", "parallel"),
        name="moba_prompt",
    )(pt_flat, mq, kt_buf, vt_buf, kmean, mgate, bias_tiles, *([cache_pages] * step_pages))


def _block_score_kernel(q_ref, ksum_ref, s_ref, *, heads, blocks_per_part):
    lane = lax.broadcasted_iota(jnp.int32, (1, LANES), 1)
    head_row = lax.broadcasted_iota(jnp.int32, (heads, 1), 0)
    scores = jnp.zeros((heads, LANES), F32)
    for r in range(ksum_ref.shape[1]):
        in_part = (lane >= r * blocks_per_part) & (lane < (r + 1) * blocks_per_part)
        for h in range(heads):
            rows = slice(h * HEAD_DIM, (h + 1) * HEAD_DIM)
            part = jnp.sum(ksum_ref[0, r, rows, :] * q_ref[0, rows, :], axis=0, keepdims=True)
            if r:
                part = pltpu.roll(part, r * blocks_per_part, 1)
            scores = jnp.where((head_row == h) & in_part, part, scores)
    s_ref[0] = scores * (1.0 / MOBA_BLOCK)


def _block_scores(q_rep, ksum, *, blocks_per_part):
    db, width, _ = q_rep.shape
    heads = width // HEAD_DIM
    parts = ksum.shape[1]
    assert parts * blocks_per_part <= LANES
    return pl.pallas_call(
        functools.partial(_block_score_kernel, heads=heads, blocks_per_part=blocks_per_part),
        grid=(db,),
        in_specs=[pl.BlockSpec((1, width, LANES), lambda b: (b, 0, 0)),
                  pl.BlockSpec((1, parts, width, LANES), lambda b: (b, 0, 0, 0))],
        out_specs=pl.BlockSpec((1, heads, LANES), lambda b: (b, 0, 0)),
        out_shape=jax.ShapeDtypeStruct((db, heads, LANES), F32),
        compiler_params=_params("parallel"),
        name="block_scores",
    )(q_rep, ksum)


def _topk_kernel(s_ref, sel_ref, *, n_blocks):
    s = s_ref[...]
    lane = lax.broadcasted_iota(jnp.int32, s.shape, 1).astype(F32)
    s = jnp.where(lane < n_blocks, s, -jnp.inf)
    sel = jnp.zeros(s.shape, F32)
    for j in range(TOP_K):
        best = jnp.max(s, axis=-1, keepdims=True)
        idx = jnp.min(jnp.where(s == best, lane, float(LANES)), axis=-1, keepdims=True)
        sel = jnp.where(lane == j, idx, sel)
        s = jnp.where(lane == idx, -jnp.inf, s)
    sel_ref[...] = sel.astype(jnp.int32)


def _topk_blocks(scores, *, n_blocks):
    db, heads, _ = scores.shape
    assert n_blocks >= TOP_K
    spec = pl.BlockSpec((db * heads, LANES), lambda i: (0, 0))
    return pl.pallas_call(
        functools.partial(_topk_kernel, n_blocks=n_blocks),
        grid=(1,),
        in_specs=[spec],
        out_specs=spec,
        out_shape=jax.ShapeDtypeStruct((db * heads, LANES), jnp.int32),
        compiler_params=_params("arbitrary"),
        name="topk_blocks",
    )(scores.reshape(db * heads, LANES)).reshape(db, heads, LANES)


ATTEND_HEADS = 4


def _page_attend_kernel(pt_ref, seqp_ref, q_ref, kn_ref, vn_ref, gate_ref, bown_ref, bias_ref, *refs,
                        heads, n_sel_pages):
    n_tiles = ATTEND_HEADS * n_sel_pages
    k_refs, v_refs, o_ref = refs[:n_tiles], refs[n_tiles:2 * n_tiles], refs[2 * n_tiles]
    b = pl.program_id(0)
    h0 = pl.program_id(1) * ATTEND_HEADS
    diag = (lax.broadcasted_iota(jnp.int32, (HEAD_DIM, LANES), 0)
            == lax.broadcasted_iota(jnp.int32, (HEAD_DIM, LANES), 1))
    for i in range(ATTEND_HEADS):
        h = h0 + i
        dims = slice(i * HEAD_DIM, (i + 1) * HEAD_DIM)
        q = q_ref[0, dims, :] * (HEAD_DIM ** -0.5)
        logits = []
        for j in range(n_sel_pages):
            seq_page = seqp_ref[(b * heads + h) * n_sel_pages + j]
            bias = bias_ref[h, pl.ds(seq_page, 1), :]
            logits.append(jnp.sum(k_refs[i * n_sel_pages + j][0] * q, axis=0, keepdims=True) + bias)
        s_own = jnp.sum(kn_ref[0, dims, :] * q, axis=0, keepdims=True) + bown_ref[pl.ds(h, 1), :]
        top = functools.reduce(jnp.maximum, logits)
        m = jnp.maximum(jnp.max(top, axis=-1, keepdims=True), s_own)
        p_own = jnp.exp(s_own - m)
        psum = jnp.zeros_like(m)
        acc = jnp.zeros((HEAD_DIM, LANES), F32)
        for j in range(n_sel_pages):
            p = jnp.exp(logits[j] - m)
            psum += p
            acc += v_refs[i * n_sel_pages + j][0] * p
        denom = jnp.sum(psum, axis=-1, keepdims=True) + p_own
        out_rep = (jnp.sum(acc, axis=-1, keepdims=True) + p_own * vn_ref[0, dims, :]) / denom
        out_row = jnp.sum(jnp.where(diag, out_rep, 0.0), axis=0, keepdims=True)[:, :HEAD_DIM]
        o_ref[0, pl.ds(h, 1), :] = out_row * _silu(gate_ref[0, pl.ds(h, 1), :])


def _page_attend(pt_flat, seq_pages, q_rep, kn_rep, vn_rep, gate8, bown, bias_rows, k_tiles, v_tiles,
                 *, n_sel_pages, n_pages, page0):
    db, heads, _ = gate8.shape
    page_size = k_tiles.shape[-1]
    assert heads % ATTEND_HEADS == 0
    head_rep = lambda: pl.BlockSpec((1, ATTEND_HEADS * HEAD_DIM, LANES), lambda b, g, pt, sp: (b, g, 0))
    whole = lambda a: pl.BlockSpec(a.shape, lambda b, g, pt, sp: (0,) * a.ndim)

    def tile_spec(i, j):
        def index(b, g, pt, sp):
            h = g * ATTEND_HEADS + i
            page = page0 + pt[b * n_pages + sp[(b * heads + h) * n_sel_pages + j]]
            return (page * heads + h, 0, 0)
        return pl.BlockSpec((1, HEAD_DIM, page_size), index)

    tile_specs = [tile_spec(i, j) for i in range(ATTEND_HEADS) for j in range(n_sel_pages)]
    grid_spec = pltpu.PrefetchScalarGridSpec(
        num_scalar_prefetch=2,
        grid=(db, heads // ATTEND_HEADS),
        in_specs=[head_rep(), head_rep(), head_rep(),
                  pl.BlockSpec((1, heads, HEAD_DIM), lambda b, g, pt, sp: (b, 0, 0)),
                  whole(bown), whole(bias_rows)] + tile_specs * 2,
        out_specs=pl.BlockSpec((1, heads, HEAD_DIM), lambda b, g, pt, sp: (b, 0, 0)),
    )
    return pl.pallas_call(
        functools.partial(_page_attend_kernel, heads=heads, n_sel_pages=n_sel_pages),
        grid_spec=grid_spec,
        out_shape=jax.ShapeDtypeStruct((db, heads, HEAD_DIM), F32),
        compiler_params=_params("parallel", "arbitrary"),
        name="page_attend",
    )(pt_flat, seq_pages, q_rep, kn_rep, vn_rep, gate8, bown, bias_rows,
      *([k_tiles] * len(tile_specs)), *([v_tiles] * len(tile_specs)))


def _sample_bias_rows(rel_bias, *, past_len, page_size):
    kpos = jnp.arange(past_len)
    onehot = (_rel_bucket(past_len - kpos)[:, None] == jnp.arange(N_BUCKETS)[None, :]).astype(F32)
    bias = jnp.dot(onehot, rel_bias.astype(F32), precision=lax.Precision.HIGHEST)
    return bias.T.reshape(rel_bias.shape[1], past_len // page_size, page_size)


def _lane_replicated(x):
    return jnp.broadcast_to(x[:, :, None], x.shape + (LANES,))


def _moba_sample(mq, mk, mv, mgate, ksum, k_cache_t, v_cache_t, page0, page_table, rel_bias, bias_rows, *,
                 page_size):
    db, mw = mq.shape
    heads = mw // HEAD_DIM
    n_pages = page_table.shape[1]
    past_len = n_pages * page_size
    pages_per_block = MOBA_BLOCK // page_size
    n_blocks = past_len // MOBA_BLOCK
    assert past_len % MOBA_BLOCK == 0
    n_rows = k_cache_t.shape[0]
    pt_flat = page_table.reshape(-1)
    q_rep = _lane_replicated(mq)
    scores = _block_scores(q_rep, ksum, blocks_per_part=n_blocks // ksum.shape[1])
    sel = _topk_blocks(scores, n_blocks=n_blocks)[:, :, :TOP_K]
    seq_pages = (sel[..., None] * pages_per_block + jnp.arange(pages_per_block, dtype=jnp.int32)).reshape(-1)
    bown = jnp.broadcast_to(rel_bias.astype(F32)[0][:, None], (heads, LANES))
    out = _page_attend(pt_flat, seq_pages, q_rep, _lane_replicated(mk), _lane_replicated(mv),
                       mgate.reshape(db, heads, HEAD_DIM), bown, bias_rows,
                       k_cache_t.reshape(n_rows * heads, HEAD_DIM, page_size),
                       v_cache_t.reshape(n_rows * heads, HEAD_DIM, page_size),
                       n_sel_pages=TOP_K * pages_per_block, n_pages=n_pages, page0=page0)
    return out.reshape(db, mw)


PROMPT_TM = 512
OUTPROJ_TM = 1024
CONV_TS = 512


def kernel(x_prompt, x_sample, cache_k, cache_v, page_table, state_ret, state_conv, norm_g, w_in, conv_w,
           conv_b, conv_ln_g, conv_ln_b, w_out, rel_bias, final_g):
    batch, seq, d = x_prompt.shape
    db, dec_seq, _ = x_sample.shape
    assert dec_seq == 1, "the sample path handles one new token per sequence"
    depth = w_in.shape[0]
    ret_heads = d // 256
    moba_heads = d // 128
    conv_ch = d // 4
    n_pages = page_table.shape[1]
    page_size = cache_k.shape[2]
    past_len = n_pages * page_size
    n_phys = cache_k.shape[1]
    tokens_minor = lambda c: jnp.transpose(c, (0, 1, 3, 4, 2)).reshape(depth * n_phys, moba_heads, HEAD_DIM,
                                                                         page_size)
    k_cache_t, v_cache_t = tokens_minor(cache_k), tokens_minor(cache_v)

    mw = moba_heads * HEAD_DIM
    k0 = 3 * conv_ch + 4 * ret_heads * HEAD_DIM + mw
    w_rows = jnp.concatenate([w_in[:, :, :k0], w_in[:, :, k0 + 2 * mw:]], axis=-1).astype(BF16)
    w_kv_t = jnp.swapaxes(w_in[:, :, k0:k0 + 2 * mw], 1, 2).astype(BF16)
    w_in_b = w_in.astype(BF16)
    w_out_b = w_out.astype(BF16)
    cos_p, sin_p = _rope_tables(jnp.arange(seq, dtype=jnp.int32), ret_heads)
    cos_s, sin_s = _rope_tables(past_len + jnp.arange(1, dtype=jnp.int32), ret_heads)
    ret_tables = _ret_tables(ret_heads)
    lg = jnp.log(1.0 - 2.0 ** (-5.0 - jnp.arange(ret_heads, dtype=F32)))
    decay_rows = jnp.broadcast_to(jnp.exp(lg)[:, None], (ret_heads, HEAD_DIM))
    bias_tiles = _moba_bias_tiles(rel_bias)
    bias_rows = _sample_bias_rows(rel_bias, past_len=past_len, page_size=page_size)
    conv0 = jnp.zeros((batch, CONV_WIDTH - 1, conv_ch), F32)
    ret0 = jnp.zeros((batch, ret_heads // PAIR, LANES, LANES), F32)
    nb = seq // MOBA_BLOCK
    nb_pad = -(-nb // 8) * 8

    hp = x_prompt.reshape(batch * seq, d)
    hs = x_sample.reshape(db, d)
    outs = {name: [] for name in ("ks", "vs", "rp", "rs", "cp", "cs")}
    kt_buf = vt_buf = None
    for l in range(depth):
        last = l == depth - 1
        conv_in, ret_in, mq, mgate, kt_buf, vt_buf, kmean = _inproj_prompt(
            hp, norm_g[l], w_rows[l], w_kv_t[l], kt_buf, vt_buf, layer=l, depth=depth, batch=batch,
            seq=seq, tm=PROMPT_TM)
        conv_out, conv_state = _conv_prompt(conv_in, conv0, conv_w[l], conv_b[l], conv_ln_g[l], conv_ln_b[l],
                                            batch=batch, seq=seq, ts=CONV_TS)
        ret_out, ret_state = _ret_prompt(ret_in, ret0, cos_p, sin_p, ret_tables, batch=batch, seq=seq)
        kmean = jnp.pad(kmean, ((0, 0), (0, nb_pad - nb), (0, 0)))
        moba_out, ksum = _moba_prompt(mq, kt_buf, vt_buf, kmean, mgate, bias_tiles, page_table.reshape(-1),
                                      k_cache_t.reshape(depth * n_phys, mw, page_size), layer=l, batch=batch,
                                      seq=seq, page0=l * n_phys, n_pages=n_pages)
        hp = _outproj(hp, conv_out, ret_out, moba_out, w_out_b[l], final_g, tm=math.gcd(OUTPROJ_TM, batch * seq),
                      final_norm=last)
        outs["rp"].append(_unpair_states(ret_state))
        outs["cp"].append(conv_state)
        conv_in, ret_in, mq, mk, mv, mgate = _inproj(hs, norm_g[l], w_in_b[l])
        conv_out, conv_state = _conv_sample(conv_in, state_conv[l], conv_w[l], conv_b[l], conv_ln_g[l],
                                            conv_ln_b[l])
        ret_out, ret_state = _ret_sample(ret_in, state_ret[l], cos_s, sin_s, decay_rows)
        moba_out = _moba_sample(mq, mk, mv, mgate, ksum, k_cache_t, v_cache_t, l * n_phys, page_table, rel_bias,
                                bias_rows, page_size=page_size)
        hs = _outproj(hs, conv_out, ret_out, moba_out, w_out_b[l], final_g, tm=db, final_norm=last)
        outs["ks"].append(mk.reshape(db, 1, moba_heads, HEAD_DIM))
        outs["vs"].append(mv.reshape(db, 1, moba_heads, HEAD_DIM))
        outs["rs"].append(ret_state)
        outs["cs"].append(conv_state)

    st = {name: jnp.stack(vals) for name, vals in outs.items()}
    rows_major = lambda t: jnp.transpose(t.reshape(depth, batch, moba_heads, HEAD_DIM, seq), (0, 1, 4, 2, 3))
    return (hp.reshape(batch, seq, d), hs.reshape(db, 1, d), rows_major(kt_buf), rows_major(vt_buf),
            st["ks"], st["vs"], st["rp"], st["rs"], st["cp"], st["cs"])
```

```python
import functools
import math

import jax
import jax.numpy as jnp
from jax import lax
from jax.experimental import pallas as pl
from jax.experimental.pallas import tpu as pltpu

F32 = jnp.float32
BF16 = jnp.bfloat16
MIX_DTYPE = BF16

HEAD_DIM = 64
CONV_WIDTH = 31
MOBA_BLOCK = 256
TOP_K = 3
N_BUCKETS = 32
MAX_DISTANCE = 128
RET_CHUNK = 128
ROPE_BASE = 10000.0
EPS = 1e-6

LANES = 128
SUBLANES = 8
PAIR = LANES // HEAD_DIM
MASKED = -1e30
LOG2E = math.log2(math.e)
VMEM_LIMIT = 56 * 1024 * 1024
_CONTRACT_LAST = (((1,), (1,)), ((), ()))
_CONTRACT_FIRST = (((0,), (0,)), ((), ()))


def _silu(x):
    return x * jax.nn.sigmoid(x)


def _params(*sem):
    return pltpu.CompilerParams(dimension_semantics=sem, vmem_limit_bytes=VMEM_LIMIT)


def _inproj_kernel(x_ref, g_ref, w_ref, *out_refs, splits):
    x = x_ref[...]
    h = x * lax.rsqrt(jnp.mean(x * x, axis=-1, keepdims=True) + EPS) * g_ref[...]
    hb = h.astype(BF16)
    for (c0, c1), o_ref in zip(splits, out_refs):
        o_ref[...] = jnp.dot(hb, w_ref[:, c0:c1], preferred_element_type=F32)


def _inproj(x2d, g, w_bf16):
    m, d = x2d.shape
    conv_ch, ret_w, moba_w = d // 4, (d // 256) * HEAD_DIM, (d // 128) * HEAD_DIM
    widths = (3 * conv_ch, 4 * ret_w, moba_w, moba_w, moba_w, moba_w)
    edges = [0]
    for wd in widths:
        edges.append(edges[-1] + wd)
    splits = tuple(zip(edges[:-1], edges[1:]))
    assert edges[-1] == w_bf16.shape[1]
    return pl.pallas_call(
        functools.partial(_inproj_kernel, splits=splits),
        grid=(1,),
        in_specs=[pl.BlockSpec((m, d), lambda i: (0, 0)),
                  pl.BlockSpec((1, d), lambda i: (0, 0)),
                  pl.BlockSpec(w_bf16.shape, lambda i: (0, 0))],
        out_specs=[pl.BlockSpec((m, wd), lambda i: (0, 0)) for wd in widths],
        out_shape=[jax.ShapeDtypeStruct((m, wd), F32) for wd in widths],
        compiler_params=_params("arbitrary"),
        name="inproj_sample",
    )(x2d, g.reshape(1, d), w_bf16)


def _inproj_prompt_kernel(x_ref, g_ref, w_ref, wkv_ref, *refs, splits, n_mean, aliased):
    if aliased:
        refs = refs[2:]
    conv_ref, ret_ref, q_ref, gate_ref, kt_ref, vt_ref, km_ref = refs
    x = x_ref[...]
    h = x * lax.rsqrt(jnp.mean(x * x, axis=-1, keepdims=True) + EPS) * g_ref[...]
    hb = h.astype(BF16)
    for (c0, c1), o_ref in zip(splits, (conv_ref, ret_ref, q_ref, gate_ref)):
        o_ref[...] = jnp.dot(hb, w_ref[:, c0:c1], preferred_element_type=F32)
    mw = kt_ref.shape[2]
    kt = lax.dot_general(wkv_ref[0:mw, :], hb, _CONTRACT_LAST, preferred_element_type=F32)
    kt_ref[0, 0] = kt
    vt_ref[0, 0] = lax.dot_general(wkv_ref[mw:2 * mw, :], hb, _CONTRACT_LAST, preferred_element_type=F32)
    for i in range(n_mean):
        blk = kt[:, i * MOBA_BLOCK:(i + 1) * MOBA_BLOCK]
        km_ref[0, :, i:i + 1] = jnp.sum(blk, axis=-1, keepdims=True) * (1.0 / MOBA_BLOCK)


def _inproj_prompt(x2d, g, w_rows, w_kv_t, kt_buf, vt_buf, *, layer, depth, batch, seq, tm):
    m, d = x2d.shape
    conv_ch, ret_w, mw = d // 4, (d // 256) * HEAD_DIM, (d // 128) * HEAD_DIM
    widths = (3 * conv_ch, 4 * ret_w, mw, mw)
    edges = [0]
    for wd in widths:
        edges.append(edges[-1] + wd)
    splits = tuple(zip(edges[:-1], edges[1:]))
    assert edges[-1] == w_rows.shape[1] and seq % tm == 0 and tm % MOBA_BLOCK == 0
    n_mean = tm // MOBA_BLOCK
    assert n_mean <= 8
    per_seq = seq // tm
    aliased = kt_buf is not None
    row = lambda wd: pl.BlockSpec((tm, wd), lambda i: (i, 0))
    const = lambda a: pl.BlockSpec(a.shape, lambda i: (0,) * a.ndim)
    kv_spec = pl.BlockSpec((1, 1, mw, tm), lambda i: (layer, i // per_seq, 0, i % per_seq))
    kv_shape = jax.ShapeDtypeStruct((depth, batch, mw, seq), F32)
    in_specs = [row(d), pl.BlockSpec((1, d), lambda i: (0, 0)), const(w_rows), const(w_kv_t)]
    args = [x2d, g.reshape(1, d), w_rows, w_kv_t]
    aliases = {}
    if aliased:
        in_specs += [pl.BlockSpec(memory_space=pl.ANY)] * 2
        args += [kt_buf, vt_buf]
        aliases = {4: 4, 5: 5}
    *outs, km_t = pl.pallas_call(
        functools.partial(_inproj_prompt_kernel, splits=splits, n_mean=n_mean, aliased=aliased),
        grid=(m // tm,),
        in_specs=in_specs,
        out_specs=[row(wd) for wd in widths] + [kv_spec, kv_spec,
                                                pl.BlockSpec((1, mw, n_mean), lambda i: (i, 0, 0))],
        out_shape=[jax.ShapeDtypeStruct((m, wd), F32) for wd in widths]
                  + [kv_shape, kv_shape, jax.ShapeDtypeStruct((m // tm, mw, n_mean), F32)],
        input_output_aliases=aliases,
        compiler_params=_params("parallel"),
        name="inproj_prompt",
    )(*args)
    kmean = jnp.transpose(km_t.reshape(batch, per_seq, mw, n_mean), (0, 1, 3, 2)).reshape(batch, -1, mw)
    return (*outs, kmean)


def _outproj_kernel(x_ref, c_ref, r_ref, m_ref, w_ref, fg_ref, y_ref, *, conv_ch, ret_w, final_norm):
    y = x_ref[...]
    y += jnp.dot(c_ref[...].astype(BF16), w_ref[0:conv_ch, :], preferred_element_type=F32)
    y += jnp.dot(r_ref[...].astype(BF16), w_ref[conv_ch:conv_ch + ret_w, :], preferred_element_type=F32)
    y += jnp.dot(m_ref[...].astype(BF16), w_ref[conv_ch + ret_w:, :], preferred_element_type=F32)
    if final_norm:
        y = y * lax.rsqrt(jnp.mean(y * y, axis=-1, keepdims=True) + EPS) * fg_ref[...]
    y_ref[...] = y


def _outproj(x2d, conv_out, ret_out, moba_out, w_bf16, final_g, *, tm, final_norm):
    m, d = x2d.shape
    conv_ch, ret_w, moba_w = conv_out.shape[1], ret_out.shape[1], moba_out.shape[1]
    row = lambda wd: pl.BlockSpec((tm, wd), lambda i: (i, 0))
    return pl.pallas_call(
        functools.partial(_outproj_kernel, conv_ch=conv_ch, ret_w=ret_w, final_norm=final_norm),
        grid=(m // tm,),
        in_specs=[row(d), row(conv_ch), row(ret_w), row(moba_w),
                  pl.BlockSpec(w_bf16.shape, lambda i: (0, 0)),
                  pl.BlockSpec((1, d), lambda i: (0, 0))],
        out_specs=row(d),
        out_shape=jax.ShapeDtypeStruct((m, d), F32),
        compiler_params=_params("parallel"),
        name="outproj",
    )(x2d, conv_out, ret_out, moba_out, w_bf16, final_g.reshape(1, d))


def _layernorm_silu(y, g, b):
    mu = jnp.mean(y, axis=-1, keepdims=True)
    yc = y - mu
    yn = yc * lax.rsqrt(jnp.mean(yc * yc, axis=-1, keepdims=True) + EPS)
    return _silu(yn * g + b)


CONV_PAD = 32
CONV_ROWS = 64


def _conv_kernel(in_ref, prev_ref, w_ref, b_ref, g_ref, beta_ref, o_ref, st_ref, buf, shifted, *, ts, ch):
    t = pl.program_id(1)
    keep = CONV_WIDTH - 1
    lo = CONV_PAD - keep

    @pl.when(t == 0)
    def _():
        buf[lo:CONV_PAD, :] = prev_ref[0]

    @pl.when(t > 0)
    def _():
        buf[lo:CONV_PAD, :] = buf[ts + lo:ts + CONV_PAD, :]

    buf[CONV_PAD:CONV_PAD + ts, :] = in_ref[:, 0:ch] * jax.nn.sigmoid(in_ref[:, ch:2 * ch])
    used = ts + CONV_PAD - SUBLANES
    for k in range(1, SUBLANES):
        shifted[k - 1, 0:used, :] = buf[k:k + used, :]
    for r0 in range(0, ts, CONV_ROWS):
        acc = jnp.zeros((CONV_ROWS, ch), F32)
        for j in range(CONV_WIDTH):
            k = (lo + j) % SUBLANES
            base = r0 + lo + j - k
            rows = buf[base:base + CONV_ROWS, :] if k == 0 else shifted[k - 1, base:base + CONV_ROWS, :]
            acc += rows * w_ref[j:j + 1, :]
        y = _layernorm_silu(acc + b_ref[...], g_ref[...], beta_ref[...])
        gated = y * _silu(in_ref[r0:r0 + CONV_ROWS, 2 * ch:3 * ch])
        o_ref[r0:r0 + CONV_ROWS, :] = gated.astype(o_ref.dtype)

    @pl.when(t == pl.num_programs(1) - 1)
    def _():
        st_ref[0] = buf[ts + lo:ts + CONV_PAD, :]


def _conv_prompt(conv_in, prev, w, b, g, beta, *, batch, seq, ts):
    ch = conv_in.shape[1] // 3
    nts = seq // ts
    assert seq % ts == 0 and ts % CONV_ROWS == 0 and ts >= CONV_PAD
    vec = lambda: pl.BlockSpec((1, ch), lambda bi, t: (0, 0))
    return pl.pallas_call(
        functools.partial(_conv_kernel, ts=ts, ch=ch),
        grid=(batch, nts),
        in_specs=[pl.BlockSpec((ts, 3 * ch), lambda bi, t: (bi * nts + t, 0)),
                  pl.BlockSpec((1, CONV_WIDTH - 1, ch), lambda bi, t: (bi, 0, 0)),
                  pl.BlockSpec((CONV_WIDTH, ch), lambda bi, t: (0, 0)),
                  vec(), vec(), vec()],
        out_specs=[pl.BlockSpec((ts, ch), lambda bi, t: (bi * nts + t, 0)),
                   pl.BlockSpec((1, CONV_WIDTH - 1, ch), lambda bi, t: (bi, 0, 0))],
        out_shape=[jax.ShapeDtypeStruct((batch * seq, ch), MIX_DTYPE),
                   jax.ShapeDtypeStruct((batch, CONV_WIDTH - 1, ch), F32)],
        scratch_shapes=[pltpu.VMEM((ts + CONV_PAD, ch), F32),
                        pltpu.VMEM((SUBLANES - 1, ts + CONV_PAD, ch), F32)],
        compiler_params=_params("parallel", "arbitrary"),
        name="conv_prompt",
    )(conv_in, prev, w, b.reshape(1, ch), g.reshape(1, ch), beta.reshape(1, ch))


SAMPLE_SEQS = 8


def _seqs_per_step(db):
    return math.gcd(db, SAMPLE_SEQS)


def _conv_step_kernel(in_ref, prev_ref, w_ref, b_ref, g_ref, beta_ref, o_ref, st_ref, *, ch):
    keep = CONV_WIDTH - 1
    for s in range(in_ref.shape[0]):
        row = in_ref[s]
        u = row[:, 0:ch] * jax.nn.sigmoid(row[:, ch:2 * ch])
        prev = prev_ref[s]
        acc = jnp.sum(prev * w_ref[0:keep, :], axis=0, keepdims=True) + u * w_ref[keep:keep + 1, :]
        y = _layernorm_silu(acc + b_ref[...], g_ref[...], beta_ref[...])
        o_ref[s] = y * _silu(row[:, 2 * ch:3 * ch])
        st_ref[s, 0:keep - 1, :] = prev_ref[s, 1:keep, :]
        st_ref[s, keep - 1:keep, :] = u


def _conv_sample(conv_in, prev, w, b, g, beta):
    db, ch3 = conv_in.shape
    ch = ch3 // 3
    keep = CONV_WIDTH - 1
    n = _seqs_per_step(db)
    vec = lambda: pl.BlockSpec((1, ch), lambda bi: (0, 0))
    out, st = pl.pallas_call(
        functools.partial(_conv_step_kernel, ch=ch),
        grid=(db // n,),
        in_specs=[pl.BlockSpec((n, 1, ch3), lambda bi: (bi, 0, 0)),
                  pl.BlockSpec((n, keep, ch), lambda bi: (bi, 0, 0)),
                  pl.BlockSpec((CONV_WIDTH, ch), lambda bi: (0, 0)),
                  vec(), vec(), vec()],
        out_specs=[pl.BlockSpec((n, 1, ch), lambda bi: (bi, 0, 0)),
                   pl.BlockSpec((n, keep, ch), lambda bi: (bi, 0, 0))],
        out_shape=[jax.ShapeDtypeStruct((db, 1, ch), F32),
                   jax.ShapeDtypeStruct((db, keep, ch), F32)],
        compiler_params=_params("parallel"),
        name="conv_sample",
    )(conv_in.reshape(db, 1, ch3), prev, w, b.reshape(1, ch), g.reshape(1, ch), beta.reshape(1, ch))
    return out.reshape(db, ch), st


def _rope_rows(x, cos, sin_signed):
    w = x.shape[-1]
    half = HEAD_DIM // 2
    lane = lax.broadcasted_iota(jnp.int32, (1, w), 1)
    partner = jnp.where((lane % HEAD_DIM) < half, pltpu.roll(x, w - half, 1), pltpu.roll(x, half, 1))
    return x * cos + partner * sin_signed


def _half_sums(x, first_half):
    a = jnp.sum(jnp.where(first_half, x, 0.0), axis=-1, keepdims=True)
    b = jnp.sum(jnp.where(first_half, 0.0, x), axis=-1, keepdims=True)
    return jnp.where(first_half, a, b)


RET_SEQS = 2
RET_STEP_CHUNKS = 4


def _ret_kernel(in_ref, r0_ref, cos_ref, sin_ref, dmat_ref, cross_ref, wk_ref, decay_ref,
                o_ref, rout_ref, r_scr, *, rw):
    in_refs = [in_ref.at[b] for b in range(RET_SEQS)]
    o_refs = [o_ref.at[b] for b in range(RET_SEQS)]
    c = pl.program_id(1)
    npair = rw // LANES

    @pl.when(c == 0)
    def _():
        r_scr[...] = r0_ref[...]

    lane = lax.broadcasted_iota(jnp.int32, (1, LANES), 1)
    first = lane < HEAD_DIM
    row = lax.broadcasted_iota(jnp.int32, (LANES, 1), 0)
    blockdiag = (row < HEAD_DIM) == first
    problems = [(b, p) for b in range(RET_SEQS) for p in range(npair)]
    lanes_of = lambda p: slice(p * LANES, (p + 1) * LANES)
    for ci in range(in_ref.shape[1] // RET_CHUNK):
        rows = slice(ci * RET_CHUNK, (ci + 1) * RET_CHUNK)
        cos, sin = cos_ref[rows, :], sin_ref[rows, :]
        q = [_rope_rows(r[rows, 0:rw], cos, sin) for r in in_refs]
        k = [_rope_rows(r[rows, rw:2 * rw], cos, sin) * (HEAD_DIM ** -0.5) for r in in_refs]
        qb = {(b, p): q[b][:, lanes_of(p)] for b, p in problems}
        kp = {(b, p): k[b][:, lanes_of(p)] for b, p in problems}
        vb = {(b, p): in_refs[b][rows, 2 * rw + p * LANES:2 * rw + (p + 1) * LANES].astype(BF16)
              for b, p in problems}
        kb = {pr: kp[pr].astype(BF16) for pr in problems}
        carry = {(b, p): jnp.dot(qb[b, p].astype(BF16), r_scr[b, p].astype(BF16), preferred_element_type=F32)
                 for b, p in problems}
        scores = {}
        for b, p in problems:
            for hh in range(PAIR):
                own = first if hh == 0 else jnp.logical_not(first)
                qh = jnp.where(own, qb[b, p], 0.0).astype(BF16)
                scores[b, p, hh] = lax.dot_general(qh, kb[b, p], _CONTRACT_LAST, preferred_element_type=F32)
        kv = {(b, p): lax.dot_general((kp[b, p] * wk_ref[p]).astype(BF16), vb[b, p], _CONTRACT_FIRST,
                                      preferred_element_type=F32) for b, p in problems}
        inner = {(b, p, hh): jnp.dot((scores[b, p, hh] * dmat_ref[PAIR * p + hh]).astype(BF16), vb[b, p],
                                     preferred_element_type=F32)
                 for b, p in problems for hh in range(PAIR)}
        for b, p in problems:
            o = carry[b, p] * cross_ref[p] + jnp.where(first, inner[b, p, 0], inner[b, p, 1])
            r_scr[b, p] = r_scr[b, p] * decay_ref[p] + jnp.where(blockdiag, kv[b, p], 0.0)
            ms = _half_sums(o * o, first) * (1.0 / HEAD_DIM)
            gate = in_refs[b][rows, 3 * rw + p * LANES:3 * rw + (p + 1) * LANES]
            o_refs[b][rows, lanes_of(p)] = (o * lax.rsqrt(ms + EPS) * _silu(gate)).astype(o_ref.dtype)

    @pl.when(c == pl.num_programs(1) - 1)
    def _():
        rout_ref[...] = r_scr[...]


def _ret_tables(heads):
    c = RET_CHUNK
    lg = jnp.log(1.0 - 2.0 ** (-5.0 - jnp.arange(heads, dtype=F32)))
    t = jnp.arange(c, dtype=F32)
    diff = t[:, None] - t[None, :]
    dmat = jnp.where(diff >= 0, jnp.exp(lg[:, None, None] * jnp.maximum(diff, 0.0)), 0.0)
    cross = jnp.exp(lg[None, :] * (t[:, None] + 1.0))
    wk = jnp.exp(lg[:, None] * (c - 1.0 - t[None, :]))
    chunk_decay = jnp.exp(lg * c)
    per_lane = lambda a: jnp.repeat(a, HEAD_DIM, axis=-1)
    to_pairs = lambda a: jnp.swapaxes(a.reshape(c, heads // PAIR, LANES), 0, 1)
    cross_p = to_pairs(per_lane(cross))
    wk_p = to_pairs(per_lane(wk.T))
    decay_p = to_pairs(per_lane(jnp.broadcast_to(chunk_decay[None, :], (c, heads))))
    return dmat, cross_p, wk_p, decay_p


def _unpair_states(pairs):
    a = pairs[:, :, :HEAD_DIM, :HEAD_DIM]
    b = pairs[:, :, HEAD_DIM:, HEAD_DIM:]
    return jnp.stack([a, b], axis=2).reshape(pairs.shape[0], -1, HEAD_DIM, HEAD_DIM)


def _ret_prompt(ret_in, state0_pairs, cos_t, sin_t, tables, *, batch, seq):
    rw = ret_in.shape[1] // 4
    npair = rw // LANES
    c = RET_CHUNK * RET_STEP_CHUNKS
    nc = seq // c
    assert seq % c == 0 and batch % RET_SEQS == 0
    dmat, cross_p, wk_p, decay_p = tables
    const = lambda a: pl.BlockSpec(a.shape, lambda bi, ci: (0,) * a.ndim)
    state_spec = pl.BlockSpec((RET_SEQS, npair, LANES, LANES), lambda bi, ci: (bi, 0, 0, 0))
    out, rout = pl.pallas_call(
        functools.partial(_ret_kernel, rw=rw),
        grid=(batch // RET_SEQS, nc),
        in_specs=[pl.BlockSpec((RET_SEQS, c, 4 * rw), lambda bi, ci: (bi, ci, 0)),
                  state_spec,
                  pl.BlockSpec((c, rw), lambda bi, ci: (ci, 0)),
                  pl.BlockSpec((c, rw), lambda bi, ci: (ci, 0)),
                  const(dmat), const(cross_p), const(wk_p), const(decay_p)],
        out_specs=[pl.BlockSpec((RET_SEQS, c, rw), lambda bi, ci: (bi, ci, 0)), state_spec],
        out_shape=[jax.ShapeDtypeStruct((batch, seq, rw), MIX_DTYPE),
                   jax.ShapeDtypeStruct((batch, npair, LANES, LANES), F32)],
        scratch_shapes=[pltpu.VMEM((RET_SEQS, npair, LANES, LANES), F32)],
        compiler_params=_params("parallel", "arbitrary"),
        name="ret_prompt",
    )(ret_in.reshape(batch, seq, 4 * rw), state0_pairs, cos_t, sin_t, dmat, cross_p, wk_p, decay_p)
    return out.reshape(batch * seq, rw), rout


def _ret_step_kernel(in_ref, r_ref, cos_ref, sin_ref, decay_ref, o_ref, rout_ref, *, rw):
    heads = rw // HEAD_DIM
    eye = (lax.broadcasted_iota(jnp.int32, (HEAD_DIM, HEAD_DIM), 0)
           == lax.broadcasted_iota(jnp.int32, (HEAD_DIM, HEAD_DIM), 1))
    col = lambda r: jnp.sum(jnp.where(eye, r, 0.0), axis=-1, keepdims=True)
    for s in range(in_ref.shape[0]):
        row = in_ref[s]
        q = _rope_rows(row[:, 0:rw], cos_ref[...], sin_ref[...])
        k = _rope_rows(row[:, rw:2 * rw], cos_ref[...], sin_ref[...]) * (HEAD_DIM ** -0.5)
        v = row[:, 2 * rw:3 * rw]
        gate = row[:, 3 * rw:4 * rw]
        outs = []
        for h in range(heads):
            sl = slice(h * HEAD_DIM, (h + 1) * HEAD_DIM)
            qh, kh, vh = q[:, sl], k[:, sl], v[:, sl]
            decay = decay_ref[h:h + 1, :]
            state = r_ref[s, h]
            o = jnp.sum(qh * kh, axis=-1, keepdims=True) * vh
            o = o + jnp.sum(col(qh) * state, axis=0, keepdims=True) * decay
            rout_ref[s, h] = state * decay + col(kh) * vh
            o = o * lax.rsqrt(jnp.mean(o * o, axis=-1, keepdims=True) + EPS)
            outs.append(o * _silu(gate[:, sl]))
        o_ref[s] = jnp.concatenate(outs, axis=-1)


def _ret_sample(ret_in, state, cos_row, sin_row, decay_rows):
    db, rw4 = ret_in.shape
    rw = rw4 // 4
    heads = rw // HEAD_DIM
    n = _seqs_per_step(db)
    out, rout = pl.pallas_call(
        functools.partial(_ret_step_kernel, rw=rw),
        grid=(db // n,),
        in_specs=[pl.BlockSpec((n, 1, rw4), lambda bi: (bi, 0, 0)),
                  pl.BlockSpec((n, heads, HEAD_DIM, HEAD_DIM), lambda bi: (bi, 0, 0, 0)),
                  pl.BlockSpec((1, rw), lambda bi: (0, 0)),
                  pl.BlockSpec((1, rw), lambda bi: (0, 0)),
                  pl.BlockSpec((heads, HEAD_DIM), lambda bi: (0, 0))],
        out_specs=[pl.BlockSpec((n, 1, rw), lambda bi: (bi, 0, 0)),
                   pl.BlockSpec((n, heads, HEAD_DIM, HEAD_DIM), lambda bi: (bi, 0, 0, 0))],
        out_shape=[jax.ShapeDtypeStruct((db, 1, rw), F32),
                   jax.ShapeDtypeStruct(state.shape, F32)],
        compiler_params=_params("parallel"),
        name="ret_sample",
    )(ret_in.reshape(db, 1, rw4), state, cos_row, sin_row, decay_rows)
    return out.reshape(db, rw), rout


def _rope_tables(pos, heads):
    half = HEAD_DIM // 2
    freqs = ROPE_BASE ** (-jnp.arange(half, dtype=F32) / half)
    ang = pos.astype(F32)[:, None] * freqs[None, :]
    cos, sin = jnp.cos(ang), jnp.sin(ang)
    cos_t = jnp.tile(jnp.concatenate([cos, cos], axis=-1), (1, heads))
    sin_t = jnp.tile(jnp.concatenate([-sin, sin], axis=-1), (1, heads))
    return cos_t, sin_t


def _rel_bucket(d):
    n = jnp.maximum(d, 0)
    max_exact = N_BUCKETS // 2
    large = max_exact + (jnp.log(jnp.maximum(n, 1).astype(F32) / max_exact)
                         / math.log(MAX_DISTANCE / max_exact) * (N_BUCKETS - max_exact)).astype(jnp.int32)
    large = jnp.minimum(large, N_BUCKETS - 1)
    return jnp.where(n < max_exact, n, large)


def _block_rank_penalty(scores, n_past):
    blk = lax.broadcasted_iota(jnp.int32, scores.shape, 0)
    past = blk < n_past
    if n_past <= TOP_K:
        return jnp.where(past, 0.0, MASKED)
    rank = jnp.zeros(scores.shape, F32)
    for m in range(n_past):
        other = scores[m:m + 1, :]
        beats = (other > scores) | ((other == scores) & (m < blk))
        rank += jnp.where(beats, 1.0, 0.0)
    return jnp.where(past & (rank < TOP_K), 0.0, MASKED)


def _moba_kernel(pt_ref, q_ref, kt_ref, vt_ref, km_ref, gate_ref, bias_ref, *refs, nb, n_step_pages,
                 pages_per_block):
    page_refs = refs[:n_step_pages]
    o_ref, ksum_ref, kext_scr, vext_scr, m_scr, acc_scr = refs[n_step_pages:]
    tq = MOBA_BLOCK
    nq = kt_ref.shape[-1] // tq
    lane = lax.broadcasted_iota(jnp.int32, (1, LANES), 1)
    row = lax.broadcasted_iota(jnp.int32, (LANES, 1), 0)
    km = km_ref[0]

    def build_block(c):
        cols = slice(c * tq, (c + 1) * tq)
        kt = kt_ref[0, 0, :, cols]
        vt = vt_ref[0, 0, :, cols]
        for hh in range(PAIR):
            own_rows = (row < HEAD_DIM) if hh == 0 else (row >= HEAD_DIM)
            flag_row = c + (HEAD_DIM if hh == 0 else 0)
            kext = jnp.where(own_rows, kt, jnp.where(row == flag_row, 1.0, 0.0))
            kext_scr[hh, cols, :] = jnp.transpose(kext).astype(BF16)
            vext_scr[hh, :, cols] = jnp.where(own_rows, vt, 1.0).astype(BF16)

    def extended_queries(t):
        q2 = q_ref[t * tq:(t + 1) * tq, :]
        q_t = jnp.transpose(q2) * (HEAD_DIM ** -0.5 * LOG2E)
        blk = lax.broadcasted_iota(jnp.int32, (nb, tq), 0)
        pad = jnp.zeros((HEAD_DIM, tq), F32)
        tail = jnp.zeros((HEAD_DIM - nb, tq), F32)
        out = []
        for hh in range(PAIR):
            own = (lane < HEAD_DIM) if hh == 0 else (lane >= HEAD_DIM)
            own_rows = (row < HEAD_DIM) if hh == 0 else (row >= HEAD_DIM)
            if t > TOP_K:
                scores = lax.dot_general(jnp.where(own, km, 0.0), q2, _CONTRACT_LAST,
                                         precision=lax.Precision.HIGHEST, preferred_element_type=F32)
            else:
                scores = jnp.zeros((nb, tq), F32)
            pen = jnp.where(blk == t, 0.0, _block_rank_penalty(scores, t))
            pen_rows = jnp.concatenate([pad, pen, tail] if hh == 0 else [pen, tail, pad], axis=0)
            out.append(jnp.where(own_rows, q_t, pen_rows).astype(BF16))
        return out

    def steps_of(t):
        pairs = [(c, 2) for c in range(t - 1, -1, -2)]
        return pairs + ([(0, 1)] if t % 2 == 0 else [])

    schedule = [(t, c, n) for t in range(nq) for c, n in steps_of(t)]
    q_ext = {}

    def logits(t, c, n):
        if t not in q_ext:
            build_block(t)
            q_ext.clear()
            q_ext[t] = extended_queries(t)
        out = []
        for hh in range(PAIR):
            s = jnp.dot(kext_scr[hh, c * tq:(c + n) * tq, :], q_ext[t][hh], preferred_element_type=F32)
            blocks = [s[i * tq:(i + 1) * tq, :] + bias_ref[hh, min(t - c - i, 2)] for i in range(n)]
            top = functools.reduce(jnp.maximum, [jnp.max(blk, axis=0, keepdims=True) for blk in blocks])
            out.append((blocks, top))
        return out

    def softmax_pv(c, n, s_pair, first):
        for hh in range(PAIR):
            blocks, top = s_pair[hh]
            m_new = top if first else jnp.maximum(m_scr[hh], top)
            probs = jnp.concatenate([jnp.exp2((blk - m_new).astype(BF16)) for blk in blocks], axis=0)
            pv = jnp.dot(vext_scr[hh, :, c * tq:(c + n) * tq], probs, preferred_element_type=F32)
            acc_scr[hh] = pv if first else jnp.exp2(m_scr[hh] - m_new) * acc_scr[hh] + pv
            m_scr[hh] = m_new

    def sum_page_block(j):
        pages = page_refs[j * pages_per_block:(j + 1) * pages_per_block]
        for r0 in range(0, ksum_ref.shape[2], HEAD_DIM):
            rows = slice(r0, r0 + HEAD_DIM)
            total = functools.reduce(jnp.add, [pg[0, rows, :] for pg in pages])
            ksum_ref[0, 0, rows, j:j + 1] = jnp.sum(total, axis=-1, keepdims=True)

    n_page_blocks = n_step_pages // pages_per_block
    ksum_ref[...] = jnp.zeros(ksum_ref.shape, F32)
    per_step = -(-n_page_blocks // len(schedule))

    s_cur = logits(*schedule[0])
    for idx, (t, c, n) in enumerate(schedule):
        s_next = logits(*schedule[idx + 1]) if idx + 1 < len(schedule) else None
        for j in range(idx * per_step, min((idx + 1) * per_step, n_page_blocks)):
            sum_page_block(j)
        softmax_pv(c, n, s_cur, first=(c + n == t + 1))
        if c == 0:
            numer = jnp.where(row < HEAD_DIM, acc_scr[0], acc_scr[1])
            denom = jnp.where(row < HEAD_DIM, pltpu.roll(acc_scr[0], HEAD_DIM, 0),
                              pltpu.roll(acc_scr[1], HEAD_DIM, 0))
            rows = slice(t * tq, (t + 1) * tq)
            o_ref[rows, :] = (jnp.transpose(numer / denom) * _silu(gate_ref[rows, :])).astype(o_ref.dtype)
        s_cur = s_next


def _toeplitz(vec, rows, cols):
    h, n = vec.shape
    assert cols <= n - 1 and rows <= n
    flat = jnp.tile(vec, (1, rows))[:, :rows * (n - 1)]
    return flat.reshape(h, rows, n - 1)[:, :, :cols]


def _moba_bias_tiles(rel_bias):
    blk = MOBA_BLOCK
    bias_t = rel_bias.astype(F32).T
    tiles = []
    n = 2 * blk - 1
    k = jnp.arange(n)
    col_minus_row = jnp.where(k < blk, k, k - n)
    for dist in range(3):
        d = dist * blk + col_minus_row
        g = bias_t[:, _rel_bucket(d)]
        if dist == 0:
            g = jnp.where((d >= 0)[None], g, -jnp.inf)
        tiles.append(_toeplitz(g, blk, blk))
    return jnp.stack(tiles, axis=1) * LOG2E


def _moba_prompt(mq, kt_buf, vt_buf, kmean, mgate, bias_tiles, pt_flat, cache_pages, *, layer, batch, seq,
                 page0, n_pages):
    mw = mq.shape[1]
    npair = mw // LANES
    nq = seq // MOBA_BLOCK
    nb = kmean.shape[1]
    assert seq % MOBA_BLOCK == 0 and nb % 8 == 0 and 2 <= nq <= nb <= HEAD_DIM
    assert MOBA_BLOCK + 1 >= MAX_DISTANCE
    width, page_size = cache_pages.shape[1:]
    pages_per_block = MOBA_BLOCK // page_size
    steps = batch * npair
    step_pages = pt_flat.shape[0] // steps
    assert pt_flat.shape[0] % steps == 0 and n_pages % step_pages == 0 and step_pages % pages_per_block == 0
    assert step_pages // pages_per_block <= LANES and page_size == LANES
    parts = n_pages // step_pages
    step_of = lambda b, p: b * npair + p
    rows = lambda: pl.BlockSpec((seq, LANES), lambda b, p, pt: (b, p))
    cols = lambda: pl.BlockSpec((1, 1, LANES, seq), lambda b, p, pt: (layer, b, p, 0))

    def page_spec(i):
        return pl.BlockSpec((1, width, page_size),
                            lambda b, p, pt: (page0 + pt[step_of(b, p) * step_pages + i], 0, 0))

    grid_spec = pltpu.PrefetchScalarGridSpec(
        num_scalar_prefetch=1,
        grid=(batch, npair),
        in_specs=[rows(), cols(), cols(),
                  pl.BlockSpec((1, nb, LANES), lambda b, p, pt: (b, 0, p)),
                  rows(),
                  pl.BlockSpec((PAIR, 3, MOBA_BLOCK, MOBA_BLOCK), lambda b, p, pt: (p, 0, 0, 0))]
                 + [page_spec(i) for i in range(step_pages)],
        out_specs=[rows(),
                   pl.BlockSpec((1, 1, width, LANES),
                                lambda b, p, pt: (step_of(b, p) // parts, step_of(b, p) % parts, 0, 0))],
        scratch_shapes=[pltpu.VMEM((PAIR, seq, LANES), BF16),
                        pltpu.VMEM((PAIR, LANES, seq), BF16),
                        pltpu.VMEM((PAIR, 1, MOBA_BLOCK), F32),
                        pltpu.VMEM((PAIR, LANES, MOBA_BLOCK), F32)],
    )
    return pl.pallas_call(
        functools.partial(_moba_kernel, nb=nb, n_step_pages=step_pages, pages_per_block=pages_per_block),
        grid_spec=grid_spec,
        out_shape=[jax.ShapeDtypeStruct((batch * seq, mw), MIX_DTYPE),
                   jax.ShapeDtypeStruct((pt_flat.shape[0] // n_pages, parts, width, LANES), F32)],
        compiler_params=_params("parallel", "parallel"),
        name="moba_prompt",
    )(pt_flat, mq, kt_buf, vt_buf, kmean, mgate, bias_tiles, *([cache_pages] * step_pages))


def _block_score_kernel(q_ref, ksum_ref, s_ref, *, heads, blocks_per_part):
    lane = lax.broadcasted_iota(jnp.int32, (1, LANES), 1)
    head_row = lax.broadcasted_iota(jnp.int32, (heads, 1), 0)
    for s in range(q_ref.shape[0]):
        scores = jnp.zeros((heads, LANES), F32)
        for r in range(ksum_ref.shape[1]):
            in_part = (lane >= r * blocks_per_part) & (lane < (r + 1) * blocks_per_part)
            for h in range(heads):
                rows = slice(h * HEAD_DIM, (h + 1) * HEAD_DIM)
                part = jnp.sum(ksum_ref[s, r, rows, :] * q_ref[s, rows, :], axis=0, keepdims=True)
                if r:
                    part = pltpu.roll(part, r * blocks_per_part, 1)
                scores = jnp.where((head_row == h) & in_part, part, scores)
        s_ref[s] = scores * (1.0 / MOBA_BLOCK)


def _block_scores(q_rep, ksum, *, blocks_per_part):
    db, width, _ = q_rep.shape
    heads = width // HEAD_DIM
    parts = ksum.shape[1]
    n = _seqs_per_step(db)
    assert parts * blocks_per_part <= LANES
    return pl.pallas_call(
        functools.partial(_block_score_kernel, heads=heads, blocks_per_part=blocks_per_part),
        grid=(db // n,),
        in_specs=[pl.BlockSpec((n, width, LANES), lambda b: (b, 0, 0)),
                  pl.BlockSpec((n, parts, width, LANES), lambda b: (b, 0, 0, 0))],
        out_specs=pl.BlockSpec((n, heads, LANES), lambda b: (b, 0, 0)),
        out_shape=jax.ShapeDtypeStruct((db, heads, LANES), F32),
        compiler_params=_params("parallel"),
        name="block_scores",
    )(q_rep, ksum)


def _topk_kernel(s_ref, sel_ref, *, n_blocks):
    s = s_ref[...]
    lane = lax.broadcasted_iota(jnp.int32, s.shape, 1).astype(F32)
    s = jnp.where(lane < n_blocks, s, -jnp.inf)
    sel = jnp.zeros(s.shape, F32)
    for j in range(TOP_K):
        best = jnp.max(s, axis=-1, keepdims=True)
        idx = jnp.min(jnp.where(s == best, lane, float(LANES)), axis=-1, keepdims=True)
        sel = jnp.where(lane == j, idx, sel)
        s = jnp.where(lane == idx, -jnp.inf, s)
    sel_ref[...] = sel.astype(jnp.int32)


def _topk_blocks(scores, *, n_blocks):
    db, heads, _ = scores.shape
    assert n_blocks >= TOP_K
    spec = pl.BlockSpec((db * heads, LANES), lambda i: (0, 0))
    return pl.pallas_call(
        functools.partial(_topk_kernel, n_blocks=n_blocks),
        grid=(1,),
        in_specs=[spec],
        out_specs=spec,
        out_shape=jax.ShapeDtypeStruct((db * heads, LANES), jnp.int32),
        compiler_params=_params("arbitrary"),
        name="topk_blocks",
    )(scores.reshape(db * heads, LANES)).reshape(db, heads, LANES)


ATTEND_HEADS = 4


def _page_attend_kernel(pt_ref, seqp_ref, q_ref, kn_ref, vn_ref, gate_ref, bown_ref, bias_ref, *refs,
                        heads, n_sel_pages):
    n_tiles = ATTEND_HEADS * n_sel_pages
    k_refs, v_refs, o_ref = refs[:n_tiles], refs[n_tiles:2 * n_tiles], refs[2 * n_tiles]
    b = pl.program_id(0)
    h0 = pl.program_id(1) * ATTEND_HEADS
    diag = (lax.broadcasted_iota(jnp.int32, (HEAD_DIM, LANES), 0)
            == lax.broadcasted_iota(jnp.int32, (HEAD_DIM, LANES), 1))
    for i in range(ATTEND_HEADS):
        h = h0 + i
        dims = slice(i * HEAD_DIM, (i + 1) * HEAD_DIM)
        q = q_ref[0, dims, :] * (HEAD_DIM ** -0.5)
        logits = []
        for j in range(n_sel_pages):
            seq_page = seqp_ref[(b * heads + h) * n_sel_pages + j]
            bias = bias_ref[h, pl.ds(seq_page, 1), :]
            logits.append(jnp.sum(k_refs[i * n_sel_pages + j][0] * q, axis=0, keepdims=True) + bias)
        s_own = jnp.sum(kn_ref[0, dims, :] * q, axis=0, keepdims=True) + bown_ref[pl.ds(h, 1), :]
        top = functools.reduce(jnp.maximum, logits)
        m = jnp.maximum(jnp.max(top, axis=-1, keepdims=True), s_own)
        p_own = jnp.exp(s_own - m)
        psum = jnp.zeros_like(m)
        acc = jnp.zeros((HEAD_DIM, LANES), F32)
        for j in range(n_sel_pages):
            p = jnp.exp(logits[j] - m)
            psum += p
            acc += v_refs[i * n_sel_pages + j][0] * p
        denom = jnp.sum(psum, axis=-1, keepdims=True) + p_own
        out_rep = (jnp.sum(acc, axis=-1, keepdims=True) + p_own * vn_ref[0, dims, :]) / denom
        out_row = jnp.sum(jnp.where(diag, out_rep, 0.0), axis=0, keepdims=True)[:, :HEAD_DIM]
        o_ref[0, pl.ds(h, 1), :] = out_row * _silu(gate_ref[0, pl.ds(h, 1), :])


def _page_attend(pt_flat, seq_pages, q_rep, kn_rep, vn_rep, gate8, bown, bias_rows, k_tiles, v_tiles,
                 *, n_sel_pages, n_pages, page0):
    db, heads, _ = gate8.shape
    page_size = k_tiles.shape[-1]
    assert heads % ATTEND_HEADS == 0
    head_rep = lambda: pl.BlockSpec((1, ATTEND_HEADS * HEAD_DIM, LANES), lambda b, g, pt, sp: (b, g, 0))
    whole = lambda a: pl.BlockSpec(a.shape, lambda b, g, pt, sp: (0,) * a.ndim)

    def tile_spec(i, j):
        def index(b, g, pt, sp):
            h = g * ATTEND_HEADS + i
            page = page0 + pt[b * n_pages + sp[(b * heads + h) * n_sel_pages + j]]
            return (page * heads + h, 0, 0)
        return pl.BlockSpec((1, HEAD_DIM, page_size), index)

    tile_specs = [tile_spec(i, j) for i in range(ATTEND_HEADS) for j in range(n_sel_pages)]
    grid_spec = pltpu.PrefetchScalarGridSpec(
        num_scalar_prefetch=2,
        grid=(db, heads // ATTEND_HEADS),
        in_specs=[head_rep(), head_rep(), head_rep(),
                  pl.BlockSpec((1, heads, HEAD_DIM), lambda b, g, pt, sp: (b, 0, 0)),
                  whole(bown), whole(bias_rows)] + tile_specs * 2,
        out_specs=pl.BlockSpec((1, heads, HEAD_DIM), lambda b, g, pt, sp: (b, 0, 0)),
    )
    return pl.pallas_call(
        functools.partial(_page_attend_kernel, heads=heads, n_sel_pages=n_sel_pages),
        grid_spec=grid_spec,
        out_shape=jax.ShapeDtypeStruct((db, heads, HEAD_DIM), F32),
        compiler_params=_params("parallel", "arbitrary"),
        name="page_attend",
    )(pt_flat, seq_pages, q_rep, kn_rep, vn_rep, gate8, bown, bias_rows,
      *([k_tiles] * len(tile_specs)), *([v_tiles] * len(tile_specs)))


def _sample_bias_rows(rel_bias, *, past_len, page_size):
    kpos = jnp.arange(past_len)
    onehot = (_rel_bucket(past_len - kpos)[:, None] == jnp.arange(N_BUCKETS)[None, :]).astype(F32)
    bias = jnp.dot(onehot, rel_bias.astype(F32), precision=lax.Precision.HIGHEST)
    return bias.T.reshape(rel_bias.shape[1], past_len // page_size, page_size)


def _lane_replicated(x):
    return jnp.broadcast_to(x[:, :, None], x.shape + (LANES,))


def _moba_sample(mq, mk, mv, mgate, ksum, k_cache_t, v_cache_t, page0, page_table, rel_bias, bias_rows, *,
                 page_size):
    db, mw = mq.shape
    heads = mw // HEAD_DIM
    n_pages = page_table.shape[1]
    past_len = n_pages * page_size
    pages_per_block = MOBA_BLOCK // page_size
    n_blocks = past_len // MOBA_BLOCK
    assert past_len % MOBA_BLOCK == 0
    n_rows = k_cache_t.shape[0]
    pt_flat = page_table.reshape(-1)
    q_rep = _lane_replicated(mq)
    scores = _block_scores(q_rep, ksum, blocks_per_part=n_blocks // ksum.shape[1])
    sel = _topk_blocks(scores, n_blocks=n_blocks)[:, :, :TOP_K]
    seq_pages = (sel[..., None] * pages_per_block + jnp.arange(pages_per_block, dtype=jnp.int32)).reshape(-1)
    bown = jnp.broadcast_to(rel_bias.astype(F32)[0][:, None], (heads, LANES))
    out = _page_attend(pt_flat, seq_pages, q_rep, _lane_replicated(mk), _lane_replicated(mv),
                       mgate.reshape(db, heads, HEAD_DIM), bown, bias_rows,
                       k_cache_t.reshape(n_rows * heads, HEAD_DIM, page_size),
                       v_cache_t.reshape(n_rows * heads, HEAD_DIM, page_size),
                       n_sel_pages=TOP_K * pages_per_block, n_pages=n_pages, page0=page0)
    return out.reshape(db, mw)


PROMPT_TM = 512
OUTPROJ_TM = 1024
CONV_TS = 512


def kernel(x_prompt, x_sample, cache_k, cache_v, page_table, state_ret, state_conv, norm_g, w_in, conv_w,
           conv_b, conv_ln_g, conv_ln_b, w_out, rel_bias, final_g):
    batch, seq, d = x_prompt.shape
    db, dec_seq, _ = x_sample.shape
    assert dec_seq == 1, "the sample path handles one new token per sequence"
    depth = w_in.shape[0]
    ret_heads = d // 256
    moba_heads = d // 128
    conv_ch = d // 4
    n_pages = page_table.shape[1]
    page_size = cache_k.shape[2]
    past_len = n_pages * page_size
    n_phys = cache_k.shape[1]
    tokens_minor = lambda c: jnp.transpose(c, (0, 1, 3, 4, 2)).reshape(depth * n_phys, moba_heads, HEAD_DIM,
                                                                         page_size)
    k_cache_t, v_cache_t = tokens_minor(cache_k), tokens_minor(cache_v)

    mw = moba_heads * HEAD_DIM
    k0 = 3 * conv_ch + 4 * ret_heads * HEAD_DIM + mw
    w_rows = jnp.concatenate([w_in[:, :, :k0], w_in[:, :, k0 + 2 * mw:]], axis=-1).astype(BF16)
    w_kv_t = jnp.swapaxes(w_in[:, :, k0:k0 + 2 * mw], 1, 2).astype(BF16)
    w_in_b = w_in.astype(BF16)
    w_out_b = w_out.astype(BF16)
    cos_p, sin_p = _rope_tables(jnp.arange(seq, dtype=jnp.int32), ret_heads)
    cos_s, sin_s = _rope_tables(past_len + jnp.arange(1, dtype=jnp.int32), ret_heads)
    ret_tables = _ret_tables(ret_heads)
    lg = jnp.log(1.0 - 2.0 ** (-5.0 - jnp.arange(ret_heads, dtype=F32)))
    decay_rows = jnp.broadcast_to(jnp.exp(lg)[:, None], (ret_heads, HEAD_DIM))
    bias_tiles = _moba_bias_tiles(rel_bias)
    bias_rows = _sample_bias_rows(rel_bias, past_len=past_len, page_size=page_size)
    conv0 = jnp.zeros((batch, CONV_WIDTH - 1, conv_ch), F32)
    ret0 = jnp.zeros((batch, ret_heads // PAIR, LANES, LANES), F32)
    nb = seq // MOBA_BLOCK
    nb_pad = -(-nb // 8) * 8

    hp = x_prompt.reshape(batch * seq, d)
    hs = x_sample.reshape(db, d)
    outs = {name: [] for name in ("ks", "vs", "rp", "rs", "cp", "cs")}
    kt_buf = vt_buf = None
    for l in range(depth):
        last = l == depth - 1
        conv_in, ret_in, mq, mgate, kt_buf, vt_buf, kmean = _inproj_prompt(
            hp, norm_g[l], w_rows[l], w_kv_t[l], kt_buf, vt_buf, layer=l, depth=depth, batch=batch,
            seq=seq, tm=PROMPT_TM)
        conv_out, conv_state = _conv_prompt(conv_in, conv0, conv_w[l], conv_b[l], conv_ln_g[l], conv_ln_b[l],
                                            batch=batch, seq=seq, ts=CONV_TS)
        ret_out, ret_state = _ret_prompt(ret_in, ret0, cos_p, sin_p, ret_tables, batch=batch, seq=seq)
        kmean = jnp.pad(kmean, ((0, 0), (0, nb_pad - nb), (0, 0)))
        moba_out, ksum = _moba_prompt(mq, kt_buf, vt_buf, kmean, mgate, bias_tiles, page_table.reshape(-1),
                                      k_cache_t.reshape(depth * n_phys, mw, page_size), layer=l, batch=batch,
                                      seq=seq, page0=l * n_phys, n_pages=n_pages)
        hp = _outproj(hp, conv_out, ret_out, moba_out, w_out_b[l], final_g, tm=math.gcd(OUTPROJ_TM, batch * seq),
                      final_norm=last)
        outs["rp"].append(_unpair_states(ret_state))
        outs["cp"].append(conv_state)
        conv_in, ret_in, mq, mk, mv, mgate = _inproj(hs, norm_g[l], w_in_b[l])
        conv_out, conv_state = _conv_sample(conv_in, state_conv[l], conv_w[l], conv_b[l], conv_ln_g[l],
                                            conv_ln_b[l])
        ret_out, ret_state = _ret_sample(ret_in, state_ret[l], cos_s, sin_s, decay_rows)
        moba_out = _moba_sample(mq, mk, mv, mgate, ksum, k_cache_t, v_cache_t, l * n_phys, page_table, rel_bias,
                                bias_rows, page_size=page_size)
        hs = _outproj(hs, conv_out, ret_out, moba_out, w_out_b[l], final_g, tm=db, final_norm=last)
        outs["ks"].append(mk.reshape(db, 1, moba_heads, HEAD_DIM))
        outs["vs"].append(mv.reshape(db, 1, moba_heads, HEAD_DIM))
        outs["rs"].append(ret_state)
        outs["cs"].append(conv_state)

    st = {name: jnp.stack(vals) for name, vals in outs.items()}
    rows_major = lambda t: jnp.transpose(t.reshape(depth, batch, moba_heads, HEAD_DIM, seq), (0, 1, 4, 2, 3))
    return (hp.reshape(batch, seq, d), hs.reshape(db, 1, d), rows_major(kt_buf), rows_major(vt_buf),
            st["ks"], st["vs"], st["rp"], st["rs"], st["cp"], st["cs"])
```

```python
import functools
import math

import jax
import jax.numpy as jnp
from jax import lax
from jax.experimental import pallas as pl
from jax.experimental.pallas import tpu as pltpu

F32 = jnp.float32
BF16 = jnp.bfloat16
MIX_DTYPE = BF16

HEAD_DIM = 64
CONV_WIDTH = 31
MOBA_BLOCK = 256
TOP_K = 3
N_BUCKETS = 32
MAX_DISTANCE = 128
RET_CHUNK = 128
ROPE_BASE = 10000.0
EPS = 1e-6

LANES = 128
SUBLANES = 8
PAIR = LANES // HEAD_DIM
MASKED = -1e30
LOG2E = math.log2(math.e)
VMEM_LIMIT = 56 * 1024 * 1024
_CONTRACT_LAST = (((1,), (1,)), ((), ()))
_CONTRACT_FIRST = (((0,), (0,)), ((), ()))


def _silu(x):
    return x * jax.nn.sigmoid(x)


def _params(*sem):
    return pltpu.CompilerParams(dimension_semantics=sem, vmem_limit_bytes=VMEM_LIMIT)


def _inproj_kernel(x_ref, g_ref, w_ref, *out_refs, splits):
    x = x_ref[...]
    h = x * lax.rsqrt(jnp.mean(x * x, axis=-1, keepdims=True) + EPS) * g_ref[...]
    hb = h.astype(BF16)
    for (c0, c1), o_ref in zip(splits, out_refs):
        o_ref[...] = jnp.dot(hb, w_ref[:, c0:c1], preferred_element_type=F32)


def _inproj(x2d, g, w_bf16):
    m, d = x2d.shape
    conv_ch, ret_w, moba_w = d // 4, (d // 256) * HEAD_DIM, (d // 128) * HEAD_DIM
    widths = (3 * conv_ch, 4 * ret_w, moba_w, moba_w, moba_w, moba_w)
    edges = [0]
    for wd in widths:
        edges.append(edges[-1] + wd)
    splits = tuple(zip(edges[:-1], edges[1:]))
    assert edges[-1] == w_bf16.shape[1]
    return pl.pallas_call(
        functools.partial(_inproj_kernel, splits=splits),
        grid=(1,),
        in_specs=[pl.BlockSpec((m, d), lambda i: (0, 0)),
                  pl.BlockSpec((1, d), lambda i: (0, 0)),
                  pl.BlockSpec(w_bf16.shape, lambda i: (0, 0))],
        out_specs=[pl.BlockSpec((m, wd), lambda i: (0, 0)) for wd in widths],
        out_shape=[jax.ShapeDtypeStruct((m, wd), F32) for wd in widths],
        compiler_params=_params("arbitrary"),
        name="inproj_sample",
    )(x2d, g.reshape(1, d), w_bf16)


def _inproj_prompt_kernel(x_ref, g_ref, w_ref, wkv_ref, *refs, splits, n_mean, aliased):
    if aliased:
        refs = refs[2:]
    conv_ref, ret_ref, q_ref, gate_ref, kt_ref, vt_ref, km_ref = refs
    x = x_ref[...]
    h = x * lax.rsqrt(jnp.mean(x * x, axis=-1, keepdims=True) + EPS) * g_ref[...]
    hb = h.astype(BF16)
    for (c0, c1), o_ref in zip(splits, (conv_ref, ret_ref, q_ref, gate_ref)):
        o_ref[...] = jnp.dot(hb, w_ref[:, c0:c1], preferred_element_type=F32)
    mw = kt_ref.shape[2]
    kt = lax.dot_general(wkv_ref[0:mw, :], hb, _CONTRACT_LAST, preferred_element_type=F32)
    kt_ref[0, 0] = kt
    vt_ref[0, 0] = lax.dot_general(wkv_ref[mw:2 * mw, :], hb, _CONTRACT_LAST, preferred_element_type=F32)
    for i in range(n_mean):
        blk = kt[:, i * MOBA_BLOCK:(i + 1) * MOBA_BLOCK]
        km_ref[0, :, i:i + 1] = jnp.sum(blk, axis=-1, keepdims=True) * (1.0 / MOBA_BLOCK)


def _inproj_prompt(x2d, g, w_rows, w_kv_t, kt_buf, vt_buf, *, layer, depth, batch, seq, tm):
    m, d = x2d.shape
    conv_ch, ret_w, mw = d // 4, (d // 256) * HEAD_DIM, (d // 128) * HEAD_DIM
    widths = (3 * conv_ch, 4 * ret_w, mw, mw)
    edges = [0]
    for wd in widths:
        edges.append(edges[-1] + wd)
    splits = tuple(zip(edges[:-1], edges[1:]))
    assert edges[-1] == w_rows.shape[1] and seq % tm == 0 and tm % MOBA_BLOCK == 0
    n_mean = tm // MOBA_BLOCK
    assert n_mean <= 8
    per_seq = seq // tm
    aliased = kt_buf is not None
    row = lambda wd: pl.BlockSpec((tm, wd), lambda i: (i, 0))
    const = lambda a: pl.BlockSpec(a.shape, lambda i: (0,) * a.ndim)
    kv_spec = pl.BlockSpec((1, 1, mw, tm), lambda i: (layer, i // per_seq, 0, i % per_seq))
    kv_shape = jax.ShapeDtypeStruct((depth, batch, mw, seq), F32)
    in_specs = [row(d), pl.BlockSpec((1, d), lambda i: (0, 0)), const(w_rows), const(w_kv_t)]
    args = [x2d, g.reshape(1, d), w_rows, w_kv_t]
    aliases = {}
    if aliased:
        in_specs += [pl.BlockSpec(memory_space=pl.ANY)] * 2
        args += [kt_buf, vt_buf]
        aliases = {4: 4, 5: 5}
    *outs, km_t = pl.pallas_call(
        functools.partial(_inproj_prompt_kernel, splits=splits, n_mean=n_mean, aliased=aliased),
        grid=(m // tm,),
        in_specs=in_specs,
        out_specs=[row(wd) for wd in widths] + [kv_spec, kv_spec,
                                                pl.BlockSpec((1, mw, n_mean), lambda i: (i, 0, 0))],
        out_shape=[jax.ShapeDtypeStruct((m, wd), F32) for wd in widths]
                  + [kv_shape, kv_shape, jax.ShapeDtypeStruct((m // tm, mw, n_mean), F32)],
        input_output_aliases=aliases,
        compiler_params=_params("parallel"),
        name="inproj_prompt",
    )(*args)
    kmean = jnp.transpose(km_t.reshape(batch, per_seq, mw, n_mean), (0, 1, 3, 2)).reshape(batch, -1, mw)
    return (*outs, kmean)


def _outproj_kernel(x_ref, c_ref, r_ref, m_ref, w_ref, fg_ref, y_ref, *, conv_ch, ret_w, final_norm):
    y = x_ref[...]
    y += jnp.dot(c_ref[...].astype(BF16), w_ref[0:conv_ch, :], preferred_element_type=F32)
    y += jnp.dot(r_ref[...].astype(BF16), w_ref[conv_ch:conv_ch + ret_w, :], preferred_element_type=F32)
    y += jnp.dot(m_ref[...].astype(BF16), w_ref[conv_ch + ret_w:, :], preferred_element_type=F32)
    if final_norm:
        y = y * lax.rsqrt(jnp.mean(y * y, axis=-1, keepdims=True) + EPS) * fg_ref[...]
    y_ref[...] = y


def _outproj(x2d, conv_out, ret_out, moba_out, w_bf16, final_g, *, tm, final_norm):
    m, d = x2d.shape
    conv_ch, ret_w, moba_w = conv_out.shape[1], ret_out.shape[1], moba_out.shape[1]
    row = lambda wd: pl.BlockSpec((tm, wd), lambda i: (i, 0))
    return pl.pallas_call(
        functools.partial(_outproj_kernel, conv_ch=conv_ch, ret_w=ret_w, final_norm=final_norm),
        grid=(m // tm,),
        in_specs=[row(d), row(conv_ch), row(ret_w), row(moba_w),
                  pl.BlockSpec(w_bf16.shape, lambda i: (0, 0)),
                  pl.BlockSpec((1, d), lambda i: (0, 0))],
        out_specs=row(d),
        out_shape=jax.ShapeDtypeStruct((m, d), F32),
        compiler_params=_params("parallel"),
        name="outproj",
    )(x2d, conv_out, ret_out, moba_out, w_bf16, final_g.reshape(1, d))


def _layernorm_silu(y, g, b):
    mu = jnp.mean(y, axis=-1, keepdims=True)
    yc = y - mu
    yn = yc * lax.rsqrt(jnp.mean(yc * yc, axis=-1, keepdims=True) + EPS)
    return _silu(yn * g + b)


CONV_PAD = 32
CONV_ROWS = 64


def _conv_kernel(in_ref, prev_ref, w_ref, b_ref, g_ref, beta_ref, o_ref, st_ref, buf, shifted, *, ts, ch):
    t = pl.program_id(1)
    keep = CONV_WIDTH - 1
    lo = CONV_PAD - keep

    @pl.when(t == 0)
    def _():
        buf[lo:CONV_PAD, :] = prev_ref[0]

    @pl.when(t > 0)
    def _():
        buf[lo:CONV_PAD, :] = buf[ts + lo:ts + CONV_PAD, :]

    buf[CONV_PAD:CONV_PAD + ts, :] = in_ref[:, 0:ch] * jax.nn.sigmoid(in_ref[:, ch:2 * ch])
    used = ts + CONV_PAD - SUBLANES
    for k in range(1, SUBLANES):
        shifted[k - 1, 0:used, :] = buf[k:k + used, :]
    for r0 in range(0, ts, CONV_ROWS):
        acc = jnp.zeros((CONV_ROWS, ch), F32)
        for j in range(CONV_WIDTH):
            k = (lo + j) % SUBLANES
            base = r0 + lo + j - k
            rows = buf[base:base + CONV_ROWS, :] if k == 0 else shifted[k - 1, base:base + CONV_ROWS, :]
            acc += rows * w_ref[j:j + 1, :]
        y = _layernorm_silu(acc + b_ref[...], g_ref[...], beta_ref[...])
        gated = y * _silu(in_ref[r0:r0 + CONV_ROWS, 2 * ch:3 * ch])
        o_ref[r0:r0 + CONV_ROWS, :] = gated.astype(o_ref.dtype)

    @pl.when(t == pl.num_programs(1) - 1)
    def _():
        st_ref[0] = buf[ts + lo:ts + CONV_PAD, :]


def _conv_prompt(conv_in, prev, w, b, g, beta, *, batch, seq, ts):
    ch = conv_in.shape[1] // 3
    nts = seq // ts
    assert seq % ts == 0 and ts % CONV_ROWS == 0 and ts >= CONV_PAD
    vec = lambda: pl.BlockSpec((1, ch), lambda bi, t: (0, 0))
    return pl.pallas_call(
        functools.partial(_conv_kernel, ts=ts, ch=ch),
        grid=(batch, nts),
        in_specs=[pl.BlockSpec((ts, 3 * ch), lambda bi, t: (bi * nts + t, 0)),
                  pl.BlockSpec((1, CONV_WIDTH - 1, ch), lambda bi, t: (bi, 0, 0)),
                  pl.BlockSpec((CONV_WIDTH, ch), lambda bi, t: (0, 0)),
                  vec(), vec(), vec()],
        out_specs=[pl.BlockSpec((ts, ch), lambda bi, t: (bi * nts + t, 0)),
                   pl.BlockSpec((1, CONV_WIDTH - 1, ch), lambda bi, t: (bi, 0, 0))],
        out_shape=[jax.ShapeDtypeStruct((batch * seq, ch), MIX_DTYPE),
                   jax.ShapeDtypeStruct((batch, CONV_WIDTH - 1, ch), F32)],
        scratch_shapes=[pltpu.VMEM((ts + CONV_PAD, ch), F32),
                        pltpu.VMEM((SUBLANES - 1, ts + CONV_PAD, ch), F32)],
        compiler_params=_params("parallel", "arbitrary"),
        name="conv_prompt",
    )(conv_in, prev, w, b.reshape(1, ch), g.reshape(1, ch), beta.reshape(1, ch))


SAMPLE_SEQS = 8


def _seqs_per_step(db):
    return math.gcd(db, SAMPLE_SEQS)


def _conv_step_kernel(in_ref, prev_ref, w_ref, b_ref, g_ref, beta_ref, o_ref, st_ref, *, ch):
    keep = CONV_WIDTH - 1
    for s in range(in_ref.shape[0]):
        row = in_ref[s]
        u = row[:, 0:ch] * jax.nn.sigmoid(row[:, ch:2 * ch])
        prev = prev_ref[s]
        acc = jnp.sum(prev * w_ref[0:keep, :], axis=0, keepdims=True) + u * w_ref[keep:keep + 1, :]
        y = _layernorm_silu(acc + b_ref[...], g_ref[...], beta_ref[...])
        o_ref[s] = y * _silu(row[:, 2 * ch:3 * ch])
        st_ref[s, 0:keep - 1, :] = prev_ref[s, 1:keep, :]
        st_ref[s, keep - 1:keep, :] = u


def _conv_sample(conv_in, prev, w, b, g, beta):
    db, ch3 = conv_in.shape
    ch = ch3 // 3
    keep = CONV_WIDTH - 1
    n = _seqs_per_step(db)
    vec = lambda: pl.BlockSpec((1, ch), lambda bi: (0, 0))
    out, st = pl.pallas_call(
        functools.partial(_conv_step_kernel, ch=ch),
        grid=(db // n,),
        in_specs=[pl.BlockSpec((n, 1, ch3), lambda bi: (bi, 0, 0)),
                  pl.BlockSpec((n, keep, ch), lambda bi: (bi, 0, 0)),
                  pl.BlockSpec((CONV_WIDTH, ch), lambda bi: (0, 0)),
                  vec(), vec(), vec()],
        out_specs=[pl.BlockSpec((n, 1, ch), lambda bi: (bi, 0, 0)),
                   pl.BlockSpec((n, keep, ch), lambda bi: (bi, 0, 0))],
        out_shape=[jax.ShapeDtypeStruct((db, 1, ch), F32),
                   jax.ShapeDtypeStruct((db, keep, ch), F32)],
        compiler_params=_params("parallel"),
        name="conv_sample",
    )(conv_in.reshape(db, 1, ch3), prev, w, b.reshape(1, ch), g.reshape(1, ch), beta.reshape(1, ch))
    return out.reshape(db, ch), st


def _rope_rows(x, cos, sin_signed):
    w = x.shape[-1]
    half = HEAD_DIM // 2
    lane = lax.broadcasted_iota(jnp.int32, (1, w), 1)
    partner = jnp.where((lane % HEAD_DIM) < half, pltpu.roll(x, w - half, 1), pltpu.roll(x, half, 1))
    return x * cos + partner * sin_signed


def _half_sums(x, first_half):
    a = jnp.sum(jnp.where(first_half, x, 0.0), axis=-1, keepdims=True)
    b = jnp.sum(jnp.where(first_half, 0.0, x), axis=-1, keepdims=True)
    return jnp.where(first_half, a, b)


RET_SEQS = 2
RET_STEP_CHUNKS = 8


def _ret_kernel(in_ref, r0_ref, cos_ref, sin_ref, dmat_ref, cross_ref, wk_ref, decay_ref,
                o_ref, rout_ref, r_scr, *, rw):
    in_refs = [in_ref.at[b] for b in range(RET_SEQS)]
    o_refs = [o_ref.at[b] for b in range(RET_SEQS)]
    c = pl.program_id(1)
    npair = rw // LANES

    @pl.when(c == 0)
    def _():
        r_scr[...] = r0_ref[...]

    lane = lax.broadcasted_iota(jnp.int32, (1, LANES), 1)
    first = lane < HEAD_DIM
    row = lax.broadcasted_iota(jnp.int32, (LANES, 1), 0)
    blockdiag = (row < HEAD_DIM) == first
    problems = [(b, p) for b in range(RET_SEQS) for p in range(npair)]
    lanes_of = lambda p: slice(p * LANES, (p + 1) * LANES)
    for ci in range(in_ref.shape[1] // RET_CHUNK):
        rows = slice(ci * RET_CHUNK, (ci + 1) * RET_CHUNK)
        cos, sin = cos_ref[rows, :], sin_ref[rows, :]
        q = [_rope_rows(r[rows, 0:rw], cos, sin) for r in in_refs]
        k = [_rope_rows(r[rows, rw:2 * rw], cos, sin) * (HEAD_DIM ** -0.5) for r in in_refs]
        qb = {(b, p): q[b][:, lanes_of(p)] for b, p in problems}
        kp = {(b, p): k[b][:, lanes_of(p)] for b, p in problems}
        vb = {(b, p): in_refs[b][rows, 2 * rw + p * LANES:2 * rw + (p + 1) * LANES].astype(BF16)
              for b, p in problems}
        kb = {pr: kp[pr].astype(BF16) for pr in problems}
        carry = {(b, p): jnp.dot(qb[b, p].astype(BF16), r_scr[b, p].astype(BF16), preferred_element_type=F32)
                 for b, p in problems}
        scores = {}
        for b, p in problems:
            for hh in range(PAIR):
                own = first if hh == 0 else jnp.logical_not(first)
                qh = jnp.where(own, qb[b, p], 0.0).astype(BF16)
                scores[b, p, hh] = lax.dot_general(qh, kb[b, p], _CONTRACT_LAST, preferred_element_type=F32)
        kv = {(b, p): lax.dot_general((kp[b, p] * wk_ref[p]).astype(BF16), vb[b, p], _CONTRACT_FIRST,
                                      preferred_element_type=F32) for b, p in problems}
        inner = {(b, p, hh): jnp.dot((scores[b, p, hh] * dmat_ref[PAIR * p + hh]).astype(BF16), vb[b, p],
                                     preferred_element_type=F32)
                 for b, p in problems for hh in range(PAIR)}
        for b, p in problems:
            o = carry[b, p] * cross_ref[p] + jnp.where(first, inner[b, p, 0], inner[b, p, 1])
            r_scr[b, p] = r_scr[b, p] * decay_ref[p] + jnp.where(blockdiag, kv[b, p], 0.0)
            ms = _half_sums(o * o, first) * (1.0 / HEAD_DIM)
            gate = in_refs[b][rows, 3 * rw + p * LANES:3 * rw + (p + 1) * LANES]
            o_refs[b][rows, lanes_of(p)] = (o * lax.rsqrt(ms + EPS) * _silu(gate)).astype(o_ref.dtype)

    @pl.when(c == pl.num_programs(1) - 1)
    def _():
        rout_ref[...] = r_scr[...]


def _ret_tables(heads):
    c = RET_CHUNK
    lg = jnp.log(1.0 - 2.0 ** (-5.0 - jnp.arange(heads, dtype=F32)))
    t = jnp.arange(c, dtype=F32)
    diff = t[:, None] - t[None, :]
    dmat = jnp.where(diff >= 0, jnp.exp(lg[:, None, None] * jnp.maximum(diff, 0.0)), 0.0)
    cross = jnp.exp(lg[None, :] * (t[:, None] + 1.0))
    wk = jnp.exp(lg[:, None] * (c - 1.0 - t[None, :]))
    chunk_decay = jnp.exp(lg * c)
    per_lane = lambda a: jnp.repeat(a, HEAD_DIM, axis=-1)
    to_pairs = lambda a: jnp.swapaxes(a.reshape(c, heads // PAIR, LANES), 0, 1)
    cross_p = to_pairs(per_lane(cross))
    wk_p = to_pairs(per_lane(wk.T))
    decay_p = to_pairs(per_lane(jnp.broadcast_to(chunk_decay[None, :], (c, heads))))
    return dmat, cross_p, wk_p, decay_p


def _unpair_states(pairs):
    a = pairs[:, :, :HEAD_DIM, :HEAD_DIM]
    b = pairs[:, :, HEAD_DIM:, HEAD_DIM:]
    return jnp.stack([a, b], axis=2).reshape(pairs.shape[0], -1, HEAD_DIM, HEAD_DIM)


def _ret_prompt(ret_in, state0_pairs, cos_t, sin_t, tables, *, batch, seq):
    rw = ret_in.shape[1] // 4
    npair = rw // LANES
    c = RET_CHUNK * RET_STEP_CHUNKS
    nc = seq // c
    assert seq % c == 0 and batch % RET_SEQS == 0
    dmat, cross_p, wk_p, decay_p = tables
    const = lambda a: pl.BlockSpec(a.shape, lambda bi, ci: (0,) * a.ndim)
    state_spec = pl.BlockSpec((RET_SEQS, npair, LANES, LANES), lambda bi, ci: (bi, 0, 0, 0))
    out, rout = pl.pallas_call(
        functools.partial(_ret_kernel, rw=rw),
        grid=(batch // RET_SEQS, nc),
        in_specs=[pl.BlockSpec((RET_SEQS, c, 4 * rw), lambda bi, ci: (bi, ci, 0)),
                  state_spec,
                  pl.BlockSpec((c, rw), lambda bi, ci: (ci, 0)),
                  pl.BlockSpec((c, rw), lambda bi, ci: (ci, 0)),
                  const(dmat), const(cross_p), const(wk_p), const(decay_p)],
        out_specs=[pl.BlockSpec((RET_SEQS, c, rw), lambda bi, ci: (bi, ci, 0)), state_spec],
        out_shape=[jax.ShapeDtypeStruct((batch, seq, rw), MIX_DTYPE),
                   jax.ShapeDtypeStruct((batch, npair, LANES, LANES), F32)],
        scratch_shapes=[pltpu.VMEM((RET_SEQS, npair, LANES, LANES), F32)],
        compiler_params=_params("parallel", "arbitrary"),
        name="ret_prompt",
    )(ret_in.reshape(batch, seq, 4 * rw), state0_pairs, cos_t, sin_t, dmat, cross_p, wk_p, decay_p)
    return out.reshape(batch * seq, rw), rout


def _ret_step_kernel(in_ref, r_ref, cos_ref, sin_ref, decay_ref, o_ref, rout_ref, *, rw):
    heads = rw // HEAD_DIM
    eye = (lax.broadcasted_iota(jnp.int32, (HEAD_DIM, HEAD_DIM), 0)
           == lax.broadcasted_iota(jnp.int32, (HEAD_DIM, HEAD_DIM), 1))
    col = lambda r: jnp.sum(jnp.where(eye, r, 0.0), axis=-1, keepdims=True)
    for s in range(in_ref.shape[0]):
        row = in_ref[s]
        q = _rope_rows(row[:, 0:rw], cos_ref[...], sin_ref[...])
        k = _rope_rows(row[:, rw:2 * rw], cos_ref[...], sin_ref[...]) * (HEAD_DIM ** -0.5)
        v = row[:, 2 * rw:3 * rw]
        gate = row[:, 3 * rw:4 * rw]
        outs = []
        for h in range(heads):
            sl = slice(h * HEAD_DIM, (h + 1) * HEAD_DIM)
            qh, kh, vh = q[:, sl], k[:, sl], v[:, sl]
            decay = decay_ref[h:h + 1, :]
            state = r_ref[s, h]
            o = jnp.sum(qh * kh, axis=-1, keepdims=True) * vh
            o = o + jnp.sum(col(qh) * state, axis=0, keepdims=True) * decay
            rout_ref[s, h] = state * decay + col(kh) * vh
            o = o * lax.rsqrt(jnp.mean(o * o, axis=-1, keepdims=True) + EPS)
            outs.append(o * _silu(gate[:, sl]))
        o_ref[s] = jnp.concatenate(outs, axis=-1)


def _ret_sample(ret_in, state, cos_row, sin_row, decay_rows):
    db, rw4 = ret_in.shape
    rw = rw4 // 4
    heads = rw // HEAD_DIM
    n = _seqs_per_step(db)
    out, rout = pl.pallas_call(
        functools.partial(_ret_step_kernel, rw=rw),
        grid=(db // n,),
        in_specs=[pl.BlockSpec((n, 1, rw4), lambda bi: (bi, 0, 0)),
                  pl.BlockSpec((n, heads, HEAD_DIM, HEAD_DIM), lambda bi: (bi, 0, 0, 0)),
                  pl.BlockSpec((1, rw), lambda bi: (0, 0)),
                  pl.BlockSpec((1, rw), lambda bi: (0, 0)),
                  pl.BlockSpec((heads, HEAD_DIM), lambda bi: (0, 0))],
        out_specs=[pl.BlockSpec((n, 1, rw), lambda bi: (bi, 0, 0)),
                   pl.BlockSpec((n, heads, HEAD_DIM, HEAD_DIM), lambda bi: (bi, 0, 0, 0))],
        out_shape=[jax.ShapeDtypeStruct((db, 1, rw), F32),
                   jax.ShapeDtypeStruct(state.shape, F32)],
        compiler_params=_params("parallel"),
        name="ret_sample",
    )(ret_in.reshape(db, 1, rw4), state, cos_row, sin_row, decay_rows)
    return out.reshape(db, rw), rout


def _rope_tables(pos, heads):
    half = HEAD_DIM // 2
    freqs = ROPE_BASE ** (-jnp.arange(half, dtype=F32) / half)
    ang = pos.astype(F32)[:, None] * freqs[None, :]
    cos, sin = jnp.cos(ang), jnp.sin(ang)
    cos_t = jnp.tile(jnp.concatenate([cos, cos], axis=-1), (1, heads))
    sin_t = jnp.tile(jnp.concatenate([-sin, sin], axis=-1), (1, heads))
    return cos_t, sin_t


def _rel_bucket(d):
    n = jnp.maximum(d, 0)
    max_exact = N_BUCKETS // 2
    large = max_exact + (jnp.log(jnp.maximum(n, 1).astype(F32) / max_exact)
                         / math.log(MAX_DISTANCE / max_exact) * (N_BUCKETS - max_exact)).astype(jnp.int32)
    large = jnp.minimum(large, N_BUCKETS - 1)
    return jnp.where(n < max_exact, n, large)


def _block_rank_penalty(scores, n_past):
    blk = lax.broadcasted_iota(jnp.int32, scores.shape, 0)
    past = blk < n_past
    if n_past <= TOP_K:
        return jnp.where(past, 0.0, MASKED)
    rank = jnp.zeros(scores.shape, F32)
    for m in range(n_past):
        other = scores[m:m + 1, :]
        beats = (other > scores) | ((other == scores) & (m < blk))
        rank += jnp.where(beats, 1.0, 0.0)
    return jnp.where(past & (rank < TOP_K), 0.0, MASKED)


def _moba_kernel(pt_ref, q_ref, kt_ref, vt_ref, km_ref, gate_ref, bias_ref, *refs, nb, n_step_pages,
                 pages_per_block):
    page_refs = refs[:n_step_pages]
    o_ref, ksum_ref, kext_scr, vext_scr, m_scr, acc_scr = refs[n_step_pages:]
    tq = MOBA_BLOCK
    nq = kt_ref.shape[-1] // tq
    lane = lax.broadcasted_iota(jnp.int32, (1, LANES), 1)
    row = lax.broadcasted_iota(jnp.int32, (LANES, 1), 0)
    km = km_ref[0]

    def build_block(c):
        cols = slice(c * tq, (c + 1) * tq)
        kt = kt_ref[0, 0, :, cols]
        vt = vt_ref[0, 0, :, cols]
        for hh in range(PAIR):
            own_rows = (row < HEAD_DIM) if hh == 0 else (row >= HEAD_DIM)
            flag_row = c + (HEAD_DIM if hh == 0 else 0)
            kext = jnp.where(own_rows, kt, jnp.where(row == flag_row, 1.0, 0.0))
            kext_scr[hh, cols, :] = jnp.transpose(kext).astype(BF16)
            vext_scr[hh, :, cols] = jnp.where(own_rows, vt, 1.0).astype(BF16)

    def extended_queries(t):
        q2 = q_ref[t * tq:(t + 1) * tq, :]
        q_t = jnp.transpose(q2) * (HEAD_DIM ** -0.5 * LOG2E)
        blk = lax.broadcasted_iota(jnp.int32, (nb, tq), 0)
        pad = jnp.zeros((HEAD_DIM, tq), F32)
        tail = jnp.zeros((HEAD_DIM - nb, tq), F32)
        out = []
        for hh in range(PAIR):
            own = (lane < HEAD_DIM) if hh == 0 else (lane >= HEAD_DIM)
            own_rows = (row < HEAD_DIM) if hh == 0 else (row >= HEAD_DIM)
            if t > TOP_K:
                scores = lax.dot_general(jnp.where(own, km, 0.0), q2, _CONTRACT_LAST,
                                         precision=lax.Precision.HIGHEST, preferred_element_type=F32)
            else:
                scores = jnp.zeros((nb, tq), F32)
            pen = jnp.where(blk == t, 0.0, _block_rank_penalty(scores, t))
            pen_rows = jnp.concatenate([pad, pen, tail] if hh == 0 else [pen, tail, pad], axis=0)
            out.append(jnp.where(own_rows, q_t, pen_rows).astype(BF16))
        return out

    def steps_of(t):
        pairs = [(c, 2) for c in range(t - 1, -1, -2)]
        return pairs + ([(0, 1)] if t % 2 == 0 else [])

    schedule = [(t, c, n) for t in range(nq) for c, n in steps_of(t)]
    q_ext = {}

    def logits(t, c, n):
        if t not in q_ext:
            build_block(t)
            q_ext.clear()
            q_ext[t] = extended_queries(t)
        out = []
        for hh in range(PAIR):
            s = jnp.dot(kext_scr[hh, c * tq:(c + n) * tq, :], q_ext[t][hh], preferred_element_type=F32)
            blocks = [s[i * tq:(i + 1) * tq, :] + bias_ref[hh, min(t - c - i, 2)] for i in range(n)]
            top = functools.reduce(jnp.maximum, [jnp.max(blk, axis=0, keepdims=True) for blk in blocks])
            out.append((blocks, top))
        return out

    def softmax_pv(c, n, s_pair, first):
        for hh in range(PAIR):
            blocks, top = s_pair[hh]
            m_new = top if first else jnp.maximum(m_scr[hh], top)
            probs = jnp.concatenate([jnp.exp2((blk - m_new).astype(BF16)) for blk in blocks], axis=0)
            pv = jnp.dot(vext_scr[hh, :, c * tq:(c + n) * tq], probs, preferred_element_type=F32)
            acc_scr[hh] = pv if first else jnp.exp2(m_scr[hh] - m_new) * acc_scr[hh] + pv
            m_scr[hh] = m_new

    def sum_page_block(j):
        pages = page_refs[j * pages_per_block:(j + 1) * pages_per_block]
        for r0 in range(0, ksum_ref.shape[2], HEAD_DIM):
            rows = slice(r0, r0 + HEAD_DIM)
            total = functools.reduce(jnp.add, [pg[0, rows, :] for pg in pages])
            ksum_ref[0, 0, rows, j:j + 1] = jnp.sum(total, axis=-1, keepdims=True)

    n_page_blocks = n_step_pages // pages_per_block
    ksum_ref[...] = jnp.zeros(ksum_ref.shape, F32)
    per_step = -(-n_page_blocks // len(schedule))

    s_cur = logits(*schedule[0])
    for idx, (t, c, n) in enumerate(schedule):
        s_next = logits(*schedule[idx + 1]) if idx + 1 < len(schedule) else None
        for j in range(idx * per_step, min((idx + 1) * per_step, n_page_blocks)):
            sum_page_block(j)
        softmax_pv(c, n, s_cur, first=(c + n == t + 1))
        if c == 0:
            numer = jnp.where(row < HEAD_DIM, acc_scr[0], acc_scr[1])
            denom = jnp.where(row < HEAD_DIM, pltpu.roll(acc_scr[0], HEAD_DIM, 0),
                              pltpu.roll(acc_scr[1], HEAD_DIM, 0))
            rows = slice(t * tq, (t + 1) * tq)
            o_ref[rows, :] = (jnp.transpose(numer / denom) * _silu(gate_ref[rows, :])).astype(o_ref.dtype)
        s_cur = s_next


def _toeplitz(vec, rows, cols):
    h, n = vec.shape
    assert cols <= n - 1 and rows <= n
    flat = jnp.tile(vec, (1, rows))[:, :rows * (n - 1)]
    return flat.reshape(h, rows, n - 1)[:, :, :cols]


def _moba_bias_tiles(rel_bias):
    blk = MOBA_BLOCK
    bias_t = rel_bias.astype(F32).T
    tiles = []
    n = 2 * blk - 1
    k = jnp.arange(n)
    col_minus_row = jnp.where(k < blk, k, k - n)
    for dist in range(3):
        d = dist * blk + col_minus_row
        g = bias_t[:, _rel_bucket(d)]
        if dist == 0:
            g = jnp.where((d >= 0)[None], g, -jnp.inf)
        tiles.append(_toeplitz(g, blk, blk))
    return jnp.stack(tiles, axis=1) * LOG2E


def _moba_prompt(mq, kt_buf, vt_buf, kmean, mgate, bias_tiles, pt_flat, cache_pages, *, layer, batch, seq,
                 page0, n_pages):
    mw = mq.shape[1]
    npair = mw // LANES
    nq = seq // MOBA_BLOCK
    nb = kmean.shape[1]
    assert seq % MOBA_BLOCK == 0 and nb % 8 == 0 and 2 <= nq <= nb <= HEAD_DIM
    assert MOBA_BLOCK + 1 >= MAX_DISTANCE
    width, page_size = cache_pages.shape[1:]
    pages_per_block = MOBA_BLOCK // page_size
    steps = batch * npair
    step_pages = pt_flat.shape[0] // steps
    assert pt_flat.shape[0] % steps == 0 and n_pages % step_pages == 0 and step_pages % pages_per_block == 0
    assert step_pages // pages_per_block <= LANES and page_size == LANES
    parts = n_pages // step_pages
    step_of = lambda b, p: b * npair + p
    rows = lambda: pl.BlockSpec((seq, LANES), lambda b, p, pt: (b, p))
    cols = lambda: pl.BlockSpec((1, 1, LANES, seq), lambda b, p, pt: (layer, b, p, 0))

    def page_spec(i):
        return pl.BlockSpec((1, width, page_size),
                            lambda b, p, pt: (page0 + pt[step_of(b, p) * step_pages + i], 0, 0))

    grid_spec = pltpu.PrefetchScalarGridSpec(
        num_scalar_prefetch=1,
        grid=(batch, npair),
        in_specs=[rows(), cols(), cols(),
                  pl.BlockSpec((1, nb, LANES), lambda b, p, pt: (b, 0, p)),
                  rows(),
                  pl.BlockSpec((PAIR, 3, MOBA_BLOCK, MOBA_BLOCK), lambda b, p, pt: (p, 0, 0, 0))]
                 + [page_spec(i) for i in range(step_pages)],
        out_specs=[rows(),
                   pl.BlockSpec((1, 1, width, LANES),
                                lambda b, p, pt: (step_of(b, p) // parts, step_of(b, p) % parts, 0, 0))],
        scratch_shapes=[pltpu.VMEM((PAIR, seq, LANES), BF16),
                        pltpu.VMEM((PAIR, LANES, seq), BF16),
                        pltpu.VMEM((PAIR, 1, MOBA_BLOCK), F32),
                        pltpu.VMEM((PAIR, LANES, MOBA_BLOCK), F32)],
    )
    return pl.pallas_call(
        functools.partial(_moba_kernel, nb=nb, n_step_pages=step_pages, pages_per_block=pages_per_block),
        grid_spec=grid_spec,
        out_shape=[jax.ShapeDtypeStruct((batch * seq, mw), MIX_DTYPE),
                   jax.ShapeDtypeStruct((pt_flat.shape[0] // n_pages, parts, width, LANES), F32)],
        compiler_params=_params("parallel", "parallel"),
        name="moba_prompt",
    )(pt_flat, mq, kt_buf, vt_buf, kmean, mgate, bias_tiles, *([cache_pages] * step_pages))


def _block_score_kernel(q_ref, ksum_ref, s_ref, *, heads, blocks_per_part):
    lane = lax.broadcasted_iota(jnp.int32, (1, LANES), 1)
    head_row = lax.broadcasted_iota(jnp.int32, (heads, 1), 0)
    for s in range(q_ref.shape[0]):
        scores = jnp.zeros((heads, LANES), F32)
        for r in range(ksum_ref.shape[1]):
            in_part = (lane >= r * blocks_per_part) & (lane < (r + 1) * blocks_per_part)
            for h in range(heads):
                rows = slice(h * HEAD_DIM, (h + 1) * HEAD_DIM)
                part = jnp.sum(ksum_ref[s, r, rows, :] * q_ref[s, rows, :], axis=0, keepdims=True)
                if r:
                    part = pltpu.roll(part, r * blocks_per_part, 1)
                scores = jnp.where((head_row == h) & in_part, part, scores)
        s_ref[s] = scores * (1.0 / MOBA_BLOCK)


def _block_scores(q_rep, ksum, *, blocks_per_part):
    db, width, _ = q_rep.shape
    heads = width // HEAD_DIM
    parts = ksum.shape[1]
    n = _seqs_per_step(db)
    assert parts * blocks_per_part <= LANES
    return pl.pallas_call(
        functools.partial(_block_score_kernel, heads=heads, blocks_per_part=blocks_per_part),
        grid=(db // n,),
        in_specs=[pl.BlockSpec((n, width, LANES), lambda b: (b, 0, 0)),
                  pl.BlockSpec((n, parts, width, LANES), lambda b: (b, 0, 0, 0))],
        out_specs=pl.BlockSpec((n, heads, LANES), lambda b: (b, 0, 0)),
        out_shape=jax.ShapeDtypeStruct((db, heads, LANES), F32),
        compiler_params=_params("parallel"),
        name="block_scores",
    )(q_rep, ksum)


def _topk_kernel(s_ref, sel_ref, *, n_blocks):
    s = s_ref[...]
    lane = lax.broadcasted_iota(jnp.int32, s.shape, 1).astype(F32)
    s = jnp.where(lane < n_blocks, s, -jnp.inf)
    sel = jnp.zeros(s.shape, F32)
    for j in range(TOP_K):
        best = jnp.max(s, axis=-1, keepdims=True)
        idx = jnp.min(jnp.where(s == best, lane, float(LANES)), axis=-1, keepdims=True)
        sel = jnp.where(lane == j, idx, sel)
        s = jnp.where(lane == idx, -jnp.inf, s)
    sel_ref[...] = sel.astype(jnp.int32)


def _topk_blocks(scores, *, n_blocks):
    db, heads, _ = scores.shape
    assert n_blocks >= TOP_K
    spec = pl.BlockSpec((db * heads, LANES), lambda i: (0, 0))
    return pl.pallas_call(
        functools.partial(_topk_kernel, n_blocks=n_blocks),
        grid=(1,),
        in_specs=[spec],
        out_specs=spec,
        out_shape=jax.ShapeDtypeStruct((db * heads, LANES), jnp.int32),
        compiler_params=_params("arbitrary"),
        name="topk_blocks",
    )(scores.reshape(db * heads, LANES)).reshape(db, heads, LANES)


ATTEND_HEADS = 4


def _page_attend_kernel(pt_ref, seqp_ref, q_ref, kn_ref, vn_ref, gate_ref, bown_ref, bias_ref, *refs,
                        heads, n_sel_pages):
    n_tiles = ATTEND_HEADS * n_sel_pages
    k_refs, v_refs, o_ref = refs[:n_tiles], refs[n_tiles:2 * n_tiles], refs[2 * n_tiles]
    b = pl.program_id(0)
    h0 = pl.program_id(1) * ATTEND_HEADS
    diag = (lax.broadcasted_iota(jnp.int32, (HEAD_DIM, LANES), 0)
            == lax.broadcasted_iota(jnp.int32, (HEAD_DIM, LANES), 1))
    for i in range(ATTEND_HEADS):
        h = h0 + i
        dims = slice(i * HEAD_DIM, (i + 1) * HEAD_DIM)
        q = q_ref[0, dims, :] * (HEAD_DIM ** -0.5)
        logits = []
        for j in range(n_sel_pages):
            seq_page = seqp_ref[(b * heads + h) * n_sel_pages + j]
            bias = bias_ref[h, pl.ds(seq_page, 1), :]
            logits.append(jnp.sum(k_refs[i * n_sel_pages + j][0] * q, axis=0, keepdims=True) + bias)
        s_own = jnp.sum(kn_ref[0, dims, :] * q, axis=0, keepdims=True) + bown_ref[pl.ds(h, 1), :]
        top = functools.reduce(jnp.maximum, logits)
        m = jnp.maximum(jnp.max(top, axis=-1, keepdims=True), s_own)
        p_own = jnp.exp(s_own - m)
        psum = jnp.zeros_like(m)
        acc = jnp.zeros((HEAD_DIM, LANES), F32)
        for j in range(n_sel_pages):
            p = jnp.exp(logits[j] - m)
            psum += p
            acc += v_refs[i * n_sel_pages + j][0] * p
        denom = jnp.sum(psum, axis=-1, keepdims=True) + p_own
        out_rep = (jnp.sum(acc, axis=-1, keepdims=True) + p_own * vn_ref[0, dims, :]) / denom
        out_row = jnp.sum(jnp.where(diag, out_rep, 0.0), axis=0, keepdims=True)[:, :HEAD_DIM]
        o_ref[0, pl.ds(h, 1), :] = out_row * _silu(gate_ref[0, pl.ds(h, 1), :])


def _page_attend(pt_flat, seq_pages, q_rep, kn_rep, vn_rep, gate8, bown, bias_rows, k_tiles, v_tiles,
                 *, n_sel_pages, n_pages, page0):
    db, heads, _ = gate8.shape
    page_size = k_tiles.shape[-1]
    assert heads % ATTEND_HEADS == 0
    head_rep = lambda: pl.BlockSpec((1, ATTEND_HEADS * HEAD_DIM, LANES), lambda b, g, pt, sp: (b, g, 0))
    whole = lambda a: pl.BlockSpec(a.shape, lambda b, g, pt, sp: (0,) * a.ndim)

    def tile_spec(i, j):
        def index(b, g, pt, sp):
            h = g * ATTEND_HEADS + i
            page = page0 + pt[b * n_pages + sp[(b * heads + h) * n_sel_pages + j]]
            return (page * heads + h, 0, 0)
        return pl.BlockSpec((1, HEAD_DIM, page_size), index)

    tile_specs = [tile_spec(i, j) for i in range(ATTEND_HEADS) for j in range(n_sel_pages)]
    grid_spec = pltpu.PrefetchScalarGridSpec(
        num_scalar_prefetch=2,
        grid=(db, heads // ATTEND_HEADS),
        in_specs=[head_rep(), head_rep(), head_rep(),
                  pl.BlockSpec((1, heads, HEAD_DIM), lambda b, g, pt, sp: (b, 0, 0)),
                  whole(bown), whole(bias_rows)] + tile_specs * 2,
        out_specs=pl.BlockSpec((1, heads, HEAD_DIM), lambda b, g, pt, sp: (b, 0, 0)),
    )
    return pl.pallas_call(
        functools.partial(_page_attend_kernel, heads=heads, n_sel_pages=n_sel_pages),
        grid_spec=grid_spec,
        out_shape=jax.ShapeDtypeStruct((db, heads, HEAD_DIM), F32),
        compiler_params=_params("parallel", "arbitrary"),
        name="page_attend",
    )(pt_flat, seq_pages, q_rep, kn_rep, vn_rep, gate8, bown, bias_rows,
      *([k_tiles] * len(tile_specs)), *([v_tiles] * len(tile_specs)))


def _sample_bias_rows(rel_bias, *, past_len, page_size):
    kpos = jnp.arange(past_len)
    onehot = (_rel_bucket(past_len - kpos)[:, None] == jnp.arange(N_BUCKETS)[None, :]).astype(F32)
    bias = jnp.dot(onehot, rel_bias.astype(F32), precision=lax.Precision.HIGHEST)
    return bias.T.reshape(rel_bias.shape[1], past_len // page_size, page_size)


def _lane_replicated(x):
    return jnp.broadcast_to(x[:, :, None], x.shape + (LANES,))


def _moba_sample(mq, mk, mv, mgate, ksum, k_cache_t, v_cache_t, page0, page_table, rel_bias, bias_rows, *,
                 page_size):
    db, mw = mq.shape
    heads = mw // HEAD_DIM
    n_pages = page_table.shape[1]
    past_len = n_pages * page_size
    pages_per_block = MOBA_BLOCK // page_size
    n_blocks = past_len // MOBA_BLOCK
    assert past_len % MOBA_BLOCK == 0
    n_rows = k_cache_t.shape[0]
    pt_flat = page_table.reshape(-1)
    q_rep = _lane_replicated(mq)
    scores = _block_scores(q_rep, ksum, blocks_per_part=n_blocks // ksum.shape[1])
    sel = _topk_blocks(scores, n_blocks=n_blocks)[:, :, :TOP_K]
    seq_pages = (sel[..., None] * pages_per_block + jnp.arange(pages_per_block, dtype=jnp.int32)).reshape(-1)
    bown = jnp.broadcast_to(rel_bias.astype(F32)[0][:, None], (heads, LANES))
    out = _page_attend(pt_flat, seq_pages, q_rep, _lane_replicated(mk), _lane_replicated(mv),
                       mgate.reshape(db, heads, HEAD_DIM), bown, bias_rows,
                       k_cache_t.reshape(n_rows * heads, HEAD_DIM, page_size),
                       v_cache_t.reshape(n_rows * heads, HEAD_DIM, page_size),
                       n_sel_pages=TOP_K * pages_per_block, n_pages=n_pages, page0=page0)
    return out.reshape(db, mw)


PROMPT_TM = 512
OUTPROJ_TM = 1024
CONV_TS = 1024


def kernel(x_prompt, x_sample, cache_k, cache_v, page_table, state_ret, state_conv, norm_g, w_in, conv_w,
           conv_b, conv_ln_g, conv_ln_b, w_out, rel_bias, final_g):
    batch, seq, d = x_prompt.shape
    db, dec_seq, _ = x_sample.shape
    assert dec_seq == 1, "the sample path handles one new token per sequence"
    depth = w_in.shape[0]
    ret_heads = d // 256
    moba_heads = d // 128
    conv_ch = d // 4
    n_pages = page_table.shape[1]
    page_size = cache_k.shape[2]
    past_len = n_pages * page_size
    n_phys = cache_k.shape[1]
    tokens_minor = lambda c: jnp.transpose(c, (0, 1, 3, 4, 2)).reshape(depth * n_phys, moba_heads, HEAD_DIM,
                                                                         page_size)
    k_cache_t, v_cache_t = tokens_minor(cache_k), tokens_minor(cache_v)

    mw = moba_heads * HEAD_DIM
    k0 = 3 * conv_ch + 4 * ret_heads * HEAD_DIM + mw
    w_rows = jnp.concatenate([w_in[:, :, :k0], w_in[:, :, k0 + 2 * mw:]], axis=-1).astype(BF16)
    w_kv_t = jnp.swapaxes(w_in[:, :, k0:k0 + 2 * mw], 1, 2).astype(BF16)
    w_in_b = w_in.astype(BF16)
    w_out_b = w_out.astype(BF16)
    cos_p, sin_p = _rope_tables(jnp.arange(seq, dtype=jnp.int32), ret_heads)
    cos_s, sin_s = _rope_tables(past_len + jnp.arange(1, dtype=jnp.int32), ret_heads)
    ret_tables = _ret_tables(ret_heads)
    lg = jnp.log(1.0 - 2.0 ** (-5.0 - jnp.arange(ret_heads, dtype=F32)))
    decay_rows = jnp.broadcast_to(jnp.exp(lg)[:, None], (ret_heads, HEAD_DIM))
    bias_tiles = _moba_bias_tiles(rel_bias)
    bias_rows = _sample_bias_rows(rel_bias, past_len=past_len, page_size=page_size)
    conv0 = jnp.zeros((batch, CONV_WIDTH - 1, conv_ch), F32)
    ret0 = jnp.zeros((batch, ret_heads // PAIR, LANES, LANES), F32)
    nb = seq // MOBA_BLOCK
    nb_pad = -(-nb // 8) * 8

    hp = x_prompt.reshape(batch * seq, d)
    hs = x_sample.reshape(db, d)
    outs = {name: [] for name in ("ks", "vs", "rp", "rs", "cp", "cs")}
    kt_buf = vt_buf = None
    for l in range(depth):
        last = l == depth - 1
        conv_in, ret_in, mq, mgate, kt_buf, vt_buf, kmean = _inproj_prompt(
            hp, norm_g[l], w_rows[l], w_kv_t[l], kt_buf, vt_buf, layer=l, depth=depth, batch=batch,
            seq=seq, tm=PROMPT_TM)
        conv_out, conv_state = _conv_prompt(conv_in, conv0, conv_w[l], conv_b[l], conv_ln_g[l], conv_ln_b[l],
                                            batch=batch, seq=seq, ts=CONV_TS)
        ret_out, ret_state = _ret_prompt(ret_in, ret0, cos_p, sin_p, ret_tables, batch=batch, seq=seq)
        kmean = jnp.pad(kmean, ((0, 0), (0, nb_pad - nb), (0, 0)))
        moba_out, ksum = _moba_prompt(mq, kt_buf, vt_buf, kmean, mgate, bias_tiles, page_table.reshape(-1),
                                      k_cache_t.reshape(depth * n_phys, mw, page_size), layer=l, batch=batch,
                                      seq=seq, page0=l * n_phys, n_pages=n_pages)
        hp = _outproj(hp, conv_out, ret_out, moba_out, w_out_b[l], final_g, tm=math.gcd(OUTPROJ_TM, batch * seq),
                      final_norm=last)
        outs["rp"].append(_unpair_states(ret_state))
        outs["cp"].append(conv_state)
        conv_in, ret_in, mq, mk, mv, mgate = _inproj(hs, norm_g[l], w_in_b[l])
        conv_out, conv_state = _conv_sample(conv_in, state_conv[l], conv_w[l], conv_b[l], conv_ln_g[l],
                                            conv_ln_b[l])
        ret_out, ret_state = _ret_sample(ret_in, state_ret[l], cos_s, sin_s, decay_rows)
        moba_out = _moba_sample(mq, mk, mv, mgate, ksum, k_cache_t, v_cache_t, l * n_phys, page_table, rel_bias,
                                bias_rows, page_size=page_size)
        hs = _outproj(hs, conv_out, ret_out, moba_out, w_out_b[l], final_g, tm=db, final_norm=last)
        outs["ks"].append(mk.reshape(db, 1, moba_heads, HEAD_DIM))
        outs["vs"].append(mv.reshape(db, 1, moba_heads, HEAD_DIM))
        outs["rs"].append(ret_state)
        outs["cs"].append(conv_state)

    st = {name: jnp.stack(vals) for name, vals in outs.items()}
    rows_major = lambda t: jnp.transpose(t.reshape(depth, batch, moba_heads, HEAD_DIM, seq), (0, 1, 4, 2, 3))
    return (hp.reshape(batch, seq, d), hs.reshape(db, 1, d), rows_major(kt_buf), rows_major(vt_buf),
            st["ks"], st["vs"], st["rp"], st["rs"], st["cp"], st["cs"])
```
